```python
import jax
import jax.numpy as jnp
from jax import lax
import numpy as np

D_MODEL = 2048
BATCH = 4
SEQ = 4096
DEPTH = 4

SB_HEADS = 8
SB_HEAD_DIM = D_MODEL // 16
SB_WIDTH = SB_HEADS * SB_HEAD_DIM
SB_BLOCK = 128
CONV_WIDTH = D_MODEL // 4
CONV_K = 3
GLA_HEADS = 4
GLA_DK = D_MODEL // 32
GLA_DV = D_MODEL // 16
GLA_KW = GLA_HEADS * GLA_DK
GLA_VW = GLA_HEADS * GLA_DV
GLA_RANK = 16
GLA_TAU = 16.0
GLA_CHUNK = 16
D_MIX = SB_WIDTH + CONV_WIDTH + GLA_VW
IN_SPLIT_WIDTHS = (SB_WIDTH,) * 3 + (CONV_WIDTH,) * 3 + (GLA_KW, GLA_KW, GLA_VW, GLA_VW, GLA_RANK)
IN_COLS = sum(IN_SPLIT_WIDTHS)
IN_SPLIT_POINTS = tuple(np.cumsum(IN_SPLIT_WIDTHS)[:-1].tolist())
D_FF = 256 * ((8 * D_MODEL + 3 * 256 - 1) // (3 * 256))
N_EXPERTS = 8
TOP_K = 2
N_DENSE = (DEPTH + 1) // 2
N_MOE = DEPTH // 2
DEEPNORM_ALPHA = (2 * DEPTH) ** 0.25
DEEPNORM_BETA = (8 * DEPTH) ** -0.25
LN_EPS = 1e-5
RMS_EPS = 1e-6

kernel_name = "hybrid_sb_conv_gla_deepnorm_moe"


def layer_norm(x, g, b):
    xf = x.astype(jnp.float32)
    mu = jnp.mean(xf, axis=-1, keepdims=True)
    xc = xf - mu
    var = jnp.mean(xc * xc, axis=-1, keepdims=True)
    return (xc * lax.rsqrt(var + LN_EPS) * g.astype(jnp.float32) + b.astype(jnp.float32)).astype(x.dtype)


def rms_norm(x, g):
    xf = x.astype(jnp.float32)
    return xf * lax.rsqrt(jnp.mean(xf * xf, axis=-1, keepdims=True) + RMS_EPS) * g.astype(jnp.float32)


def stick_breaking_attention(q, k, v):
    b_, s_, h_, dh = q.shape
    nb = s_ // SB_BLOCK
    qf = (q.astype(jnp.float32) * dh ** -0.5).transpose(0, 2, 1, 3)
    kf = k.astype(jnp.float32).transpose(0, 2, 1, 3)
    vf = v.astype(jnp.float32).transpose(0, 2, 1, 3)
    q_blocks = qf.reshape(b_, h_, nb, SB_BLOCK, dh).transpose(2, 0, 1, 3, 4)
    key_pos = jnp.arange(s_, dtype=jnp.int32)

    def one_block(args):
        qb, start = args
        z = jnp.einsum('bhqd,bhkd->bhqk', qb, kf)
        q_pos = start + jnp.arange(SB_BLOCK, dtype=jnp.int32)
        mask = key_pos[None, :] < q_pos[:, None]
        log_1m_beta = jnp.where(mask, jax.nn.log_sigmoid(-z), 0.0)
        tail = lax.cumsum(log_1m_beta, axis=3, reverse=True) - log_1m_beta
        a = jnp.where(mask, jnp.exp(jax.nn.log_sigmoid(z) + tail), 0.0)
        return jnp.einsum('bhqk,bhkd->bhqd', a, vf)

    starts = jnp.arange(nb, dtype=jnp.int32) * SB_BLOCK
    o = lax.map(one_block, (q_blocks, starts))
    return o.transpose(1, 0, 3, 2, 4).reshape(b_, s_, h_, dh)


def causal_depthwise_conv(u, w):
    s_ = u.shape[1]
    up = jnp.pad(u, ((0, 0), (CONV_K - 1, 0), (0, 0)))
    y = w[0] * up[:, 0:s_]
    for i in range(1, CONV_K):
        y = y + w[i] * up[:, i:i + s_]
    return y


def gla_chunked(q, k, v, log_a):
    b_, s_, h_, dk = q.shape
    dv = v.shape[-1]
    c = GLA_CHUNK
    n = s_ // c

    def chunks(t):
        return t.astype(jnp.float32).reshape(b_, n, c, h_, t.shape[-1]).transpose(0, 3, 1, 2, 4)

    qc = chunks(q) * dk ** -0.5
    kc = chunks(k)
    vc = chunks(v)
    bcum = jnp.cumsum(chunks(log_a), axis=3)
    idx = jnp.arange(c)
    causal = idx[:, None] >= idx[None, :]
    diff = bcum[:, :, :, :, None, :] - bcum[:, :, :, None, :, :]
    decay = jnp.where(causal[:, :, None], jnp.exp(jnp.minimum(diff, 0.0)), 0.0)
    attn = jnp.einsum('bhnqd,bhnkd,bhnqkd->bhnqk', qc, kc, decay)
    o_intra = jnp.einsum('bhnqk,bhnkv->bhnqv', attn, vc)

    b_last = bcum[:, :, :, -1:, :]
    q_in = qc * jnp.exp(bcum)
    k_up = kc * jnp.exp(b_last - bcum)
    g_chunk = jnp.exp(b_last[:, :, :, 0, :])

    def step(state, xs):
        q_n, k_n, v_n, g_n = xs
        o_n = jnp.einsum('bhqd,bhdv->bhqv', q_n, state)
        state = g_n[..., None] * state + jnp.einsum('bhkd,bhkv->bhdv', k_n, v_n)
        return state, o_n

    xs = (jnp.moveaxis(q_in, 2, 0), jnp.moveaxis(k_up, 2, 0), jnp.moveaxis(vc, 2, 0), jnp.moveaxis(g_chunk, 2, 0))
    state0 = jnp.zeros((b_, h_, dk, dv), jnp.float32)
    _, o_inter = lax.scan(step, state0, xs)
    o = o_intra + jnp.moveaxis(o_inter, 0, 2)
    return o.transpose(0, 2, 3, 1, 4).reshape(b_, s_, h_, dv)


def hybrid_mixer(x, w_in, sb_norm_g, conv_w, gla_w_up, gla_b, gla_norm_g, w_out):
    b_, s_, _ = x.shape
    h = x @ w_in
    sb_q, sb_k, sb_v, cv_b, cv_c, cv_h, g_q, g_k, g_v, g_g, g_r = jnp.split(h, IN_SPLIT_POINTS, axis=-1)

    def sb_heads(t):
        return t.reshape(b_, s_, SB_HEADS, SB_HEAD_DIM)
    o_sb = stick_breaking_attention(sb_heads(sb_q), sb_heads(sb_k), sb_heads(sb_v))
    o_sb = rms_norm(o_sb, sb_norm_g).reshape(b_, s_, SB_WIDTH).astype(x.dtype)

    o_cv = cv_b * causal_depthwise_conv(cv_c * cv_h, conv_w)

    log_a = jax.nn.log_sigmoid((g_r @ gla_w_up + gla_b).astype(jnp.float32)) / GLA_TAU
    o_gla = gla_chunked(g_q.reshape(b_, s_, GLA_HEADS, GLA_DK),
                        g_k.reshape(b_, s_, GLA_HEADS, GLA_DK),
                        g_v.reshape(b_, s_, GLA_HEADS, GLA_DV),
                        log_a.reshape(b_, s_, GLA_HEADS, GLA_DK))
    o_gla = rms_norm(o_gla, gla_norm_g).reshape(b_, s_, GLA_VW) * jax.nn.silu(g_g.astype(jnp.float32))
    o_gla = o_gla.astype(x.dtype)

    mixed = jnp.concatenate([o_sb, o_cv, o_gla], axis=-1)
    return mixed @ w_out


def swiglu(x, w_gate, w_up, w_down):
    return (jax.nn.silu(x @ w_gate) * (x @ w_up)) @ w_down


def moe_swiglu(x, w_router, w_gate, w_up, w_down):
    logits = (x @ w_router).astype(jnp.float32)
    top_v, top_i = lax.top_k(logits, TOP_K)
    top_w = jax.nn.softmax(top_v, axis=-1)
    gates = jnp.sum(jax.nn.one_hot(top_i, N_EXPERTS, dtype=jnp.float32) * top_w[..., None], axis=-2)
    gates = gates.astype(x.dtype)
    y = gates[..., 0:1] * swiglu(x, w_gate[0], w_up[0], w_down[0])
    for e in range(1, N_EXPERTS):
        y = y + gates[..., e:e + 1] * swiglu(x, w_gate[e], w_up[e], w_down[e])
    return y


def setup_inputs(seed: int = 0) -> dict:
    key = jax.random.key(seed)
    ks = jax.random.split(key, 19)
    f32 = jnp.float32

    def nrm(k, shape, scale):
        return jax.random.normal(k, shape, f32) * scale

    return {
        "x": nrm(ks[0], (BATCH, SEQ, D_MODEL), 1.0),
        "w_in": nrm(ks[1], (DEPTH, D_MODEL, IN_COLS), D_MODEL ** -0.5),
        "sb_norm_g": 1.0 + nrm(ks[2], (DEPTH, SB_HEAD_DIM), 0.02),
        "conv_w": nrm(ks[3], (DEPTH, CONV_K, CONV_WIDTH), CONV_K ** -0.5),
        "gla_w_up": nrm(ks[4], (DEPTH, GLA_RANK, GLA_KW), GLA_RANK ** -0.5),
        "gla_b": nrm(ks[5], (DEPTH, GLA_KW), 0.1),
        "gla_norm_g": 1.0 + nrm(ks[6], (DEPTH, GLA_DV), 0.02),
        "w_out": nrm(ks[7], (DEPTH, D_MIX, D_MODEL), DEEPNORM_BETA * D_MIX ** -0.5),
        "ln1_g": 1.0 + nrm(ks[8], (DEPTH, D_MODEL), 0.02),
        "ln1_b": nrm(ks[9], (DEPTH, D_MODEL), 0.02),
        "ffn_w_gate": nrm(ks[10], (N_DENSE, D_MODEL, D_FF), D_MODEL ** -0.5),
        "ffn_w_up": nrm(ks[11], (N_DENSE, D_MODEL, D_FF), D_MODEL ** -0.5),
        "ffn_w_down": nrm(ks[12], (N_DENSE, D_FF, D_MODEL), DEEPNORM_BETA * D_FF ** -0.5),
        "moe_w_router": nrm(ks[13], (N_MOE, D_MODEL, N_EXPERTS), D_MODEL ** -0.5),
        "moe_w_gate": nrm(ks[14], (N_MOE, N_EXPERTS, D_MODEL, D_FF), D_MODEL ** -0.5),
        "moe_w_up": nrm(ks[15], (N_MOE, N_EXPERTS, D_MODEL, D_FF), D_MODEL ** -0.5),
        "moe_w_down": nrm(ks[16], (N_MOE, N_EXPERTS, D_FF, D_MODEL), DEEPNORM_BETA * D_FF ** -0.5),
        "ln2_g": 1.0 + nrm(ks[17], (DEPTH, D_MODEL), 0.02),
        "ln2_b": nrm(ks[18], (DEPTH, D_MODEL), 0.02),
    }


def reference(x, w_in, sb_norm_g, conv_w, gla_w_up, gla_b, gla_norm_g, w_out, ln1_g, ln1_b,
              ffn_w_gate, ffn_w_up, ffn_w_down, moe_w_router, moe_w_gate, moe_w_up, moe_w_down,
              ln2_g, ln2_b):
    for layer in range(DEPTH):
        mix = hybrid_mixer(x, w_in[layer], sb_norm_g[layer], conv_w[layer], gla_w_up[layer],
                           gla_b[layer], gla_norm_g[layer], w_out[layer])
        x = layer_norm(DEEPNORM_ALPHA * x + mix, ln1_g[layer], ln1_b[layer])
        j = layer // 2
        if layer % 2 == 0:
            f = swiglu(x, ffn_w_gate[j], ffn_w_up[j], ffn_w_down[j])
        else:
            f = moe_swiglu(x, moe_w_router[j], moe_w_gate[j], moe_w_up[j], moe_w_down[j])
        x = layer_norm(DEEPNORM_ALPHA * x + f, ln2_g[layer], ln2_b[layer])
    return x
```

```python
import functools

import jax
import jax.numpy as jnp
from jax import lax
from jax.experimental import pallas as pl
from jax.experimental.pallas import tpu as pltpu

F32 = jnp.float32
BF16 = jnp.bfloat16

D_MODEL = 2048
SB_HEADS = 8
SB_DH = 128
SB_WIDTH = SB_HEADS * SB_DH
CONV_WIDTH = 512
CONV_K = 3
GLA_HEADS = 4
GLA_DK = 64
GLA_DV = 128
GLA_KW = GLA_HEADS * GLA_DK
GLA_VW = GLA_HEADS * GLA_DV
GLA_RANK = 16
GLA_TAU = 16.0
GLA_CHUNK = 16
IN_COLS = 3 * SB_WIDTH + 3 * CONV_WIDTH + 2 * GLA_KW + 2 * GLA_VW + GLA_RANK
N_EXPERTS = 8
LN_EPS = 1e-5
RMS_EPS = 1e-6

LANES = 128
IN_COLS_PAD = 6272
VMEM_LIMIT = 56 * 1024 * 1024

OFF_SB_Q, OFF_SB_K, OFF_SB_V = 0, SB_WIDTH, 2 * SB_WIDTH
OFF_CV_B = 3 * SB_WIDTH
OFF_CV_C = OFF_CV_B + CONV_WIDTH
OFF_CV_H = OFF_CV_C + CONV_WIDTH
OFF_G_Q = OFF_CV_H + CONV_WIDTH
OFF_G_K = OFF_G_Q + GLA_KW
OFF_G_V = OFF_G_K + GLA_KW
OFF_G_G = OFF_G_V + GLA_VW
OFF_G_R = OFF_G_G + GLA_VW

SB_EXIT = -90.0


def _params(sem):
    return pltpu.CompilerParams(dimension_semantics=sem, vmem_limit_bytes=VMEM_LIMIT)


def _split_bf16(x):
    hi = x.astype(BF16)
    lo = (x - hi.astype(F32)).astype(BF16)
    return hi, lo


def _dot(a, b):
    return jnp.dot(a, b, preferred_element_type=F32)


def _softplus(z):
    return jnp.maximum(z, 0.0) + jnp.log1p(jnp.exp(-jnp.abs(z)))


def _silu(x):
    return x * (1.0 / (1.0 + jnp.exp(-x)))


def _mm_kernel(a_ref, w_ref, o_ref):
    o_ref[...] = _dot(a_ref[...], w_ref[...]).astype(o_ref.dtype)


def matmul(a, w, *, tm, tn, out_dtype):
    m, k = a.shape
    n = w.shape[1]
    return pl.pallas_call(
        _mm_kernel,
        grid=(m // tm, n // tn),
        in_specs=[pl.BlockSpec((tm, k), lambda i, j: (i, 0)),
                  pl.BlockSpec((k, tn), lambda i, j: (0, j))],
        out_specs=pl.BlockSpec((tm, tn), lambda i, j: (i, j)),
        out_shape=jax.ShapeDtypeStruct((m, n), out_dtype),
        compiler_params=_params(("parallel", "parallel")),
        name="in_proj",
    )(a, w)


def _sb_kernel(q_ref, k_ref, v_ref, g_ref, o_ref, acc_ref, c_ref, *, blk, scale):
    qi = pl.program_id(2)
    q = q_ref[...]
    row = lax.broadcasted_iota(jnp.int32, (blk, blk), 0)
    col = lax.broadcasted_iota(jnp.int32, (blk, blk), 1)
    later = (row > col).astype(BF16)
    causal = col < row

    def block(j, masked):
        start = pl.multiple_of(j * blk, blk)
        k = k_ref[pl.ds(start, blk), :]
        v = v_ref[pl.ds(start, blk), :]
        z = lax.dot_general(q, k, (((1,), (1,)), ((), ())), preferred_element_type=F32) * scale
        sp = _softplus(z)
        log1m = -sp
        if masked:
            log1m = jnp.where(causal, log1m, 0.0)
        hi, lo = _split_bf16(log1m)
        tail_in = _dot(hi, later) + _dot(lo, later)
        c = c_ref[...]
        a = jnp.exp(z - sp + tail_in + c)
        if masked:
            a = jnp.where(causal, a, 0.0)
        acc_ref[...] += _dot(a.astype(BF16), v)
        c_new = c + tail_in[:, 0:1] + log1m[:, 0:1]
        c_ref[...] = c_new
        return jnp.max(c_new) > SB_EXIT

    acc_ref[...] = jnp.zeros_like(acc_ref)
    c_ref[...] = jnp.zeros_like(c_ref)
    go = block(qi, True)

    def cond(carry):
        j, more = carry
        return jnp.logical_and(j >= 0, more)

    def body(carry):
        j, _ = carry
        more = block(j, False)
        return j - 1, more

    lax.while_loop(cond, body, (qi - 1, go))

    o = acc_ref[...]
    ms = jnp.mean(o * o, axis=-1, keepdims=True)
    o_ref[...] = (o * lax.rsqrt(ms + RMS_EPS) * g_ref[...]).astype(o_ref.dtype)


def sb_attention(h, norm_g, *, batch, seq, blk):
    n = batch * seq
    nq = seq // blk
    qb, kb, vb = OFF_SB_Q // SB_DH, OFF_SB_K // SB_DH, OFF_SB_V // SB_DH
    kern = functools.partial(_sb_kernel, blk=blk, scale=SB_DH ** -0.5)
    return pl.pallas_call(
        kern,
        grid=(batch, SB_HEADS, nq),
        in_specs=[pl.BlockSpec((blk, SB_DH), lambda b, hh, i: (b * nq + i, qb + hh)),
                  pl.BlockSpec((seq, SB_DH), lambda b, hh, i: (b, kb + hh)),
                  pl.BlockSpec((seq, SB_DH), lambda b, hh, i: (b, vb + hh)),
                  pl.BlockSpec((1, SB_DH), lambda b, hh, i: (0, 0))],
        out_specs=pl.BlockSpec((blk, SB_DH), lambda b, hh, i: (b * nq + i, hh)),
        out_shape=jax.ShapeDtypeStruct((n, SB_WIDTH), BF16),
        scratch_shapes=[pltpu.VMEM((blk, SB_DH), F32), pltpu.VMEM((blk, 1), F32)],
        compiler_params=_params(("parallel", "parallel", "arbitrary")),
        name="sb_attention",
    )(h, h, h, norm_g.reshape(1, SB_DH))


def _conv_kernel(b_ref, c_ref, h_ref, w_ref, o_ref):
    u = c_ref[...].astype(F32) * h_ref[...].astype(F32)
    t = lax.broadcasted_iota(jnp.int32, u.shape, 0)
    u1 = jnp.where(t >= 1, pltpu.roll(u, 1, 0), 0.0)
    u2 = jnp.where(t >= 2, pltpu.roll(u, 2, 0), 0.0)
    w = w_ref[...]
    y = w[0:1, :] * u2 + w[1:2, :] * u1 + w[2:3, :] * u
    o_ref[...] = (b_ref[...].astype(F32) * y).astype(o_ref.dtype)


def gated_conv(h, conv_w, *, batch, seq):
    n = batch * seq
    nb = CONV_WIDTH // LANES
    bb, cb, hb = OFF_CV_B // LANES, OFF_CV_C // LANES, OFF_CV_H // LANES
    return pl.pallas_call(
        _conv_kernel,
        grid=(batch, nb),
        in_specs=[pl.BlockSpec((seq, LANES), lambda b, j: (b, bb + j)),
                  pl.BlockSpec((seq, LANES), lambda b, j: (b, cb + j)),
                  pl.BlockSpec((seq, LANES), lambda b, j: (b, hb + j)),
                  pl.BlockSpec((CONV_K, LANES), lambda b, j: (0, j))],
        out_specs=pl.BlockSpec((seq, LANES), lambda b, j: (b, j)),
        out_shape=jax.ShapeDtypeStruct((n, CONV_WIDTH), BF16),
        compiler_params=_params(("parallel", "parallel")),
        name="gated_conv",
    )(h, h, h, conv_w)


GLA_SEG = 128


def _gla_kernel(q_ref, k_ref, v_ref, g_ref, r_ref, wup_ref, b_ref, gn_ref, o_ref, s_ref, *, n_seg):
    seg_len, chunk = GLA_SEG, GLA_CHUNK
    n_chunk = seg_len // chunk
    two_dk, two_dv = 2 * GLA_DK, 2 * GLA_DV

    row = lax.broadcasted_iota(jnp.int32, (seg_len, seg_len), 0)
    col = lax.broadcasted_iota(jnp.int32, (seg_len, seg_len), 1)
    same = (row // chunk) == (col // chunk)
    cum_m = jnp.logical_and(same, col <= row).astype(BF16)
    tot_m = same.astype(BF16)
    r2 = lax.broadcasted_iota(jnp.int32, (two_dk, two_dv), 0)
    c2 = lax.broadcasted_iota(jnp.int32, (two_dk, two_dv), 1)
    head_bd = (r2 // GLA_DK) == (c2 // GLA_DV)
    ones_bd = head_bd.astype(BF16)
    bd_f = head_bd.astype(F32)
    t_idx = lax.broadcasted_iota(jnp.int32, (seg_len, 1), 0)
    t_in_chunk = t_idx % chunk
    chunk_id = t_idx // chunk

    wup = wup_ref[...]
    bias = b_ref[...]
    gn = gn_ref[...]
    s_ref[...] = jnp.zeros_like(s_ref)

    def rms(x):
        return x * lax.rsqrt(jnp.mean(x * x, axis=-1, keepdims=True) + RMS_EPS)

    def seg(si, carry):
        rows = pl.ds(pl.multiple_of(si * seg_len, seg_len), seg_len)
        q = q_ref[rows, :].astype(F32) * (GLA_DK ** -0.5)
        k = k_ref[rows, :].astype(F32)
        v = v_ref[rows, :].astype(F32)
        u = _dot(r_ref[rows, :], wup) + bias
        la = (jnp.minimum(u, 0.0) - jnp.log1p(jnp.exp(-jnp.abs(u)))) * (1.0 / GLA_TAU)
        hi, lo = _split_bf16(la)
        bcum = _dot(cum_m, hi) + _dot(cum_m, lo)
        blast = _dot(tot_m, hi) + _dot(tot_m, lo)

        o_acc = _dot((q * k).astype(BF16), ones_bd) * v
        for d in range(1, chunk):
            kd = pltpu.roll(k, d, 0)
            bd = pltpu.roll(bcum, d, 0)
            vd = pltpu.roll(v, d, 0)
            p = q * kd * jnp.exp(jnp.minimum(bcum - bd, 0.0))
            p = jnp.where(t_in_chunk >= d, p, 0.0)
            o_acc = o_acc + _dot(p.astype(BF16), ones_bd) * vd

        q_in = q * jnp.exp(bcum)
        k_up = k * jnp.exp(blast - bcum)
        k_up_t = k_up.T.astype(BF16)
        g_t = jnp.exp(blast.T)
        state = s_ref[...]
        for c in range(n_chunk):
            in_c = chunk_id == c
            q_c = jnp.where(in_c, q_in, 0.0).astype(BF16)
            o_acc = o_acc + _dot(q_c, state.astype(BF16))
            v_c = jnp.where(in_c, v, 0.0).astype(BF16)
            kv = _dot(k_up_t, v_c)
            g_c = g_t[:, c * chunk:c * chunk + 1]
            state = g_c * state + kv * bd_f
        s_ref[...] = state

        gate = g_ref[rows, :].astype(F32)
        o0 = rms(o_acc[:, :GLA_DV]) * gn
        o1 = rms(o_acc[:, GLA_DV:]) * gn
        out = jnp.concatenate([o0, o1], axis=1) * _silu(gate)
        o_ref[rows, :] = out.astype(o_ref.dtype)
        return carry

    lax.fori_loop(0, n_seg, seg, 0)


def gla(h, w_up_pad, gla_b, gla_norm_g, *, batch, seq):
    n = batch * seq
    pairs = GLA_HEADS // 2
    qb, kb = OFF_G_Q // LANES, OFF_G_K // LANES
    vb, gb = OFF_G_V // (2 * GLA_DV), OFF_G_G // (2 * GLA_DV)
    rb = OFF_G_R // LANES
    kern = functools.partial(_gla_kernel, n_seg=seq // GLA_SEG)
    return pl.pallas_call(
        kern,
        grid=(batch, pairs),
        in_specs=[pl.BlockSpec((seq, LANES), lambda b, p: (b, qb + p)),
                  pl.BlockSpec((seq, LANES), lambda b, p: (b, kb + p)),
                  pl.BlockSpec((seq, 2 * GLA_DV), lambda b, p: (b, vb + p)),
                  pl.BlockSpec((seq, 2 * GLA_DV), lambda b, p: (b, gb + p)),
                  pl.BlockSpec((seq, LANES), lambda b, p: (b, rb)),
                  pl.BlockSpec((LANES, LANES), lambda b, p: (0, p)),
                  pl.BlockSpec((1, LANES), lambda b, p: (0, p)),
                  pl.BlockSpec((1, GLA_DV), lambda b, p: (0, 0))],
        out_specs=pl.BlockSpec((seq, 2 * GLA_DV), lambda b, p: (b, p)),
        out_shape=jax.ShapeDtypeStruct((n, GLA_VW), BF16),
        scratch_shapes=[pltpu.VMEM((2 * GLA_DK, 2 * GLA_DV), F32)],
        compiler_params=_params(("parallel", "parallel")),
        name="gla",
    )(h, h, h, h, h, w_up_pad, gla_b.reshape(1, GLA_KW), gla_norm_g.reshape(1, GLA_DV))


def _layer_norm(y, g, b):
    mu = jnp.mean(y, axis=-1, keepdims=True)
    yc = y - mu
    var = jnp.mean(yc * yc, axis=-1, keepdims=True)
    return yc * lax.rsqrt(var + LN_EPS) * g + b


def _outproj_ln_kernel(sb_ref, cv_ref, gl_ref, w_ref, x_ref, g_ref, b_ref, xo_ref, xb_ref, *, alpha):
    mix = _dot(sb_ref[...], w_ref[0:SB_WIDTH, :])
    mix = mix + _dot(cv_ref[...], w_ref[SB_WIDTH:SB_WIDTH + CONV_WIDTH, :])
    mix = mix + _dot(gl_ref[...], w_ref[SB_WIDTH + CONV_WIDTH:, :])
    out = _layer_norm(alpha * x_ref[...] + mix, g_ref[...], b_ref[...])
    xo_ref[...] = out
    xb_ref[...] = out.astype(BF16)


def outproj_ln(o_sb, o_cv, o_gla, w_out, x, ln_g, ln_b, *, alpha, tm):
    n, d = x.shape
    kern = functools.partial(_outproj_ln_kernel, alpha=alpha)
    row = lambda i: (i, 0)
    fixed = lambda i: (0, 0)
    return pl.pallas_call(
        kern,
        grid=(n // tm,),
        in_specs=[pl.BlockSpec((tm, SB_WIDTH), row),
                  pl.BlockSpec((tm, CONV_WIDTH), row),
                  pl.BlockSpec((tm, GLA_VW), row),
                  pl.BlockSpec((d, d), fixed),
                  pl.BlockSpec((tm, d), row),
                  pl.BlockSpec((1, d), fixed),
                  pl.BlockSpec((1, d), fixed)],
        out_specs=[pl.BlockSpec((tm, d), row), pl.BlockSpec((tm, d), row)],
        out_shape=[jax.ShapeDtypeStruct((n, d), F32), jax.ShapeDtypeStruct((n, d), BF16)],
        compiler_params=_params(("parallel",)),
        name="outproj_ln",
    )(o_sb, o_cv, o_gla, w_out, x, ln_g.reshape(1, d), ln_b.reshape(1, d))


def _residual_ln_kernel(x_ref, f_ref, g_ref, b_ref, xo_ref, xb_ref, *, alpha):
    out = _layer_norm(alpha * x_ref[...] + f_ref[...].astype(F32), g_ref[...], b_ref[...])
    xo_ref[...] = out
    xb_ref[...] = out.astype(BF16)


def residual_ln(x, f, ln_g, ln_b, *, alpha, tm):
    n, d = x.shape
    kern = functools.partial(_residual_ln_kernel, alpha=alpha)
    row = lambda i: (i, 0)
    fixed = lambda i: (0, 0)
    return pl.pallas_call(
        kern,
        grid=(n // tm,),
        in_specs=[pl.BlockSpec((tm, d), row), pl.BlockSpec((tm, d), row),
                  pl.BlockSpec((1, d), fixed), pl.BlockSpec((1, d), fixed)],
        out_specs=[pl.BlockSpec((tm, d), row), pl.BlockSpec((tm, d), row)],
        out_shape=[jax.ShapeDtypeStruct((n, d), F32), jax.ShapeDtypeStruct((n, d), BF16)],
        compiler_params=_params(("parallel",)),
        name="residual_ln",
    )(x, f, ln_g.reshape(1, d), ln_b.reshape(1, d))


def _ffn_kernel(x_ref, wg_ref, wu_ref, wd_ref, gate_ref, o_ref, acc_ref, *, gated):
    e, f = pl.program_id(1), pl.program_id(2)

    @pl.when(jnp.logical_and(e == 0, f == 0))
    def _():
        acc_ref[...] = jnp.zeros_like(acc_ref)

    x = x_ref[...]
    hg = _dot(x, wg_ref[...])
    hu = _dot(x, wu_ref[...])
    hidden = _silu(hg) * hu
    if gated:
        hidden = hidden * gate_ref[...]
    acc_ref[...] += _dot(hidden.astype(BF16), wd_ref[...])

    @pl.when(jnp.logical_and(e == pl.num_programs(1) - 1, f == pl.num_programs(2) - 1))
    def _():
        o_ref[...] = acc_ref[...].astype(o_ref.dtype)


def ffn(xb, w_gate, w_up, w_down, gates, *, tm, tf):
    n, d = xb.shape
    n_e, _, d_ff = w_gate.shape
    gated = gates is not None
    if not gated:
        gates = jnp.ones((1, n, 1), F32)
    kern = functools.partial(_ffn_kernel, gated=gated)
    return pl.pallas_call(
        kern,
        grid=(n // tm, n_e, d_ff // tf),
        in_specs=[pl.BlockSpec((tm, d), lambda i, e, f: (i, 0)),
                  pl.BlockSpec((None, d, tf), lambda i, e, f: (e, 0, f)),
                  pl.BlockSpec((None, d, tf), lambda i, e, f: (e, 0, f)),
                  pl.BlockSpec((None, tf, d), lambda i, e, f: (e, f, 0)),
                  pl.BlockSpec((None, tm, 1), lambda i, e, f: (e, i, 0))],
        out_specs=pl.BlockSpec((tm, d), lambda i, e, f: (i, 0)),
        out_shape=jax.ShapeDtypeStruct((n, d), BF16),
        scratch_shapes=[pltpu.VMEM((tm, d), F32)],
        compiler_params=_params(("parallel", "arbitrary", "arbitrary")),
        name="ffn",
    )(xb, w_gate, w_up, w_down, gates)


def _router_kernel(x_ref, w_ref, gates_ref):
    logits = jnp.dot(x_ref[...], w_ref[...], preferred_element_type=F32, precision=lax.Precision.HIGHEST)
    col = lax.broadcasted_iota(jnp.int32, logits.shape, 1)
    neg = jnp.float32(-jnp.inf)
    l1 = jnp.where(col < N_EXPERTS, logits, neg)
    m1 = jnp.max(l1, axis=-1, keepdims=True)
    i1 = jnp.min(jnp.where(l1 == m1, col, LANES), axis=-1, keepdims=True)
    l2 = jnp.where(col == i1, neg, l1)
    m2 = jnp.max(l2, axis=-1, keepdims=True)
    i2 = jnp.min(jnp.where(l2 == m2, col, LANES), axis=-1, keepdims=True)
    e2 = jnp.exp(m2 - m1)
    w1 = 1.0 / (1.0 + e2)
    w2 = e2 / (1.0 + e2)
    gates_ref[...] = jnp.where(col == i1, w1, 0.0) + jnp.where(col == i2, w2, 0.0)


def router(x, w_router_pad, *, tm):
    n, d = x.shape
    return pl.pallas_call(
        _router_kernel,
        grid=(n // tm,),
        in_specs=[pl.BlockSpec((tm, d), lambda i: (i, 0)),
                  pl.BlockSpec((d, LANES), lambda i: (0, 0))],
        out_specs=pl.BlockSpec((tm, LANES), lambda i: (i, 0)),
        out_shape=jax.ShapeDtypeStruct((n, LANES), F32),
        compiler_params=_params(("parallel",)),
        name="router",
    )(x, w_router_pad)


def kernel(x, w_in, sb_norm_g, conv_w, gla_w_up, gla_b, gla_norm_g, w_out, ln1_g, ln1_b,
           ffn_w_gate, ffn_w_up, ffn_w_down, moe_w_router, moe_w_gate, moe_w_up, moe_w_down,
           ln2_g, ln2_b):
    batch, seq, d = x.shape
    depth = w_in.shape[0]
    n = batch * seq
    alpha = (2 * depth) ** 0.25

    xf = x.reshape(n, d)
    xb = xf.astype(BF16)
    for layer in range(depth):
        w_in_l = jnp.pad(w_in[layer], ((0, 0), (0, IN_COLS_PAD - IN_COLS))).astype(BF16)
        h = matmul(xb, w_in_l, tm=1024, tn=896, out_dtype=BF16)
        o_sb = sb_attention(h, sb_norm_g[layer], batch=batch, seq=seq, blk=256)
        o_cv = gated_conv(h, conv_w[layer], batch=batch, seq=seq)
        w_up_pad = jnp.pad(gla_w_up[layer], ((0, LANES - GLA_RANK), (0, 0))).astype(BF16)
        o_gla = gla(h, w_up_pad, gla_b[layer], gla_norm_g[layer], batch=batch, seq=seq)
        xf, xb = outproj_ln(o_sb, o_cv, o_gla, w_out[layer].astype(BF16), xf, ln1_g[layer], ln1_b[layer],
                            alpha=alpha, tm=256)
        j = layer // 2
        if layer % 2 == 0:
            f = ffn(xb, ffn_w_gate[j][None].astype(BF16), ffn_w_up[j][None].astype(BF16),
                    ffn_w_down[j][None].astype(BF16), None, tm=1024, tf=512)
        else:
            w_r = jnp.pad(moe_w_router[j], ((0, 0), (0, LANES - N_EXPERTS)))
            gates = router(xf, w_r, tm=512)
            gates_e = gates[:, :N_EXPERTS].T[:, :, None]
            f = ffn(xb, moe_w_gate[j].astype(BF16), moe_w_up[j].astype(BF16), moe_w_down[j].astype(BF16),
                    gates_e, tm=1024, tf=512)
        xf, xb = residual_ln(xf, f, ln2_g[layer], ln2_b[layer], alpha=alpha, tm=512)
    return xf.reshape(batch, seq, d)
```

```python
import functools

import jax
import jax.numpy as jnp
from jax import lax
from jax.experimental import pallas as pl
from jax.experimental.pallas import tpu as pltpu

F32 = jnp.float32
BF16 = jnp.bfloat16

D_MODEL = 2048
SB_HEADS = 8
SB_DH = 128
SB_WIDTH = SB_HEADS * SB_DH
CONV_WIDTH = 512
CONV_K = 3
GLA_HEADS = 4
GLA_DK = 64
GLA_DV = 128
GLA_KW = GLA_HEADS * GLA_DK
GLA_VW = GLA_HEADS * GLA_DV
GLA_RANK = 16
GLA_TAU = 16.0
GLA_CHUNK = 16
IN_COLS = 3 * SB_WIDTH + 3 * CONV_WIDTH + 2 * GLA_KW + 2 * GLA_VW + GLA_RANK
N_EXPERTS = 8
LN_EPS = 1e-5
RMS_EPS = 1e-6

LANES = 128
IN_COLS_PAD = 6272
VMEM_LIMIT = 56 * 1024 * 1024

OFF_SB_Q, OFF_SB_K, OFF_SB_V = 0, SB_WIDTH, 2 * SB_WIDTH
OFF_CV_B = 3 * SB_WIDTH
OFF_CV_C = OFF_CV_B + CONV_WIDTH
OFF_CV_H = OFF_CV_C + CONV_WIDTH
OFF_G_Q = OFF_CV_H + CONV_WIDTH
OFF_G_K = OFF_G_Q + GLA_KW
OFF_G_V = OFF_G_K + GLA_KW
OFF_G_G = OFF_G_V + GLA_VW
OFF_G_R = OFF_G_G + GLA_VW

SB_EXIT = -90.0


def _params(sem):
    return pltpu.CompilerParams(dimension_semantics=sem, vmem_limit_bytes=VMEM_LIMIT)


def _split_bf16(x):
    hi = x.astype(BF16)
    lo = (x - hi.astype(F32)).astype(BF16)
    return hi, lo


def _dot(a, b):
    return jnp.dot(a, b, preferred_element_type=F32)


def _softplus(z):
    return jnp.maximum(z, 0.0) + jnp.log1p(jnp.exp(-jnp.abs(z)))


def _silu(x):
    return x * (1.0 / (1.0 + jnp.exp(-x)))


def _mm_kernel(a_ref, w_ref, o_ref):
    o_ref[...] = _dot(a_ref[...], w_ref[...]).astype(o_ref.dtype)


def matmul(a, w, *, tm, tn, out_dtype):
    m, k = a.shape
    n = w.shape[1]
    return pl.pallas_call(
        _mm_kernel,
        grid=(m // tm, n // tn),
        in_specs=[pl.BlockSpec((tm, k), lambda i, j: (i, 0)),
                  pl.BlockSpec((k, tn), lambda i, j: (0, j))],
        out_specs=pl.BlockSpec((tm, tn), lambda i, j: (i, j)),
        out_shape=jax.ShapeDtypeStruct((m, n), out_dtype),
        compiler_params=_params(("parallel", "parallel")),
        name="in_proj",
    )(a, w)


def _sb_kernel(q_ref, k_ref, v_ref, g_ref, o_ref, acc_ref, c_ref, *, blk, scale):
    qi = pl.program_id(2)
    q = q_ref[...]
    row = lax.broadcasted_iota(jnp.int32, (blk, blk), 0)
    col = lax.broadcasted_iota(jnp.int32, (blk, blk), 1)
    later = (row > col).astype(BF16)
    causal = col < row

    def block(j, masked):
        start = pl.multiple_of(j * blk, blk)
        k = k_ref[pl.ds(start, blk), :]
        v = v_ref[pl.ds(start, blk), :]
        z = lax.dot_general(q, k, (((1,), (1,)), ((), ())), preferred_element_type=F32) * scale
        sp = _softplus(z)
        log1m = -sp
        if masked:
            log1m = jnp.where(causal, log1m, 0.0)
        hi, lo = _split_bf16(log1m)
        tail_in = _dot(hi, later) + _dot(lo, later)
        c = c_ref[...]
        a = jnp.exp(z - sp + tail_in + c)
        if masked:
            a = jnp.where(causal, a, 0.0)
        acc_ref[...] += _dot(a.astype(BF16), v)
        c_new = c + tail_in[:, 0:1] + log1m[:, 0:1]
        c_ref[...] = c_new
        return jnp.max(c_new) > SB_EXIT

    acc_ref[...] = jnp.zeros_like(acc_ref)
    c_ref[...] = jnp.zeros_like(c_ref)
    go = block(qi, True)

    def cond(carry):
        j, more = carry
        return jnp.logical_and(j >= 0, more)

    def body(carry):
        j, _ = carry
        more = block(j, False)
        return j - 1, more

    lax.while_loop(cond, body, (qi - 1, go))

    o = acc_ref[...]
    ms = jnp.mean(o * o, axis=-1, keepdims=True)
    o_ref[...] = (o * lax.rsqrt(ms + RMS_EPS) * g_ref[...]).astype(o_ref.dtype)


def sb_attention(h, norm_g, *, batch, seq, blk):
    n = batch * seq
    nq = seq // blk
    qb, kb, vb = OFF_SB_Q // SB_DH, OFF_SB_K // SB_DH, OFF_SB_V // SB_DH
    kern = functools.partial(_sb_kernel, blk=blk, scale=SB_DH ** -0.5)
    return pl.pallas_call(
        kern,
        grid=(batch, SB_HEADS, nq),
        in_specs=[pl.BlockSpec((blk, SB_DH), lambda b, hh, i: (b * nq + i, qb + hh)),
                  pl.BlockSpec((seq, SB_DH), lambda b, hh, i: (b, kb + hh)),
                  pl.BlockSpec((seq, SB_DH), lambda b, hh, i: (b, vb + hh)),
                  pl.BlockSpec((1, SB_DH), lambda b, hh, i: (0, 0))],
        out_specs=pl.BlockSpec((blk, SB_DH), lambda b, hh, i: (b * nq + i, hh)),
        out_shape=jax.ShapeDtypeStruct((n, SB_WIDTH), BF16),
        scratch_shapes=[pltpu.VMEM((blk, SB_DH), F32), pltpu.VMEM((blk, 1), F32)],
        compiler_params=_params(("parallel", "parallel", "arbitrary")),
        name="sb_attention",
    )(h, h, h, norm_g.reshape(1, SB_DH))


def _conv_kernel(b_ref, c_ref, h_ref, w_ref, o_ref):
    u = c_ref[...].astype(F32) * h_ref[...].astype(F32)
    t = lax.broadcasted_iota(jnp.int32, u.shape, 0)
    u1 = jnp.where(t >= 1, pltpu.roll(u, 1, 0), 0.0)
    u2 = jnp.where(t >= 2, pltpu.roll(u, 2, 0), 0.0)
    w = w_ref[...]
    y = w[0:1, :] * u2 + w[1:2, :] * u1 + w[2:3, :] * u
    o_ref[...] = (b_ref[...].astype(F32) * y).astype(o_ref.dtype)


def gated_conv(h, conv_w, *, batch, seq):
    n = batch * seq
    nb = CONV_WIDTH // LANES
    bb, cb, hb = OFF_CV_B // LANES, OFF_CV_C // LANES, OFF_CV_H // LANES
    return pl.pallas_call(
        _conv_kernel,
        grid=(batch, nb),
        in_specs=[pl.BlockSpec((seq, LANES), lambda b, j: (b, bb + j)),
                  pl.BlockSpec((seq, LANES), lambda b, j: (b, cb + j)),
                  pl.BlockSpec((seq, LANES), lambda b, j: (b, hb + j)),
                  pl.BlockSpec((CONV_K, LANES), lambda b, j: (0, j))],
        out_specs=pl.BlockSpec((seq, LANES), lambda b, j: (b, j)),
        out_shape=jax.ShapeDtypeStruct((n, CONV_WIDTH), BF16),
        compiler_params=_params(("parallel", "parallel")),
        name="gated_conv",
    )(h, h, h, conv_w)


GLA_SEG = 128


def _gla_kernel(q_ref, k_ref, v_ref, g_ref, r_ref, wup_ref, b_ref, gn_ref, o_ref, s_ref, *, n_seg):
    seg_len, chunk = GLA_SEG, GLA_CHUNK
    n_chunk = seg_len // chunk
    two_dk, two_dv = 2 * GLA_DK, 2 * GLA_DV

    row = lax.broadcasted_iota(jnp.int32, (seg_len, seg_len), 0)
    col = lax.broadcasted_iota(jnp.int32, (seg_len, seg_len), 1)
    same = (row // chunk) == (col // chunk)
    cum_m = jnp.logical_and(same, col <= row).astype(BF16)
    tot_m = same.astype(BF16)
    r2 = lax.broadcasted_iota(jnp.int32, (two_dk, two_dv), 0)
    c2 = lax.broadcasted_iota(jnp.int32, (two_dk, two_dv), 1)
    head_bd = (r2 // GLA_DK) == (c2 // GLA_DV)
    ones_bd = head_bd.astype(BF16)
    bd_f = head_bd.astype(F32)
    t_idx = lax.broadcasted_iota(jnp.int32, (seg_len, 1), 0)
    t_in_chunk = t_idx % chunk
    chunk_id = t_idx // chunk

    wup = wup_ref[...]
    bias = b_ref[...]
    gn = gn_ref[...]
    s_ref[...] = jnp.zeros_like(s_ref)

    def rms(x):
        return x * lax.rsqrt(jnp.mean(x * x, axis=-1, keepdims=True) + RMS_EPS)

    def seg(si, carry):
        rows = pl.ds(pl.multiple_of(si * seg_len, seg_len), seg_len)
        q = q_ref[rows, :].astype(F32) * (GLA_DK ** -0.5)
        k = k_ref[rows, :].astype(F32)
        v = v_ref[rows, :].astype(F32)
        u = _dot(r_ref[rows, :], wup) + bias
        la = (jnp.minimum(u, 0.0) - jnp.log1p(jnp.exp(-jnp.abs(u)))) * (1.0 / GLA_TAU)
        hi, lo = _split_bf16(la)
        bcum = _dot(cum_m, hi) + _dot(cum_m, lo)
        blast = _dot(tot_m, hi) + _dot(tot_m, lo)

        o_acc = _dot((q * k).astype(BF16), ones_bd) * v
        for d in range(1, chunk):
            kd = pltpu.roll(k, d, 0)
            bd = pltpu.roll(bcum, d, 0)
            vd = pltpu.roll(v, d, 0)
            p = q * kd * jnp.exp(jnp.minimum(bcum - bd, 0.0))
            p = jnp.where(t_in_chunk >= d, p, 0.0)
            o_acc = o_acc + _dot(p.astype(BF16), ones_bd) * vd

        q_in = q * jnp.exp(bcum)
        k_up = k * jnp.exp(blast - bcum)
        k_up_t = k_up.T.astype(BF16)
        g_t = jnp.exp(blast.T)
        state = s_ref[...]
        for c in range(n_chunk):
            in_c = chunk_id == c
            q_c = jnp.where(in_c, q_in, 0.0).astype(BF16)
            o_acc = o_acc + _dot(q_c, state.astype(BF16))
            v_c = jnp.where(in_c, v, 0.0).astype(BF16)
            kv = _dot(k_up_t, v_c)
            g_c = g_t[:, c * chunk:c * chunk + 1]
            state = g_c * state + kv * bd_f
        s_ref[...] = state

        gate = g_ref[rows, :].astype(F32)
        o0 = rms(o_acc[:, :GLA_DV]) * gn
        o1 = rms(o_acc[:, GLA_DV:]) * gn
        out = jnp.concatenate([o0, o1], axis=1) * _silu(gate)
        o_ref[rows, :] = out.astype(o_ref.dtype)
        return carry

    lax.fori_loop(0, n_seg, seg, 0)


def gla(h, w_up_pad, gla_b, gla_norm_g, *, batch, seq):
    n = batch * seq
    pairs = GLA_HEADS // 2
    qb, kb = OFF_G_Q // LANES, OFF_G_K // LANES
    vb, gb = OFF_G_V // (2 * GLA_DV), OFF_G_G // (2 * GLA_DV)
    rb = OFF_G_R // LANES
    kern = functools.partial(_gla_kernel, n_seg=seq // GLA_SEG)
    return pl.pallas_call(
        kern,
        grid=(batch, pairs),
        in_specs=[pl.BlockSpec((seq, LANES), lambda b, p: (b, qb + p)),
                  pl.BlockSpec((seq, LANES), lambda b, p: (b, kb + p)),
                  pl.BlockSpec((seq, 2 * GLA_DV), lambda b, p: (b, vb + p)),
                  pl.BlockSpec((seq, 2 * GLA_DV), lambda b, p: (b, gb + p)),
                  pl.BlockSpec((seq, LANES), lambda b, p: (b, rb)),
                  pl.BlockSpec((LANES, LANES), lambda b, p: (0, p)),
                  pl.BlockSpec((1, LANES), lambda b, p: (0, p)),
                  pl.BlockSpec((1, GLA_DV), lambda b, p: (0, 0))],
        out_specs=pl.BlockSpec((seq, 2 * GLA_DV), lambda b, p: (b, p)),
        out_shape=jax.ShapeDtypeStruct((n, GLA_VW), BF16),
        scratch_shapes=[pltpu.VMEM((2 * GLA_DK, 2 * GLA_DV), F32)],
        compiler_params=_params(("parallel", "parallel")),
        name="gla",
    )(h, h, h, h, h, w_up_pad, gla_b.reshape(1, GLA_KW), gla_norm_g.reshape(1, GLA_DV))


def _layer_norm(y, g, b):
    mu = jnp.mean(y, axis=-1, keepdims=True)
    yc = y - mu
    var = jnp.mean(yc * yc, axis=-1, keepdims=True)
    return yc * lax.rsqrt(var + LN_EPS) * g + b


def _outproj_ln_kernel(sb_ref, cv_ref, gl_ref, w_ref, x_ref, g_ref, b_ref, xo_ref, xb_ref, *, alpha):
    mix = _dot(sb_ref[...], w_ref[0:SB_WIDTH, :])
    mix = mix + _dot(cv_ref[...], w_ref[SB_WIDTH:SB_WIDTH + CONV_WIDTH, :])
    mix = mix + _dot(gl_ref[...], w_ref[SB_WIDTH + CONV_WIDTH:, :])
    out = _layer_norm(alpha * x_ref[...] + mix, g_ref[...], b_ref[...])
    xo_ref[...] = out
    xb_ref[...] = out.astype(BF16)


def outproj_ln(o_sb, o_cv, o_gla, w_out, x, ln_g, ln_b, *, alpha, tm):
    n, d = x.shape
    kern = functools.partial(_outproj_ln_kernel, alpha=alpha)
    row = lambda i: (i, 0)
    fixed = lambda i: (0, 0)
    return pl.pallas_call(
        kern,
        grid=(n // tm,),
        in_specs=[pl.BlockSpec((tm, SB_WIDTH), row),
                  pl.BlockSpec((tm, CONV_WIDTH), row),
                  pl.BlockSpec((tm, GLA_VW), row),
                  pl.BlockSpec((d, d), fixed),
                  pl.BlockSpec((tm, d), row),
                  pl.BlockSpec((1, d), fixed),
                  pl.BlockSpec((1, d), fixed)],
        out_specs=[pl.BlockSpec((tm, d), row), pl.BlockSpec((tm, d), row)],
        out_shape=[jax.ShapeDtypeStruct((n, d), F32), jax.ShapeDtypeStruct((n, d), BF16)],
        compiler_params=_params(("parallel",)),
        name="outproj_ln",
    )(o_sb, o_cv, o_gla, w_out, x, ln_g.reshape(1, d), ln_b.reshape(1, d))


def _residual_ln_kernel(x_ref, f_ref, g_ref, b_ref, xo_ref, xb_ref, *, alpha):
    out = _layer_norm(alpha * x_ref[...] + f_ref[...].astype(F32), g_ref[...], b_ref[...])
    xo_ref[...] = out
    xb_ref[...] = out.astype(BF16)


def residual_ln(x, f, ln_g, ln_b, *, alpha, tm):
    n, d = x.shape
    kern = functools.partial(_residual_ln_kernel, alpha=alpha)
    row = lambda i: (i, 0)
    fixed = lambda i: (0, 0)
    return pl.pallas_call(
        kern,
        grid=(n // tm,),
        in_specs=[pl.BlockSpec((tm, d), row), pl.BlockSpec((tm, d), row),
                  pl.BlockSpec((1, d), fixed), pl.BlockSpec((1, d), fixed)],
        out_specs=[pl.BlockSpec((tm, d), row), pl.BlockSpec((tm, d), row)],
        out_shape=[jax.ShapeDtypeStruct((n, d), F32), jax.ShapeDtypeStruct((n, d), BF16)],
        compiler_params=_params(("parallel",)),
        name="residual_ln",
    )(x, f, ln_g.reshape(1, d), ln_b.reshape(1, d))


def _ffn_kernel(te_ref, valid_ref, x_ref, wg_ref, wu_ref, wd_ref, o_ref):
    i, f = pl.program_id(0), pl.program_id(1)

    @pl.when(f == 0)
    def _():
        o_ref[...] = jnp.zeros_like(o_ref)

    @pl.when(valid_ref[i] == 1)
    def _():
        x = x_ref[...]
        hg = _dot(x, wg_ref[...].astype(BF16))
        hu = _dot(x, wu_ref[...].astype(BF16))
        hidden = (_silu(hg) * hu).astype(BF16)
        o_ref[...] += _dot(hidden, wd_ref[...].astype(BF16))


def ffn(xb, w_gate, w_up, w_down, tile_expert, tile_valid, *, tm, tf):
    r, d = xb.shape
    d_ff = w_gate.shape[2]
    n_f = d_ff // tf

    def w_col(i, f, te, va):
        return te[i], 0, jnp.where(va[i] == 1, f, n_f - 1)

    def w_row(i, f, te, va):
        return te[i], jnp.where(va[i] == 1, f, n_f - 1), 0

    grid_spec = pltpu.PrefetchScalarGridSpec(
        num_scalar_prefetch=2,
        grid=(r // tm, n_f),
        in_specs=[pl.BlockSpec((tm, d), lambda i, f, te, va: (i, 0)),
                  pl.BlockSpec((None, d, tf), w_col),
                  pl.BlockSpec((None, d, tf), w_col),
                  pl.BlockSpec((None, tf, d), w_row)],
        out_specs=pl.BlockSpec((tm, d), lambda i, f, te, va: (i, 0)),
    )
    return pl.pallas_call(
        _ffn_kernel,
        grid_spec=grid_spec,
        out_shape=jax.ShapeDtypeStruct((r, d), F32),
        compiler_params=_params(("parallel", "arbitrary")),
        name="ffn",
    )(tile_expert, tile_valid, xb, w_gate, w_up, w_down)


def _router_kernel(x_ref, w_ref, o_ref):
    logits = jnp.dot(x_ref[...], w_ref[...], preferred_element_type=F32, precision=lax.Precision.HIGHEST)
    col = lax.broadcasted_iota(jnp.int32, logits.shape, 1)
    neg = jnp.float32(-jnp.inf)
    l1 = jnp.where(col < N_EXPERTS, logits, neg)
    m1 = jnp.max(l1, axis=-1, keepdims=True)
    i1 = jnp.min(jnp.where(l1 == m1, col, LANES), axis=-1, keepdims=True)
    l2 = jnp.where(col == i1, neg, l1)
    m2 = jnp.max(l2, axis=-1, keepdims=True)
    i2 = jnp.min(jnp.where(l2 == m2, col, LANES), axis=-1, keepdims=True)
    e2 = jnp.exp(m2 - m1)
    w1 = 1.0 / (1.0 + e2)
    w2 = e2 / (1.0 + e2)
    out = jnp.where(col == 0, i1.astype(F32), 0.0) + jnp.where(col == 1, i2.astype(F32), 0.0)
    out = out + jnp.where(col == 2, w1, 0.0) + jnp.where(col == 3, w2, 0.0)
    o_ref[...] = out


def router(x, w_router_pad, *, tm):
    n, d = x.shape
    return pl.pallas_call(
        _router_kernel,
        grid=(n // tm,),
        in_specs=[pl.BlockSpec((tm, d), lambda i: (i, 0)),
                  pl.BlockSpec((d, LANES), lambda i: (0, 0))],
        out_specs=pl.BlockSpec((tm, LANES), lambda i: (i, 0)),
        out_shape=jax.ShapeDtypeStruct((n, LANES), F32),
        compiler_params=_params(("parallel",)),
        name="router",
    )(x, w_router_pad)


def _gather_kernel(idx_ref, x_hbm, o_ref, buf, sem, *, rows):
    def row_copy(r):
        tok = idx_ref[0, 0, r]
        return pltpu.make_async_copy(x_hbm.at[pl.ds(tok, 1), :], buf.at[pl.ds(r, 1), :], sem)

    def issue(r, c):
        row_copy(r).start()
        return c

    def wait(r, c):
        row_copy(r).wait()
        return c

    lax.fori_loop(0, rows, issue, 0)
    lax.fori_loop(0, rows, wait, 0)
    o_ref[...] = buf[...].astype(o_ref.dtype)


def gather_rows(x, src, *, rows):
    d = x.shape[1]
    r = src.shape[0]
    kern = functools.partial(_gather_kernel, rows=rows)
    return pl.pallas_call(
        kern,
        grid=(r // rows,),
        in_specs=[pl.BlockSpec((1, 1, rows), lambda i: (i, 0, 0), memory_space=pltpu.SMEM),
                  pl.BlockSpec(memory_space=pl.ANY)],
        out_specs=pl.BlockSpec((rows, d), lambda i: (i, 0)),
        out_shape=jax.ShapeDtypeStruct((r, d), BF16),
        scratch_shapes=[pltpu.VMEM((rows, d), F32), pltpu.SemaphoreType.DMA(())],
        compiler_params=_params(("arbitrary",)),
        name="gather_rows",
    )(src.reshape(r // rows, 1, rows), x)


def _combine_ln_kernel(pos_ref, w_ref, x_ref, y_hbm, g_ref, b_ref, xo_ref, xb_ref,
                       buf_a, buf_b, sem_a, sem_b, *, rows, alpha):
    def row_copy(r, off, buf, sem):
        p = pos_ref[0, 0, off + r]
        return pltpu.make_async_copy(y_hbm.at[pl.ds(p, 1), :], buf.at[pl.ds(r, 1), :], sem)

    def issue(r, c):
        row_copy(r, 0, buf_a, sem_a).start()
        row_copy(r, rows, buf_b, sem_b).start()
        return c

    def wait(r, c):
        row_copy(r, 0, buf_a, sem_a).wait()
        row_copy(r, rows, buf_b, sem_b).wait()
        return c

    lax.fori_loop(0, rows, issue, 0)
    lax.fori_loop(0, rows, wait, 0)
    w = w_ref[...]
    y = w[:, 2:3] * buf_a[...] + w[:, 3:4] * buf_b[...]
    out = _layer_norm(alpha * x_ref[...] + y, g_ref[...], b_ref[...])
    xo_ref[...] = out
    xb_ref[...] = out.astype(BF16)


def combine_ln(x, y, pos, route, ln_g, ln_b, *, alpha, rows):
    n, d = x.shape
    kern = functools.partial(_combine_ln_kernel, rows=rows, alpha=alpha)
    row = lambda i: (i, 0)
    fixed = lambda i: (0, 0)
    return pl.pallas_call(
        kern,
        grid=(n // rows,),
        in_specs=[pl.BlockSpec((1, 1, 2 * rows), lambda i: (i, 0, 0), memory_space=pltpu.SMEM),
                  pl.BlockSpec((rows, LANES), row),
                  pl.BlockSpec((rows, d), row),
                  pl.BlockSpec(memory_space=pl.ANY),
                  pl.BlockSpec((1, d), fixed),
                  pl.BlockSpec((1, d), fixed)],
        out_specs=[pl.BlockSpec((rows, d), row), pl.BlockSpec((rows, d), row)],
        out_shape=[jax.ShapeDtypeStruct((n, d), F32), jax.ShapeDtypeStruct((n, d), BF16)],
        scratch_shapes=[pltpu.VMEM((rows, d), F32), pltpu.VMEM((rows, d), F32),
                        pltpu.SemaphoreType.DMA(()), pltpu.SemaphoreType.DMA(())],
        compiler_params=_params(("arbitrary",)),
        name="combine_ln",
    )(pos, route, x, y, ln_g.reshape(1, d), ln_b.reshape(1, d))


def moe_block(xf, w_router, w_gate, w_up, w_down, ln_g, ln_b, *, alpha, tm, tf, rows):
    n, d = xf.shape
    n_e = w_gate.shape[0]
    w_r = jnp.pad(w_router, ((0, 0), (0, LANES - n_e)))
    route = router(xf, w_r, tm=rows)
    ef = route[:, 0:2].astype(jnp.int32).reshape(-1)
    onehot = (ef[:, None] == jnp.arange(n_e, dtype=jnp.int32)[None, :]).astype(jnp.int32)
    csum = jnp.cumsum(onehot, axis=0)
    rank = jnp.take_along_axis(csum, ef[:, None], axis=1)[:, 0] - 1
    counts = csum[-1]
    padded = ((counts + tm - 1) // tm) * tm
    ends = jnp.cumsum(padded)
    pos = (ends - padded)[ef] + rank
    r_rows = 2 * n + n_e * tm
    src = jnp.zeros((r_rows,), jnp.int32).at[pos].set(jnp.arange(2 * n, dtype=jnp.int32) // 2)
    tile_start = jnp.arange(r_rows // tm, dtype=jnp.int32) * tm
    tile_valid = (tile_start < ends[-1]).astype(jnp.int32)
    te_raw = jnp.minimum(jnp.searchsorted(ends, tile_start, side="right").astype(jnp.int32), n_e - 1)
    last_e = te_raw[jnp.maximum(ends[-1] // tm - 1, 0)]
    tile_expert = jnp.where(tile_valid == 1, te_raw, last_e)

    xs = gather_rows(xf, src, rows=rows)
    ys = ffn(xs, w_gate, w_up, w_down, tile_expert, tile_valid, tm=tm, tf=tf)
    pos2 = pos.reshape(n // rows, rows, 2).transpose(0, 2, 1).reshape(n // rows, 1, 2 * rows)
    return combine_ln(xf, ys, pos2, route, ln_g, ln_b, alpha=alpha, rows=rows)


def kernel(x, w_in, sb_norm_g, conv_w, gla_w_up, gla_b, gla_norm_g, w_out, ln1_g, ln1_b,
           ffn_w_gate, ffn_w_up, ffn_w_down, moe_w_router, moe_w_gate, moe_w_up, moe_w_down,
           ln2_g, ln2_b):
    batch, seq, d = x.shape
    depth = w_in.shape[0]
    n = batch * seq
    alpha = (2 * depth) ** 0.25

    xf = x.reshape(n, d)
    xb = xf.astype(BF16)
    for layer in range(depth):
        w_in_l = jnp.pad(w_in[layer], ((0, 0), (0, IN_COLS_PAD - IN_COLS))).astype(BF16)
        h = matmul(xb, w_in_l, tm=1024, tn=896, out_dtype=BF16)
        o_sb = sb_attention(h, sb_norm_g[layer], batch=batch, seq=seq, blk=256)
        o_cv = gated_conv(h, conv_w[layer], batch=batch, seq=seq)
        w_up_pad = jnp.pad(gla_w_up[layer], ((0, LANES - GLA_RANK), (0, 0))).astype(BF16)
        o_gla = gla(h, w_up_pad, gla_b[layer], gla_norm_g[layer], batch=batch, seq=seq)
        xf, xb = outproj_ln(o_sb, o_cv, o_gla, w_out[layer].astype(BF16), xf, ln1_g[layer], ln1_b[layer],
                            alpha=alpha, tm=256)
        j = layer // 2
        if layer % 2 == 0:
            ones = jnp.ones((n // 1024,), jnp.int32)
            f = ffn(xb, ffn_w_gate[j][None], ffn_w_up[j][None], ffn_w_down[j][None], ones * 0, ones, tm=1024, tf=256)
            xf, xb = residual_ln(xf, f, ln2_g[layer], ln2_b[layer], alpha=alpha, tm=512)
        else:
            xf, xb = moe_block(xf, moe_w_router[j], moe_w_gate[j], moe_w_up[j], moe_w_down[j],
                               ln2_g[layer], ln2_b[layer], alpha=alpha, tm=1024, tf=256, rows=256)
    return xf.reshape(batch, seq, d)
```

```python
import functools

import jax
import jax.numpy as jnp
from jax import lax
from jax.experimental import pallas as pl
from jax.experimental.pallas import tpu as pltpu

F32 = jnp.float32
BF16 = jnp.bfloat16

D_MODEL = 2048
SB_HEADS = 8
SB_DH = 128
SB_WIDTH = SB_HEADS * SB_DH
CONV_WIDTH = 512
CONV_K = 3
GLA_HEADS = 4
GLA_DK = 64
GLA_DV = 128
GLA_KW = GLA_HEADS * GLA_DK
GLA_VW = GLA_HEADS * GLA_DV
GLA_RANK = 16
GLA_TAU = 16.0
GLA_CHUNK = 16
IN_COLS = 3 * SB_WIDTH + 3 * CONV_WIDTH + 2 * GLA_KW + 2 * GLA_VW + GLA_RANK
N_EXPERTS = 8
LN_EPS = 1e-5
RMS_EPS = 1e-6

LANES = 128
IN_COLS_PAD = 6272
VMEM_LIMIT = 56 * 1024 * 1024

OFF_SB_Q, OFF_SB_K, OFF_SB_V = 0, SB_WIDTH, 2 * SB_WIDTH
OFF_CV_B = 3 * SB_WIDTH
OFF_CV_C = OFF_CV_B + CONV_WIDTH
OFF_CV_H = OFF_CV_C + CONV_WIDTH
OFF_G_Q = OFF_CV_H + CONV_WIDTH
OFF_G_K = OFF_G_Q + GLA_KW
OFF_G_V = OFF_G_K + GLA_KW
OFF_G_G = OFF_G_V + GLA_VW
OFF_G_R = OFF_G_G + GLA_VW

SB_EXIT = -90.0


def _params(sem):
    return pltpu.CompilerParams(dimension_semantics=sem, vmem_limit_bytes=VMEM_LIMIT)


def _split_bf16(x):
    hi = x.astype(BF16)
    lo = (x - hi.astype(F32)).astype(BF16)
    return hi, lo


def _dot(a, b):
    return jnp.dot(a, b, preferred_element_type=F32)


def _softplus(z):
    return jnp.maximum(z, 0.0) + jnp.log1p(jnp.exp(-jnp.abs(z)))


def _silu(x):
    return x * (1.0 / (1.0 + jnp.exp(-x)))


def _in_proj_kernel(x_ref, w_ref, wr_ref, h_ref, r_ref):
    x = x_ref[...]
    h_ref[...] = _dot(x, w_ref[...].astype(BF16)).astype(h_ref.dtype)

    @pl.when(pl.program_id(1) == 0)
    def _():
        r_ref[...] = _dot(x, wr_ref[...]).astype(r_ref.dtype)


def in_proj(xb, w_in, w_r_pad, layer, *, tm, tn):
    n, d = xb.shape
    return pl.pallas_call(
        _in_proj_kernel,
        grid=(n // tm, OFF_G_R // tn),
        in_specs=[pl.BlockSpec((tm, d), lambda i, j: (i, 0)),
                  pl.BlockSpec((None, d, tn), lambda i, j: (layer, 0, j)),
                  pl.BlockSpec((d, LANES), lambda i, j: (0, 0))],
        out_specs=[pl.BlockSpec((tm, tn), lambda i, j: (i, j)),
                   pl.BlockSpec((tm, LANES), lambda i, j: (i, 0))],
        out_shape=[jax.ShapeDtypeStruct((n, OFF_G_R), BF16), jax.ShapeDtypeStruct((n, LANES), BF16)],
        compiler_params=_params(("parallel", "arbitrary")),
        name="in_proj",
    )(xb, w_in, w_r_pad)


def _sb_kernel(q_ref, k_ref, v_ref, g_ref, o_ref, acc_ref, c_ref, *, blk, scale):
    qi = pl.program_id(2)
    q = q_ref[...]
    row = lax.broadcasted_iota(jnp.int32, (blk, blk), 0)
    col = lax.broadcasted_iota(jnp.int32, (blk, blk), 1)
    later = (row > col).astype(BF16)
    causal = col < row

    def block(j, masked):
        start = pl.multiple_of(j * blk, blk)
        k = k_ref[pl.ds(start, blk), :]
        v = v_ref[pl.ds(start, blk), :]
        z = lax.dot_general(q, k, (((1,), (1,)), ((), ())), preferred_element_type=F32) * scale
        sp = _softplus(z)
        log1m = -sp
        if masked:
            log1m = jnp.where(causal, log1m, 0.0)
        hi, lo = _split_bf16(log1m)
        tail_in = _dot(hi, later) + _dot(lo, later)
        c = c_ref[...]
        a = jnp.exp(z - sp + tail_in + c)
        if masked:
            a = jnp.where(causal, a, 0.0)
        acc_ref[...] += _dot(a.astype(BF16), v)
        c_new = c + tail_in[:, 0:1] + log1m[:, 0:1]
        c_ref[...] = c_new
        return jnp.max(c_new) > SB_EXIT

    acc_ref[...] = jnp.zeros_like(acc_ref)
    c_ref[...] = jnp.zeros_like(c_ref)
    go = block(qi, True)

    def cond(carry):
        j, more = carry
        return jnp.logical_and(j >= 0, more)

    def body(carry):
        j, _ = carry
        more = block(j, False)
        return j - 1, more

    lax.while_loop(cond, body, (qi - 1, go))

    o = acc_ref[...]
    ms = jnp.mean(o * o, axis=-1, keepdims=True)
    o_ref[...] = (o * lax.rsqrt(ms + RMS_EPS) * g_ref[...]).astype(o_ref.dtype)


def sb_attention(h, norm_g, *, batch, seq, blk):
    n = batch * seq
    nq = seq // blk
    qb, kb, vb = OFF_SB_Q // SB_DH, OFF_SB_K // SB_DH, OFF_SB_V // SB_DH
    kern = functools.partial(_sb_kernel, blk=blk, scale=SB_DH ** -0.5)
    return pl.pallas_call(
        kern,
        grid=(batch, SB_HEADS, nq),
        in_specs=[pl.BlockSpec((blk, SB_DH), lambda b, hh, i: (b * nq + i, qb + hh)),
                  pl.BlockSpec((seq, SB_DH), lambda b, hh, i: (b, kb + hh)),
                  pl.BlockSpec((seq, SB_DH), lambda b, hh, i: (b, vb + hh)),
                  pl.BlockSpec((1, SB_DH), lambda b, hh, i: (0, 0))],
        out_specs=pl.BlockSpec((blk, SB_DH), lambda b, hh, i: (b * nq + i, hh)),
        out_shape=jax.ShapeDtypeStruct((n, SB_WIDTH), BF16),
        scratch_shapes=[pltpu.VMEM((blk, SB_DH), F32), pltpu.VMEM((blk, 1), F32)],
        compiler_params=_params(("parallel", "parallel", "arbitrary")),
        name="sb_attention",
    )(h, h, h, norm_g.reshape(1, SB_DH))


def _conv_kernel(b_ref, c_ref, h_ref, w_ref, o_ref):
    u = c_ref[...].astype(F32) * h_ref[...].astype(F32)
    t = lax.broadcasted_iota(jnp.int32, u.shape, 0)
    u1 = jnp.where(t >= 1, pltpu.roll(u, 1, 0), 0.0)
    u2 = jnp.where(t >= 2, pltpu.roll(u, 2, 0), 0.0)
    w = w_ref[...]
    y = w[0:1, :] * u2 + w[1:2, :] * u1 + w[2:3, :] * u
    o_ref[...] = (b_ref[...].astype(F32) * y).astype(o_ref.dtype)


def gated_conv(h, conv_w, *, batch, seq):
    n = batch * seq
    nb = CONV_WIDTH // LANES
    bb, cb, hb = OFF_CV_B // LANES, OFF_CV_C // LANES, OFF_CV_H // LANES
    return pl.pallas_call(
        _conv_kernel,
        grid=(batch, nb),
        in_specs=[pl.BlockSpec((seq, LANES), lambda b, j: (b, bb + j)),
                  pl.BlockSpec((seq, LANES), lambda b, j: (b, cb + j)),
                  pl.BlockSpec((seq, LANES), lambda b, j: (b, hb + j)),
                  pl.BlockSpec((CONV_K, LANES), lambda b, j: (0, j))],
        out_specs=pl.BlockSpec((seq, LANES), lambda b, j: (b, j)),
        out_shape=jax.ShapeDtypeStruct((n, CONV_WIDTH), BF16),
        compiler_params=_params(("parallel", "parallel")),
        name="gated_conv",
    )(h, h, h, conv_w)


GLA_SEG = 128


def _gla_kernel(q_ref, k_ref, v_ref, g_ref, r_ref, wup_ref, b_ref, gn_ref, o_ref, s_ref, *, n_seg):
    seg_len, chunk = GLA_SEG, GLA_CHUNK
    n_chunk = seg_len // chunk
    two_dk, two_dv = 2 * GLA_DK, 2 * GLA_DV

    row = lax.broadcasted_iota(jnp.int32, (seg_len, seg_len), 0)
    col = lax.broadcasted_iota(jnp.int32, (seg_len, seg_len), 1)
    same = (row // chunk) == (col // chunk)
    cum_m = jnp.logical_and(same, col <= row).astype(BF16)
    tot_m = same.astype(BF16)
    r2 = lax.broadcasted_iota(jnp.int32, (two_dk, two_dv), 0)
    c2 = lax.broadcasted_iota(jnp.int32, (two_dk, two_dv), 1)
    head_bd = (r2 // GLA_DK) == (c2 // GLA_DV)
    ones_bd = head_bd.astype(BF16)
    bd_f = head_bd.astype(F32)
    t_idx = lax.broadcasted_iota(jnp.int32, (seg_len, 1), 0)
    t_in_chunk = t_idx % chunk
    chunk_id = t_idx // chunk

    wup = wup_ref[...]
    bias = b_ref[...]
    gn = gn_ref[...]
    s_ref[...] = jnp.zeros_like(s_ref)

    def rms(x):
        return x * lax.rsqrt(jnp.mean(x * x, axis=-1, keepdims=True) + RMS_EPS)

    def seg(si, carry):
        rows = pl.ds(pl.multiple_of(si * seg_len, seg_len), seg_len)
        q = q_ref[rows, :].astype(F32) * (GLA_DK ** -0.5)
        k = k_ref[rows, :].astype(F32)
        v = v_ref[rows, :].astype(F32)
        u = _dot(r_ref[rows, :], wup) + bias
        la = (jnp.minimum(u, 0.0) - jnp.log1p(jnp.exp(-jnp.abs(u)))) * (1.0 / GLA_TAU)
        hi, lo = _split_bf16(la)
        bcum = _dot(cum_m, hi) + _dot(cum_m, lo)
        blast = _dot(tot_m, hi) + _dot(tot_m, lo)

        o_acc = _dot((q * k).astype(BF16), ones_bd) * v
        for d in range(1, chunk):
            kd = pltpu.roll(k, d, 0)
            bd = pltpu.roll(bcum, d, 0)
            vd = pltpu.roll(v, d, 0)
            p = q * kd * jnp.exp(jnp.minimum(bcum - bd, 0.0))
            p = jnp.where(t_in_chunk >= d, p, 0.0)
            o_acc = o_acc + _dot(p.astype(BF16), ones_bd) * vd

        q_in = q * jnp.exp(bcum)
        k_up = k * jnp.exp(blast - bcum)
        k_up_t = k_up.T.astype(BF16)
        g_t = jnp.exp(blast.T)
        state = s_ref[...]
        for c in range(n_chunk):
            in_c = chunk_id == c
            q_c = jnp.where(in_c, q_in, 0.0).astype(BF16)
            o_acc = o_acc + _dot(q_c, state.astype(BF16))
            v_c = jnp.where(in_c, v, 0.0).astype(BF16)
            kv = _dot(k_up_t, v_c)
            g_c = g_t[:, c * chunk:c * chunk + 1]
            state = g_c * state + kv * bd_f
        s_ref[...] = state

        gate = g_ref[rows, :].astype(F32)
        o0 = rms(o_acc[:, :GLA_DV]) * gn
        o1 = rms(o_acc[:, GLA_DV:]) * gn
        out = jnp.concatenate([o0, o1], axis=1) * _silu(gate)
        o_ref[rows, :] = out.astype(o_ref.dtype)
        return carry

    lax.fori_loop(0, n_seg, seg, 0)


def gla(h, h_r, w_up_pad, gla_b, gla_norm_g, *, batch, seq):
    n = batch * seq
    pairs = GLA_HEADS // 2
    qb, kb = OFF_G_Q // LANES, OFF_G_K // LANES
    vb, gb = OFF_G_V // (2 * GLA_DV), OFF_G_G // (2 * GLA_DV)
    kern = functools.partial(_gla_kernel, n_seg=seq // GLA_SEG)
    return pl.pallas_call(
        kern,
        grid=(batch, pairs),
        in_specs=[pl.BlockSpec((seq, LANES), lambda b, p: (b, qb + p)),
                  pl.BlockSpec((seq, LANES), lambda b, p: (b, kb + p)),
                  pl.BlockSpec((seq, 2 * GLA_DV), lambda b, p: (b, vb + p)),
                  pl.BlockSpec((seq, 2 * GLA_DV), lambda b, p: (b, gb + p)),
                  pl.BlockSpec((seq, LANES), lambda b, p: (b, 0)),
                  pl.BlockSpec((LANES, LANES), lambda b, p: (0, p)),
                  pl.BlockSpec((1, LANES), lambda b, p: (0, p)),
                  pl.BlockSpec((1, GLA_DV), lambda b, p: (0, 0))],
        out_specs=pl.BlockSpec((seq, 2 * GLA_DV), lambda b, p: (b, p)),
        out_shape=jax.ShapeDtypeStruct((n, GLA_VW), BF16),
        scratch_shapes=[pltpu.VMEM((2 * GLA_DK, 2 * GLA_DV), F32)],
        compiler_params=_params(("parallel", "parallel")),
        name="gla",
    )(h, h, h, h, h_r, w_up_pad, gla_b.reshape(1, GLA_KW), gla_norm_g.reshape(1, GLA_DV))


def _layer_norm(y, g, b):
    mu = jnp.mean(y, axis=-1, keepdims=True)
    yc = y - mu
    var = jnp.mean(yc * yc, axis=-1, keepdims=True)
    return yc * lax.rsqrt(var + LN_EPS) * g + b


def _outproj_ln_kernel(sb_ref, cv_ref, gl_ref, w_ref, x_ref, g_ref, b_ref, xo_ref, xb_ref, wb_ref, *, alpha):
    @pl.when(pl.program_id(0) == 0)
    def _():
        wb_ref[...] = w_ref[...].astype(BF16)

    mix = _dot(sb_ref[...], wb_ref[0:SB_WIDTH, :])
    mix = mix + _dot(cv_ref[...], wb_ref[SB_WIDTH:SB_WIDTH + CONV_WIDTH, :])
    mix = mix + _dot(gl_ref[...], wb_ref[SB_WIDTH + CONV_WIDTH:, :])
    out = _layer_norm(alpha * x_ref[...] + mix, g_ref[...], b_ref[...])
    xo_ref[...] = out
    xb_ref[...] = out.astype(BF16).reshape(xb_ref.shape)


def _token_major_spec(tm, d, tiled):
    if tiled:
        return pl.BlockSpec((tm, d // LANES, LANES), lambda i: (i, 0, 0))
    return pl.BlockSpec((tm, d), lambda i: (i, 0))


def _token_major_shape(n, d, tiled):
    return (n, d // LANES, LANES) if tiled else (n, d)


def outproj_ln(o_sb, o_cv, o_gla, w_out, layer, x, ln_g, ln_b, *, alpha, tm, tiled):
    n, d = x.shape
    kern = functools.partial(_outproj_ln_kernel, alpha=alpha)
    row = lambda i: (i, 0)
    fixed = lambda i: (0, 0)
    return pl.pallas_call(
        kern,
        grid=(n // tm,),
        in_specs=[pl.BlockSpec((tm, SB_WIDTH), row),
                  pl.BlockSpec((tm, CONV_WIDTH), row),
                  pl.BlockSpec((tm, GLA_VW), row),
                  pl.BlockSpec((None, d, d), lambda i: (layer, 0, 0), pipeline_mode=pl.Buffered(1)),
                  pl.BlockSpec((tm, d), row),
                  pl.BlockSpec((1, d), fixed),
                  pl.BlockSpec((1, d), fixed)],
        out_specs=[pl.BlockSpec((tm, d), row), _token_major_spec(tm, d, tiled)],
        out_shape=[jax.ShapeDtypeStruct((n, d), F32),
                   jax.ShapeDtypeStruct(_token_major_shape(n, d, tiled), BF16)],
        scratch_shapes=[pltpu.VMEM((d, d), BF16)],
        compiler_params=_params(("arbitrary",)),
        name="outproj_ln",
    )(o_sb, o_cv, o_gla, w_out, x, ln_g.reshape(1, d), ln_b.reshape(1, d))


def _residual_ln_kernel(x_ref, f_ref, g_ref, b_ref, xo_ref, xb_ref, *, alpha):
    f = f_ref[...].reshape(x_ref.shape).astype(F32)
    out = _layer_norm(alpha * x_ref[...] + f, g_ref[...], b_ref[...])
    xo_ref[...] = out
    xb_ref[...] = out.astype(BF16)


def residual_ln(x, f, ln_g, ln_b, *, alpha, tm):
    n, d = x.shape
    kern = functools.partial(_residual_ln_kernel, alpha=alpha)
    row = lambda i: (i, 0)
    fixed = lambda i: (0, 0)
    return pl.pallas_call(
        kern,
        grid=(n // tm,),
        in_specs=[pl.BlockSpec((tm, d), row), _token_major_spec(tm, d, True),
                  pl.BlockSpec((1, d), fixed), pl.BlockSpec((1, d), fixed)],
        out_specs=[pl.BlockSpec((tm, d), row), pl.BlockSpec((tm, d), row)],
        out_shape=[jax.ShapeDtypeStruct((n, d), F32), jax.ShapeDtypeStruct((n, d), BF16)],
        compiler_params=_params(("parallel",)),
        name="residual_ln",
    )(x, f, ln_g.reshape(1, d), ln_b.reshape(1, d))


def _ffn_kernel(te_ref, valid_ref, x_ref, wg_ref, wu_ref, wd_ref, o_ref, acc_ref):
    i, f = pl.program_id(0), pl.program_id(1)

    @pl.when(f == 0)
    def _():
        acc_ref[...] = jnp.zeros_like(acc_ref)

    @pl.when(valid_ref[i] == 1)
    def _():
        x = x_ref[...]
        hg = _dot(x, wg_ref[...].astype(BF16))
        hu = _dot(x, wu_ref[...].astype(BF16))
        hidden = (_silu(hg) * hu).astype(BF16)
        acc_ref[...] += _dot(hidden, wd_ref[...].astype(BF16))

    @pl.when(f == pl.num_programs(1) - 1)
    def _():
        o_ref[...] = acc_ref[...].astype(BF16).reshape(o_ref.shape)


def ffn(xb, w_gate, w_up, w_down, layer, tile_expert, tile_valid, *, tm, tf):
    r, d = xb.shape
    d_ff = w_gate.shape[-1]
    n_f = d_ff // tf

    def w_col(i, f, te, va):
        return layer, te[i], 0, jnp.where(va[i] == 1, f, n_f - 1)

    def w_row(i, f, te, va):
        return layer, te[i], jnp.where(va[i] == 1, f, n_f - 1), 0

    grid_spec = pltpu.PrefetchScalarGridSpec(
        num_scalar_prefetch=2,
        grid=(r // tm, n_f),
        in_specs=[pl.BlockSpec((tm, d), lambda i, f, te, va: (i, 0)),
                  pl.BlockSpec((None, None, d, tf), w_col),
                  pl.BlockSpec((None, None, d, tf), w_col),
                  pl.BlockSpec((None, None, tf, d), w_row)],
        out_specs=pl.BlockSpec((tm, d // LANES, LANES), lambda i, f, te, va: (i, 0, 0)),
        scratch_shapes=[pltpu.VMEM((tm, d), F32)],
    )
    return pl.pallas_call(
        _ffn_kernel,
        grid_spec=grid_spec,
        out_shape=jax.ShapeDtypeStruct((r, d // LANES, LANES), BF16),
        compiler_params=_params(("parallel", "arbitrary")),
        name="ffn",
    )(tile_expert, tile_valid, xb, w_gate, w_up, w_down)


def _router_kernel(x_ref, w_ref, o_ref):
    logits = jnp.dot(x_ref[...], w_ref[...], preferred_element_type=F32, precision=lax.Precision.HIGHEST)
    col = lax.broadcasted_iota(jnp.int32, logits.shape, 1)
    neg = jnp.float32(-jnp.inf)
    l1 = jnp.where(col < N_EXPERTS, logits, neg)
    m1 = jnp.max(l1, axis=-1, keepdims=True)
    i1 = jnp.min(jnp.where(l1 == m1, col, LANES), axis=-1, keepdims=True)
    l2 = jnp.where(col == i1, neg, l1)
    m2 = jnp.max(l2, axis=-1, keepdims=True)
    i2 = jnp.min(jnp.where(l2 == m2, col, LANES), axis=-1, keepdims=True)
    e2 = jnp.exp(m2 - m1)
    w1 = 1.0 / (1.0 + e2)
    w2 = e2 / (1.0 + e2)
    out = jnp.where(col == 0, i1.astype(F32), 0.0) + jnp.where(col == 1, i2.astype(F32), 0.0)
    out = out + jnp.where(col == 2, w1, 0.0) + jnp.where(col == 3, w2, 0.0)
    o_ref[...] = out


def router(x, w_router_pad, *, tm):
    n, d = x.shape
    return pl.pallas_call(
        _router_kernel,
        grid=(n // tm,),
        in_specs=[pl.BlockSpec((tm, d), lambda i: (i, 0)),
                  pl.BlockSpec((d, LANES), lambda i: (0, 0))],
        out_specs=pl.BlockSpec((tm, LANES), lambda i: (i, 0)),
        out_shape=jax.ShapeDtypeStruct((n, LANES), F32),
        compiler_params=_params(("parallel",)),
        name="router",
    )(x, w_router_pad)


def _gather_kernel(idx_ref, x_hbm, o_ref, buf, sem, *, rows):
    def row_copy(r):
        tok = idx_ref[0, 0, r]
        return pltpu.make_async_copy(x_hbm.at[tok], buf.at[r], sem)

    def issue(r, c):
        row_copy(r).start()
        return c

    def wait(r, c):
        row_copy(r).wait()
        return c

    lax.fori_loop(0, rows, issue, 0)
    lax.fori_loop(0, rows, wait, 0)
    o_ref[...] = buf[...].reshape(o_ref.shape)


def gather_rows(x3, src, *, rows):
    _, s, l = x3.shape
    r = src.shape[0]
    kern = functools.partial(_gather_kernel, rows=rows)
    return pl.pallas_call(
        kern,
        grid=(r // rows,),
        in_specs=[pl.BlockSpec((1, 1, rows), lambda i: (i, 0, 0), memory_space=pltpu.SMEM),
                  pl.BlockSpec(memory_space=pl.ANY)],
        out_specs=pl.BlockSpec((rows, s * l), lambda i: (i, 0)),
        out_shape=jax.ShapeDtypeStruct((r, s * l), BF16),
        scratch_shapes=[pltpu.VMEM((rows, s, l), BF16), pltpu.SemaphoreType.DMA(())],
        compiler_params=_params(("arbitrary",)),
        name="gather_rows",
    )(src.reshape(r // rows, 1, rows), x3)


def _combine_ln_kernel(pos_ref, w_ref, x_ref, y_hbm, g_ref, b_ref, xo_ref, xb_ref,
                       buf_a, buf_b, sem_a, sem_b, *, rows, alpha):
    def row_copy(r, off, buf, sem):
        p = pos_ref[0, 0, off + r]
        return pltpu.make_async_copy(y_hbm.at[p], buf.at[r], sem)

    def issue(r, c):
        row_copy(r, 0, buf_a, sem_a).start()
        row_copy(r, rows, buf_b, sem_b).start()
        return c

    def wait(r, c):
        row_copy(r, 0, buf_a, sem_a).wait()
        row_copy(r, rows, buf_b, sem_b).wait()
        return c

    lax.fori_loop(0, rows, issue, 0)
    lax.fori_loop(0, rows, wait, 0)
    w = w_ref[...]
    ya = buf_a[...].reshape(x_ref.shape).astype(F32)
    yb = buf_b[...].reshape(x_ref.shape).astype(F32)
    y = w[:, 2:3] * ya + w[:, 3:4] * yb
    out = _layer_norm(alpha * x_ref[...] + y, g_ref[...], b_ref[...])
    xo_ref[...] = out
    xb_ref[...] = out.astype(BF16)


def combine_ln(x, y, pos, route, ln_g, ln_b, *, alpha, rows):
    n, d = x.shape
    tile = (rows, d // LANES, LANES)
    kern = functools.partial(_combine_ln_kernel, rows=rows, alpha=alpha)
    row = lambda i: (i, 0)
    fixed = lambda i: (0, 0)
    return pl.pallas_call(
        kern,
        grid=(n // rows,),
        in_specs=[pl.BlockSpec((1, 1, 2 * rows), lambda i: (i, 0, 0), memory_space=pltpu.SMEM),
                  pl.BlockSpec((rows, LANES), row),
                  pl.BlockSpec((rows, d), row),
                  pl.BlockSpec(memory_space=pl.ANY),
                  pl.BlockSpec((1, d), fixed),
                  pl.BlockSpec((1, d), fixed)],
        out_specs=[pl.BlockSpec((rows, d), row), pl.BlockSpec((rows, d), row)],
        out_shape=[jax.ShapeDtypeStruct((n, d), F32), jax.ShapeDtypeStruct((n, d), BF16)],
        scratch_shapes=[pltpu.VMEM(tile, BF16), pltpu.VMEM(tile, BF16),
                        pltpu.SemaphoreType.DMA(()), pltpu.SemaphoreType.DMA(())],
        compiler_params=_params(("arbitrary",)),
        name="combine_ln",
    )(pos, route, x, y, ln_g.reshape(1, d), ln_b.reshape(1, d))


def moe_block(xf, xb3, w_router, w_gate, w_up, w_down, layer, ln_g, ln_b, *, alpha, tm, tf, rows):
    n, d = xf.shape
    n_e = w_gate.shape[1]
    w_r = jnp.pad(w_router, ((0, 0), (0, LANES - n_e)))
    route = router(xf, w_r, tm=rows)
    ef = route[:, 0:2].astype(jnp.int32).reshape(-1)
    onehot = (ef[:, None] == jnp.arange(n_e, dtype=jnp.int32)[None, :]).astype(jnp.int32)
    csum = jnp.cumsum(onehot, axis=0)
    rank = jnp.take_along_axis(csum, ef[:, None], axis=1)[:, 0] - 1
    counts = csum[-1]
    padded = ((counts + tm - 1) // tm) * tm
    ends = jnp.cumsum(padded)
    pos = (ends - padded)[ef] + rank
    r_rows = 2 * n + n_e * tm
    src = jnp.zeros((r_rows,), jnp.int32).at[pos].set(jnp.arange(2 * n, dtype=jnp.int32) // 2)
    tile_start = jnp.arange(r_rows // tm, dtype=jnp.int32) * tm
    tile_valid = (tile_start < ends[-1]).astype(jnp.int32)
    te_raw = jnp.minimum(jnp.searchsorted(ends, tile_start, side="right").astype(jnp.int32), n_e - 1)
    last_e = te_raw[jnp.maximum(ends[-1] // tm - 1, 0)]
    tile_expert = jnp.where(tile_valid == 1, te_raw, last_e)

    xs = gather_rows(xb3, src, rows=rows)
    ys = ffn(xs, w_gate, w_up, w_down, layer, tile_expert, tile_valid, tm=tm, tf=tf)
    pos2 = pos.reshape(n // rows, rows, 2).transpose(0, 2, 1).reshape(n // rows, 1, 2 * rows)
    return combine_ln(xf, ys, pos2, route, ln_g, ln_b, alpha=alpha, rows=rows)


def kernel(x, w_in, sb_norm_g, conv_w, gla_w_up, gla_b, gla_norm_g, w_out, ln1_g, ln1_b,
           ffn_w_gate, ffn_w_up, ffn_w_down, moe_w_router, moe_w_gate, moe_w_up, moe_w_down,
           ln2_g, ln2_b):
    batch, seq, d = x.shape
    depth = w_in.shape[0]
    n = batch * seq
    alpha = (2 * depth) ** 0.25

    xf = x.reshape(n, d)
    xb = xf.astype(BF16)
    for layer in range(depth):
        w_r_pad = jnp.pad(w_in[layer, :, OFF_G_R:], ((0, 0), (0, LANES - GLA_RANK))).astype(BF16)
        h, h_r = in_proj(xb, w_in, w_r_pad, layer, tm=1024, tn=1024)
        o_sb = sb_attention(h, sb_norm_g[layer], batch=batch, seq=seq, blk=256)
        o_cv = gated_conv(h, conv_w[layer], batch=batch, seq=seq)
        w_up_pad = jnp.pad(gla_w_up[layer], ((0, LANES - GLA_RANK), (0, 0))).astype(BF16)
        o_gla = gla(h, h_r, w_up_pad, gla_b[layer], gla_norm_g[layer], batch=batch, seq=seq)
        is_moe = layer % 2 == 1
        xf, xb = outproj_ln(o_sb, o_cv, o_gla, w_out, layer, xf, ln1_g[layer], ln1_b[layer],
                            alpha=alpha, tm=256, tiled=is_moe)
        j = layer // 2
        if is_moe:
            xf, xb = moe_block(xf, xb, moe_w_router[j], moe_w_gate, moe_w_up, moe_w_down, j,
                               ln2_g[layer], ln2_b[layer], alpha=alpha, tm=1024, tf=256, rows=256)
        else:
            ones = jnp.ones((n // 1024,), jnp.int32)
            f = ffn(xb, ffn_w_gate[:, None], ffn_w_up[:, None], ffn_w_down[:, None], j, ones * 0, ones,
                    tm=1024, tf=256)
            xf, xb = residual_ln(xf, f, ln2_g[layer], ln2_b[layer], alpha=alpha, tm=512)
    return xf.reshape(batch, seq, d)
```

```python
import functools

import jax
import jax.numpy as jnp
from jax import lax
from jax.experimental import pallas as pl
from jax.experimental.pallas import tpu as pltpu

F32 = jnp.float32
BF16 = jnp.bfloat16

D_MODEL = 2048
SB_HEADS = 8
SB_DH = 128
SB_WIDTH = SB_HEADS * SB_DH
CONV_WIDTH = 512
CONV_K = 3
GLA_HEADS = 4
GLA_DK = 64
GLA_DV = 128
GLA_KW = GLA_HEADS * GLA_DK
GLA_VW = GLA_HEADS * GLA_DV
GLA_RANK = 16
GLA_TAU = 16.0
GLA_CHUNK = 16
IN_COLS = 3 * SB_WIDTH + 3 * CONV_WIDTH + 2 * GLA_KW + 2 * GLA_VW + GLA_RANK
N_EXPERTS = 8
LN_EPS = 1e-5
RMS_EPS = 1e-6

LANES = 128
IN_COLS_PAD = 6272
VMEM_LIMIT = 56 * 1024 * 1024

OFF_SB_Q, OFF_SB_K, OFF_SB_V = 0, SB_WIDTH, 2 * SB_WIDTH
OFF_CV_B = 3 * SB_WIDTH
OFF_CV_C = OFF_CV_B + CONV_WIDTH
OFF_CV_H = OFF_CV_C + CONV_WIDTH
OFF_G_Q = OFF_CV_H + CONV_WIDTH
OFF_G_K = OFF_G_Q + GLA_KW
OFF_G_V = OFF_G_K + GLA_KW
OFF_G_G = OFF_G_V + GLA_VW
OFF_G_R = OFF_G_G + GLA_VW

SB_EXIT = -90.0


def _params(sem):
    return pltpu.CompilerParams(dimension_semantics=sem, vmem_limit_bytes=VMEM_LIMIT)


def _split_bf16(x):
    hi = x.astype(BF16)
    lo = (x - hi.astype(F32)).astype(BF16)
    return hi, lo


def _dot(a, b):
    return jnp.dot(a, b, preferred_element_type=F32)


def _softplus(z):
    return jnp.maximum(z, 0.0) + jnp.log1p(jnp.exp(-jnp.abs(z)))


def _silu(x):
    return x * (1.0 / (1.0 + jnp.exp(-x)))


def _in_proj_kernel(x_ref, w_ref, wr_ref, h_ref, r_ref):
    x = x_ref[...]
    h_ref[...] = _dot(x, w_ref[...].astype(BF16)).astype(h_ref.dtype)

    @pl.when(pl.program_id(1) == 0)
    def _():
        r_ref[...] = _dot(x, wr_ref[...]).astype(r_ref.dtype)


def in_proj(xb, w_in, w_r_pad, layer, *, tm, tn):
    n, d = xb.shape
    return pl.pallas_call(
        _in_proj_kernel,
        grid=(n // tm, OFF_G_R // tn),
        in_specs=[pl.BlockSpec((tm, d), lambda i, j: (i, 0)),
                  pl.BlockSpec((None, d, tn), lambda i, j: (layer, 0, j)),
                  pl.BlockSpec((d, LANES), lambda i, j: (0, 0))],
        out_specs=[pl.BlockSpec((tm, tn), lambda i, j: (i, j)),
                   pl.BlockSpec((tm, LANES), lambda i, j: (i, 0))],
        out_shape=[jax.ShapeDtypeStruct((n, OFF_G_R), BF16), jax.ShapeDtypeStruct((n, LANES), BF16)],
        compiler_params=_params(("parallel", "arbitrary")),
        name="in_proj",
    )(xb, w_in, w_r_pad)


SB_BLK = 128
SB_WIN = 2 * SB_BLK
SB_GROUP = 8


LOG2E = 1.4426950408889634
SB_EXIT_LOG2 = SB_EXIT * LOG2E


def _minus_later_keys_matrix(n_keys):
    r = lax.broadcasted_iota(jnp.int32, (2 * n_keys, n_keys), 0)
    c = lax.broadcasted_iota(jnp.int32, (2 * n_keys, n_keys), 1)
    return jnp.where((r % n_keys) > c, -1.0, 0.0).astype(BF16)


def _sb_scores(q, k, scale):
    y = lax.dot_general(q, k, (((1,), (1,)), ((), ())), preferred_element_type=F32) * (scale * LOG2E)
    sp = jnp.maximum(y, 0.0) + jnp.log2(1.0 + jnp.exp2(-jnp.abs(y)))
    return y, sp


def _sb_tail(sp_masked, minus_later):
    hi, lo = _split_bf16(sp_masked)
    return _dot(jnp.concatenate([hi, lo], axis=1), minus_later)


def _sb_kernel(q_ref, k_ref, v_ref, g_ref, o_ref, acc_ref, c_ref, *, group, scale):
    minus_later_win = _minus_later_keys_matrix(SB_WIN)
    minus_later_blk = _minus_later_keys_matrix(SB_BLK)
    row = lax.broadcasted_iota(jnp.int32, (SB_BLK, SB_WIN), 0)
    col = lax.broadcasted_iota(jnp.int32, (SB_BLK, SB_WIN), 1)
    causal_prev = col < row + SB_BLK
    gain = g_ref[...]

    def q_rows(qi):
        return pl.ds(pl.multiple_of(qi * SB_BLK, SB_BLK), SB_BLK)

    def windows(qis):
        wss = [pl.multiple_of(jnp.maximum(qi - 1, 0) * SB_BLK, SB_BLK) for qi in qis]
        causal = [col < row + (qis[0] * SB_BLK - wss[0])] + [causal_prev] * (len(qis) - 1)
        scores = [_sb_scores(q_ref[q_rows(qi), :], k_ref[pl.ds(ws, SB_WIN), :], scale) for qi, ws in zip(qis, wss)]
        sp_m = [jnp.where(m, sp, 0.0) for m, (_, sp) in zip(causal, scores)]
        tails = [_sb_tail(s, minus_later_win) for s in sp_m]
        c_max = None
        for u, (m, (y, sp), s, tail, ws) in enumerate(zip(causal, scores, sp_m, tails, wss)):
            a = jnp.where(m, jnp.exp2(y - sp + tail), 0.0)
            acc_ref[u] = _dot(a.astype(BF16), v_ref[pl.ds(ws, SB_WIN), :])
            c = tail[:, 0:1] - s[:, 0:1]
            c_ref[u] = c
            c_max = c if c_max is None else jnp.maximum(c_max, c)
        return c_max

    def earlier_blocks(qi, u):
        q = q_ref[q_rows(qi), :]

        def body(carry):
            j, _ = carry
            start = pl.multiple_of(j * SB_BLK, SB_BLK)
            y, sp = _sb_scores(q, k_ref[pl.ds(start, SB_BLK), :], scale)
            tail = _sb_tail(sp, minus_later_blk)
            c = c_ref[u]
            a = jnp.exp2(y - sp + tail + c)
            acc_ref[u] += _dot(a.astype(BF16), v_ref[pl.ds(start, SB_BLK), :])
            c_new = c + tail[:, 0:1] - sp[:, 0:1]
            c_ref[u] = c_new
            return j - 1, jnp.max(c_new) > SB_EXIT_LOG2

        def cond(carry):
            j, more = carry
            return jnp.logical_and(j >= 0, more)

        lax.while_loop(cond, body, (jnp.maximum(qi - 1, 0) - 1, jnp.max(c_ref[u]) > SB_EXIT_LOG2))

    def run_group(gi, carry):
        c_max = windows([gi * group + u for u in range(group)])

        @pl.when(jnp.max(c_max) > SB_EXIT_LOG2)
        def _():
            for u in range(group):
                earlier_blocks(gi * group + u, u)

        for u in range(group):
            o = acc_ref[u]
            ms = jnp.mean(o * o, axis=-1, keepdims=True)
            rows = pl.ds(pl.multiple_of((gi * group + u) * SB_BLK, SB_BLK), SB_BLK)
            o_ref[rows, :] = (o * lax.rsqrt(ms + RMS_EPS) * gain).astype(o_ref.dtype)
        return carry

    lax.fori_loop(0, q_ref.shape[0] // (SB_BLK * group), run_group, 0)


def sb_attention(h, norm_g, *, batch, seq):
    n = batch * seq
    qb, kb, vb = OFF_SB_Q // SB_DH, OFF_SB_K // SB_DH, OFF_SB_V // SB_DH
    group = min(SB_GROUP, seq // SB_BLK)
    kern = functools.partial(_sb_kernel, group=group, scale=SB_DH ** -0.5)
    return pl.pallas_call(
        kern,
        grid=(batch, SB_HEADS),
        in_specs=[pl.BlockSpec((seq, SB_DH), lambda b, hh: (b, qb + hh)),
                  pl.BlockSpec((seq, SB_DH), lambda b, hh: (b, kb + hh)),
                  pl.BlockSpec((seq, SB_DH), lambda b, hh: (b, vb + hh)),
                  pl.BlockSpec((1, SB_DH), lambda b, hh: (0, 0))],
        out_specs=pl.BlockSpec((seq, SB_DH), lambda b, hh: (b, hh)),
        out_shape=jax.ShapeDtypeStruct((n, SB_WIDTH), BF16),
        scratch_shapes=[pltpu.VMEM((group, SB_BLK, SB_DH), F32), pltpu.VMEM((group, SB_BLK, 1), F32)],
        compiler_params=_params(("parallel", "parallel")),
        name="sb_attention",
    )(h, h, h, norm_g.reshape(1, SB_DH))


def _conv_kernel(b_ref, c_ref, h_ref, w_ref, o_ref):
    u = c_ref[...].astype(F32) * h_ref[...].astype(F32)
    t = lax.broadcasted_iota(jnp.int32, u.shape, 0)
    u1 = jnp.where(t >= 1, pltpu.roll(u, 1, 0), 0.0)
    u2 = jnp.where(t >= 2, pltpu.roll(u, 2, 0), 0.0)
    w = w_ref[...]
    y = w[0:1, :] * u2 + w[1:2, :] * u1 + w[2:3, :] * u
    o_ref[...] = (b_ref[...].astype(F32) * y).astype(o_ref.dtype)


def gated_conv(h, conv_w, *, batch, seq):
    n = batch * seq
    nb = CONV_WIDTH // LANES
    bb, cb, hb = OFF_CV_B // LANES, OFF_CV_C // LANES, OFF_CV_H // LANES
    return pl.pallas_call(
        _conv_kernel,
        grid=(batch, nb),
        in_specs=[pl.BlockSpec((seq, LANES), lambda b, j: (b, bb + j)),
                  pl.BlockSpec((seq, LANES), lambda b, j: (b, cb + j)),
                  pl.BlockSpec((seq, LANES), lambda b, j: (b, hb + j)),
                  pl.BlockSpec((CONV_K, LANES), lambda b, j: (0, j))],
        out_specs=pl.BlockSpec((seq, LANES), lambda b, j: (b, j)),
        out_shape=jax.ShapeDtypeStruct((n, CONV_WIDTH), BF16),
        compiler_params=_params(("parallel", "parallel")),
        name="gated_conv",
    )(h, h, h, conv_w)


GLA_SEG = 128


def _gla_kernel(q_ref, k_ref, v_ref, g_ref, r_ref, wup_ref, b_ref, gn_ref, o_ref, s_ref, *, n_seg):
    seg_len, chunk = GLA_SEG, GLA_CHUNK
    n_chunk = seg_len // chunk
    two_dk, two_dv = 2 * GLA_DK, 2 * GLA_DV

    row = lax.broadcasted_iota(jnp.int32, (seg_len, seg_len), 0)
    col = lax.broadcasted_iota(jnp.int32, (seg_len, seg_len), 1)
    same = (row // chunk) == (col // chunk)
    cum_m = jnp.logical_and(same, col <= row).astype(BF16)
    tot_m = same.astype(BF16)
    r2 = lax.broadcasted_iota(jnp.int32, (two_dk, two_dv), 0)
    c2 = lax.broadcasted_iota(jnp.int32, (two_dk, two_dv), 1)
    head_bd = (r2 // GLA_DK) == (c2 // GLA_DV)
    ones_bd = head_bd.astype(BF16)
    bd_f = head_bd.astype(F32)
    t_idx = lax.broadcasted_iota(jnp.int32, (seg_len, 1), 0)
    t_in_chunk = t_idx % chunk
    chunk_id = t_idx // chunk

    wup = wup_ref[...]
    bias = b_ref[...]
    gn = gn_ref[...]
    s_ref[...] = jnp.zeros_like(s_ref)

    def rms(x):
        return x * lax.rsqrt(jnp.mean(x * x, axis=-1, keepdims=True) + RMS_EPS)

    def seg(si, carry):
        rows = pl.ds(pl.multiple_of(si * seg_len, seg_len), seg_len)
        q = q_ref[rows, :].astype(F32) * (GLA_DK ** -0.5)
        k = k_ref[rows, :].astype(F32)
        v = v_ref[rows, :].astype(F32)
        u = _dot(r_ref[rows, :], wup) + bias
        la = (jnp.minimum(u, 0.0) - jnp.log1p(jnp.exp(-jnp.abs(u)))) * (1.0 / GLA_TAU)
        hi, lo = _split_bf16(la)
        bcum = _dot(cum_m, hi) + _dot(cum_m, lo)
        blast = _dot(tot_m, hi) + _dot(tot_m, lo)

        o_acc = _dot((q * k).astype(BF16), ones_bd) * v
        for d in range(1, chunk):
            kd = pltpu.roll(k, d, 0)
            bd = pltpu.roll(bcum, d, 0)
            vd = pltpu.roll(v, d, 0)
            p = q * kd * jnp.exp(jnp.minimum(bcum - bd, 0.0))
            p = jnp.where(t_in_chunk >= d, p, 0.0)
            o_acc = o_acc + _dot(p.astype(BF16), ones_bd) * vd

        q_in = q * jnp.exp(bcum)
        k_up = k * jnp.exp(blast - bcum)
        k_up_t = k_up.T.astype(BF16)
        g_t = jnp.exp(blast.T)
        state = s_ref[...]
        for c in range(n_chunk):
            in_c = chunk_id == c
            q_c = jnp.where(in_c, q_in, 0.0).astype(BF16)
            o_acc = o_acc + _dot(q_c, state.astype(BF16))
            v_c = jnp.where(in_c, v, 0.0).astype(BF16)
            kv = _dot(k_up_t, v_c)
            g_c = g_t[:, c * chunk:c * chunk + 1]
            state = g_c * state + kv * bd_f
        s_ref[...] = state

        gate = g_ref[rows, :].astype(F32)
        o0 = rms(o_acc[:, :GLA_DV]) * gn
        o1 = rms(o_acc[:, GLA_DV:]) * gn
        out = jnp.concatenate([o0, o1], axis=1) * _silu(gate)
        o_ref[rows, :] = out.astype(o_ref.dtype)
        return carry

    lax.fori_loop(0, n_seg, seg, 0)


def gla(h, h_r, w_up_pad, gla_b, gla_norm_g, *, batch, seq):
    n = batch * seq
    pairs = GLA_HEADS // 2
    qb, kb = OFF_G_Q // LANES, OFF_G_K // LANES
    vb, gb = OFF_G_V // (2 * GLA_DV), OFF_G_G // (2 * GLA_DV)
    kern = functools.partial(_gla_kernel, n_seg=seq // GLA_SEG)
    return pl.pallas_call(
        kern,
        grid=(batch, pairs),
        in_specs=[pl.BlockSpec((seq, LANES), lambda b, p: (b, qb + p)),
                  pl.BlockSpec((seq, LANES), lambda b, p: (b, kb + p)),
                  pl.BlockSpec((seq, 2 * GLA_DV), lambda b, p: (b, vb + p)),
                  pl.BlockSpec((seq, 2 * GLA_DV), lambda b, p: (b, gb + p)),
                  pl.BlockSpec((seq, LANES), lambda b, p: (b, 0)),
                  pl.BlockSpec((LANES, LANES), lambda b, p: (0, p)),
                  pl.BlockSpec((1, LANES), lambda b, p: (0, p)),
                  pl.BlockSpec((1, GLA_DV), lambda b, p: (0, 0))],
        out_specs=pl.BlockSpec((seq, 2 * GLA_DV), lambda b, p: (b, p)),
        out_shape=jax.ShapeDtypeStruct((n, GLA_VW), BF16),
        scratch_shapes=[pltpu.VMEM((2 * GLA_DK, 2 * GLA_DV), F32)],
        compiler_params=_params(("parallel", "parallel")),
        name="gla",
    )(h, h, h, h, h_r, w_up_pad, gla_b.reshape(1, GLA_KW), gla_norm_g.reshape(1, GLA_DV))


def _layer_norm(y, g, b):
    mu = jnp.mean(y, axis=-1, keepdims=True)
    yc = y - mu
    var = jnp.mean(yc * yc, axis=-1, keepdims=True)
    return yc * lax.rsqrt(var + LN_EPS) * g + b


def _outproj_ln_kernel(sb_ref, cv_ref, gl_ref, w_ref, x_ref, g_ref, b_ref, xo_ref, xb_ref, wb_ref, *, alpha):
    @pl.when(pl.program_id(0) == 0)
    def _():
        wb_ref[...] = w_ref[...].astype(BF16)

    mix = _dot(sb_ref[...], wb_ref[0:SB_WIDTH, :])
    mix = mix + _dot(cv_ref[...], wb_ref[SB_WIDTH:SB_WIDTH + CONV_WIDTH, :])
    mix = mix + _dot(gl_ref[...], wb_ref[SB_WIDTH + CONV_WIDTH:, :])
    out = _layer_norm(alpha * x_ref[...] + mix, g_ref[...], b_ref[...])
    xo_ref[...] = out
    xb_ref[...] = out.astype(BF16).reshape(xb_ref.shape)


def _token_major_spec(tm, d, tiled):
    if tiled:
        return pl.BlockSpec((tm, d // LANES, LANES), lambda i: (i, 0, 0))
    return pl.BlockSpec((tm, d), lambda i: (i, 0))


def _token_major_shape(n, d, tiled):
    return (n, d // LANES, LANES) if tiled else (n, d)


def outproj_ln(o_sb, o_cv, o_gla, w_out, layer, x, ln_g, ln_b, *, alpha, tm, tiled):
    n, d = x.shape
    kern = functools.partial(_outproj_ln_kernel, alpha=alpha)
    row = lambda i: (i, 0)
    fixed = lambda i: (0, 0)
    return pl.pallas_call(
        kern,
        grid=(n // tm,),
        in_specs=[pl.BlockSpec((tm, SB_WIDTH), row),
                  pl.BlockSpec((tm, CONV_WIDTH), row),
                  pl.BlockSpec((tm, GLA_VW), row),
                  pl.BlockSpec((None, d, d), lambda i: (layer, 0, 0), pipeline_mode=pl.Buffered(1)),
                  pl.BlockSpec((tm, d), row),
                  pl.BlockSpec((1, d), fixed),
                  pl.BlockSpec((1, d), fixed)],
        out_specs=[pl.BlockSpec((tm, d), row), _token_major_spec(tm, d, tiled)],
        out_shape=[jax.ShapeDtypeStruct((n, d), F32),
                   jax.ShapeDtypeStruct(_token_major_shape(n, d, tiled), BF16)],
        scratch_shapes=[pltpu.VMEM((d, d), BF16)],
        compiler_params=_params(("arbitrary",)),
        name="outproj_ln",
    )(o_sb, o_cv, o_gla, w_out, x, ln_g.reshape(1, d), ln_b.reshape(1, d))


def _residual_ln_kernel(x_ref, f_ref, g_ref, b_ref, xo_ref, xb_ref, *, alpha):
    f = f_ref[...].reshape(x_ref.shape).astype(F32)
    out = _layer_norm(alpha * x_ref[...] + f, g_ref[...], b_ref[...])
    xo_ref[...] = out
    xb_ref[...] = out.astype(BF16)


def residual_ln(x, f, ln_g, ln_b, *, alpha, tm):
    n, d = x.shape
    kern = functools.partial(_residual_ln_kernel, alpha=alpha)
    row = lambda i: (i, 0)
    fixed = lambda i: (0, 0)
    return pl.pallas_call(
        kern,
        grid=(n // tm,),
        in_specs=[pl.BlockSpec((tm, d), row), _token_major_spec(tm, d, True),
                  pl.BlockSpec((1, d), fixed), pl.BlockSpec((1, d), fixed)],
        out_specs=[pl.BlockSpec((tm, d), row), pl.BlockSpec((tm, d), row)],
        out_shape=[jax.ShapeDtypeStruct((n, d), F32), jax.ShapeDtypeStruct((n, d), BF16)],
        compiler_params=_params(("parallel",)),
        name="residual_ln",
    )(x, f, ln_g.reshape(1, d), ln_b.reshape(1, d))


def _ffn_kernel(te_ref, valid_ref, x_ref, wg_ref, wu_ref, wd_ref, o_ref, acc_ref):
    i, f = pl.program_id(0), pl.program_id(1)

    @pl.when(f == 0)
    def _():
        acc_ref[...] = jnp.zeros_like(acc_ref)

    @pl.when(valid_ref[i] == 1)
    def _():
        x = x_ref[...]
        hg = _dot(x, wg_ref[...].astype(BF16))
        hu = _dot(x, wu_ref[...].astype(BF16))
        hidden = (_silu(hg) * hu).astype(BF16)
        acc_ref[...] += _dot(hidden, wd_ref[...].astype(BF16))

    @pl.when(f == pl.num_programs(1) - 1)
    def _():
        o_ref[...] = acc_ref[...].astype(BF16).reshape(o_ref.shape)


def ffn(xb, w_gate, w_up, w_down, layer, tile_expert, tile_valid, *, tm, tf):
    r, d = xb.shape
    d_ff = w_gate.shape[-1]
    n_f = d_ff // tf

    def w_col(i, f, te, va):
        return layer, te[i], 0, jnp.where(va[i] == 1, f, n_f - 1)

    def w_row(i, f, te, va):
        return layer, te[i], jnp.where(va[i] == 1, f, n_f - 1), 0

    grid_spec = pltpu.PrefetchScalarGridSpec(
        num_scalar_prefetch=2,
        grid=(r // tm, n_f),
        in_specs=[pl.BlockSpec((tm, d), lambda i, f, te, va: (i, 0), pipeline_mode=pl.Buffered(1)),
                  pl.BlockSpec((None, None, d, tf), w_col),
                  pl.BlockSpec((None, None, d, tf), w_col),
                  pl.BlockSpec((None, None, tf, d), w_row)],
        out_specs=pl.BlockSpec((tm, d // LANES, LANES), lambda i, f, te, va: (i, 0, 0), pipeline_mode=pl.Buffered(1)),
        scratch_shapes=[pltpu.VMEM((tm, d), F32)],
    )
    return pl.pallas_call(
        _ffn_kernel,
        grid_spec=grid_spec,
        out_shape=jax.ShapeDtypeStruct((r, d // LANES, LANES), BF16),
        compiler_params=_params(("parallel", "arbitrary")),
        name="ffn",
    )(tile_expert, tile_valid, xb, w_gate, w_up, w_down)


def _router_kernel(x_ref, w_ref, o_ref):
    logits = jnp.dot(x_ref[...], w_ref[...], preferred_element_type=F32, precision=lax.Precision.HIGHEST)
    col = lax.broadcasted_iota(jnp.int32, logits.shape, 1)
    neg = jnp.float32(-jnp.inf)
    l1 = jnp.where(col < N_EXPERTS, logits, neg)
    m1 = jnp.max(l1, axis=-1, keepdims=True)
    i1 = jnp.min(jnp.where(l1 == m1, col, LANES), axis=-1, keepdims=True)
    l2 = jnp.where(col == i1, neg, l1)
    m2 = jnp.max(l2, axis=-1, keepdims=True)
    i2 = jnp.min(jnp.where(l2 == m2, col, LANES), axis=-1, keepdims=True)
    e2 = jnp.exp(m2 - m1)
    w1 = 1.0 / (1.0 + e2)
    w2 = e2 / (1.0 + e2)
    out = jnp.where(col == 0, i1.astype(F32), 0.0) + jnp.where(col == 1, i2.astype(F32), 0.0)
    out = out + jnp.where(col == 2, w1, 0.0) + jnp.where(col == 3, w2, 0.0)
    o_ref[...] = out


def router(x, w_router_pad, *, tm):
    n, d = x.shape
    return pl.pallas_call(
        _router_kernel,
        grid=(n // tm,),
        in_specs=[pl.BlockSpec((tm, d), lambda i: (i, 0)),
                  pl.BlockSpec((d, LANES), lambda i: (0, 0))],
        out_specs=pl.BlockSpec((tm, LANES), lambda i: (i, 0)),
        out_shape=jax.ShapeDtypeStruct((n, LANES), F32),
        compiler_params=_params(("parallel",)),
        name="router",
    )(x, w_router_pad)


DMA_UNROLL = 8


def _gather_kernel(idx_ref, nxt_ref, x_hbm, o_ref, buf, sem, *, rows):
    i = pl.program_id(0)
    slot = i % 2

    def row_copy(ref, s, r):
        return pltpu.make_async_copy(x_hbm.at[ref[0, 0, r]], buf.at[s, r], sem.at[s])

    def issue(ref, s):
        def body(r, c):
            row_copy(ref, s, r).start()
            return c
        lax.fori_loop(0, rows, body, 0, unroll=DMA_UNROLL)

    @pl.when(i == 0)
    def _():
        issue(idx_ref, slot)

    @pl.when(i + 1 < pl.num_programs(0))
    def _():
        issue(nxt_ref, 1 - slot)

    def wait(r, c):
        row_copy(idx_ref, slot, r).wait()
        return c

    lax.fori_loop(0, rows, wait, 0, unroll=DMA_UNROLL)
    o_ref[...] = buf[slot].reshape(o_ref.shape)


def _step_and_next_specs(n_steps, width):
    cur = pl.BlockSpec((1, 1, width), lambda i: (i, 0, 0), memory_space=pltpu.SMEM)
    nxt = pl.BlockSpec((1, 1, width), lambda i: (jnp.minimum(i + 1, n_steps - 1), 0, 0), memory_space=pltpu.SMEM)
    return cur, nxt


def gather_rows(x3, src, *, rows):
    _, s, l = x3.shape
    r = src.shape[0]
    n_steps = r // rows
    kern = functools.partial(_gather_kernel, rows=rows)
    idx = src.reshape(n_steps, 1, rows)
    return pl.pallas_call(
        kern,
        grid=(n_steps,),
        in_specs=[*_step_and_next_specs(n_steps, rows), pl.BlockSpec(memory_space=pl.ANY)],
        out_specs=pl.BlockSpec((rows, s * l), lambda i: (i, 0)),
        out_shape=jax.ShapeDtypeStruct((r, s * l), BF16),
        scratch_shapes=[pltpu.VMEM((2, rows, s, l), BF16), pltpu.SemaphoreType.DMA((2,))],
        compiler_params=_params(("arbitrary",)),
        name="gather_rows",
    )(idx, idx, x3)


def _combine_ln_kernel(pos_ref, nxt_ref, w_ref, x_ref, y_hbm, g_ref, b_ref, xo_ref, xb_ref,
                       buf, sem, *, rows, alpha):
    i = pl.program_id(0)
    slot = i % 2

    def row_copy(ref, s, k, r):
        return pltpu.make_async_copy(y_hbm.at[ref[0, 0, k * rows + r]], buf.at[s, k, r], sem.at[s, k])

    def issue(ref, s):
        def body(r, c):
            row_copy(ref, s, 0, r).start()
            row_copy(ref, s, 1, r).start()
            return c
        lax.fori_loop(0, rows, body, 0, unroll=DMA_UNROLL)

    @pl.when(i == 0)
    def _():
        issue(pos_ref, slot)

    @pl.when(i + 1 < pl.num_programs(0))
    def _():
        issue(nxt_ref, 1 - slot)

    def wait(r, c):
        row_copy(pos_ref, slot, 0, r).wait()
        row_copy(pos_ref, slot, 1, r).wait()
        return c

    lax.fori_loop(0, rows, wait, 0, unroll=DMA_UNROLL)
    w = w_ref[...]
    ya = buf[slot, 0].reshape(x_ref.shape).astype(F32)
    yb = buf[slot, 1].reshape(x_ref.shape).astype(F32)
    y = w[:, 2:3] * ya + w[:, 3:4] * yb
    out = _layer_norm(alpha * x_ref[...] + y, g_ref[...], b_ref[...])
    xo_ref[...] = out
    xb_ref[...] = out.astype(BF16)


def combine_ln(x, y, pos, route, ln_g, ln_b, *, alpha, rows):
    n, d = x.shape
    n_steps = n // rows
    kern = functools.partial(_combine_ln_kernel, rows=rows, alpha=alpha)
    row = lambda i: (i, 0)
    fixed = lambda i: (0, 0)
    return pl.pallas_call(
        kern,
        grid=(n_steps,),
        in_specs=[*_step_and_next_specs(n_steps, 2 * rows),
                  pl.BlockSpec((rows, LANES), row),
                  pl.BlockSpec((rows, d), row),
                  pl.BlockSpec(memory_space=pl.ANY),
                  pl.BlockSpec((1, d), fixed),
                  pl.BlockSpec((1, d), fixed)],
        out_specs=[pl.BlockSpec((rows, d), row), pl.BlockSpec((rows, d), row)],
        out_shape=[jax.ShapeDtypeStruct((n, d), F32), jax.ShapeDtypeStruct((n, d), BF16)],
        scratch_shapes=[pltpu.VMEM((2, 2, rows, d // LANES, LANES), BF16), pltpu.SemaphoreType.DMA((2, 2))],
        compiler_params=_params(("arbitrary",)),
        name="combine_ln",
    )(pos, pos, route, x, y, ln_g.reshape(1, d), ln_b.reshape(1, d))


def moe_block(xf, xb3, w_router, w_gate, w_up, w_down, layer, ln_g, ln_b, *, alpha, tm, tf, rows):
    n, d = xf.shape
    n_e = w_gate.shape[1]
    w_r = jnp.pad(w_router, ((0, 0), (0, LANES - n_e)))
    route = router(xf, w_r, tm=rows)
    ef = route[:, 0:2].astype(jnp.int32).reshape(-1)
    onehot = (ef[:, None] == jnp.arange(n_e, dtype=jnp.int32)[None, :]).astype(jnp.int32)
    csum = jnp.cumsum(onehot, axis=0)
    rank = jnp.take_along_axis(csum, ef[:, None], axis=1)[:, 0] - 1
    counts = csum[-1]
    padded = ((counts + tm - 1) // tm) * tm
    ends = jnp.cumsum(padded)
    pos = (ends - padded)[ef] + rank
    r_rows = 2 * n + n_e * tm
    src = jnp.zeros((r_rows,), jnp.int32).at[pos].set(jnp.arange(2 * n, dtype=jnp.int32) // 2)
    tile_start = jnp.arange(r_rows // tm, dtype=jnp.int32) * tm
    tile_valid = (tile_start < ends[-1]).astype(jnp.int32)
    te_raw = jnp.minimum(jnp.searchsorted(ends, tile_start, side="right").astype(jnp.int32), n_e - 1)
    last_e = te_raw[jnp.maximum(ends[-1] // tm - 1, 0)]
    tile_expert = jnp.where(tile_valid == 1, te_raw, last_e)

    xs = gather_rows(xb3, src, rows=rows)
    ys = ffn(xs, w_gate, w_up, w_down, layer, tile_expert, tile_valid, tm=tm, tf=tf)
    pos2 = pos.reshape(n // rows, rows, 2).transpose(0, 2, 1).reshape(n // rows, 1, 2 * rows)
    return combine_ln(xf, ys, pos2, route, ln_g, ln_b, alpha=alpha, rows=rows)


def kernel(x, w_in, sb_norm_g, conv_w, gla_w_up, gla_b, gla_norm_g, w_out, ln1_g, ln1_b,
           ffn_w_gate, ffn_w_up, ffn_w_down, moe_w_router, moe_w_gate, moe_w_up, moe_w_down,
           ln2_g, ln2_b):
    batch, seq, d = x.shape
    depth = w_in.shape[0]
    n = batch * seq
    alpha = (2 * depth) ** 0.25

    xf = x.reshape(n, d)
    xb = xf.astype(BF16)
    for layer in range(depth):
        w_r_pad = jnp.pad(w_in[layer, :, OFF_G_R:], ((0, 0), (0, LANES - GLA_RANK))).astype(BF16)
        h, h_r = in_proj(xb, w_in, w_r_pad, layer, tm=1024, tn=1024)
        o_sb = sb_attention(h, sb_norm_g[layer], batch=batch, seq=seq)
        o_cv = gated_conv(h, conv_w[layer], batch=batch, seq=seq)
        w_up_pad = jnp.pad(gla_w_up[layer], ((0, LANES - GLA_RANK), (0, 0))).astype(BF16)
        o_gla = gla(h, h_r, w_up_pad, gla_b[layer], gla_norm_g[layer], batch=batch, seq=seq)
        is_moe = layer % 2 == 1
        xf, xb = outproj_ln(o_sb, o_cv, o_gla, w_out, layer, xf, ln1_g[layer], ln1_b[layer],
                            alpha=alpha, tm=256, tiled=is_moe)
        j = layer // 2
        if is_moe:
            xf, xb = moe_block(xf, xb, moe_w_router[j], moe_w_gate, moe_w_up, moe_w_down, j,
                               ln2_g[layer], ln2_b[layer], alpha=alpha, tm=1024, tf=512, rows=256)
        else:
            ones = jnp.ones((n // 1024,), jnp.int32)
            f = ffn(xb, ffn_w_gate[:, None], ffn_w_up[:, None], ffn_w_down[:, None], j, ones * 0, ones,
                    tm=1024, tf=512)
            xf, xb = residual_ln(xf, f, ln2_g[layer], ln2_b[layer], alpha=alpha, tm=512)
    return xf.reshape(batch, seq, d)
```

```python
import functools

import jax
import jax.numpy as jnp
from jax import lax
from jax.experimental import pallas as pl
from jax.experimental.pallas import tpu as pltpu

F32 = jnp.float32
BF16 = jnp.bfloat16

D_MODEL = 2048
SB_HEADS = 8
SB_DH = 128
SB_WIDTH = SB_HEADS * SB_DH
CONV_WIDTH = 512
CONV_K = 3
GLA_HEADS = 4
GLA_DK = 64
GLA_DV = 128
GLA_KW = GLA_HEADS * GLA_DK
GLA_VW = GLA_HEADS * GLA_DV
GLA_RANK = 16
GLA_TAU = 16.0
GLA_CHUNK = 16
IN_COLS = 3 * SB_WIDTH + 3 * CONV_WIDTH + 2 * GLA_KW + 2 * GLA_VW + GLA_RANK
N_EXPERTS = 8
LN_EPS = 1e-5
RMS_EPS = 1e-6

LANES = 128
VMEM_LIMIT = 56 * 1024 * 1024

OFF_SB_Q, OFF_SB_K, OFF_SB_V = 0, SB_WIDTH, 2 * SB_WIDTH
OFF_CV_B = 3 * SB_WIDTH
OFF_CV_C = OFF_CV_B + CONV_WIDTH
OFF_CV_H = OFF_CV_C + CONV_WIDTH
OFF_G_Q = OFF_CV_H + CONV_WIDTH
OFF_G_K = OFF_G_Q + GLA_KW
OFF_G_V = OFF_G_K + GLA_KW
OFF_G_G = OFF_G_V + GLA_VW
OFF_G_R = OFF_G_G + GLA_VW

SB_EXIT = -90.0


def _params(sem):
    return pltpu.CompilerParams(dimension_semantics=sem, vmem_limit_bytes=VMEM_LIMIT)


def _split_bf16(x):
    hi = x.astype(BF16)
    lo = (x - hi.astype(F32)).astype(BF16)
    return hi, lo


def _dot(a, b):
    return jnp.dot(a, b, preferred_element_type=F32)


def _silu(x):
    return x * (1.0 / (1.0 + jnp.exp(-x)))


def _in_proj_kernel(x_ref, w_ref, wr_ref, h_ref, r_ref):
    x = x_ref[...]
    h_ref[...] = _dot(x, w_ref[...].astype(BF16)).astype(h_ref.dtype)

    @pl.when(pl.program_id(1) == 0)
    def _():
        r_ref[...] = _dot(x, wr_ref[...]).astype(r_ref.dtype)


def in_proj(xb, w_in, w_r_pad, layer, *, tm, tn):
    n, d = xb.shape
    return pl.pallas_call(
        _in_proj_kernel,
        grid=(n // tm, OFF_G_R // tn),
        in_specs=[pl.BlockSpec((tm, d), lambda i, j: (i, 0)),
                  pl.BlockSpec((None, d, tn), lambda i, j: (layer, 0, j)),
                  pl.BlockSpec((d, LANES), lambda i, j: (0, 0))],
        out_specs=[pl.BlockSpec((tm, tn), lambda i, j: (i, j)),
                   pl.BlockSpec((tm, LANES), lambda i, j: (i, 0))],
        out_shape=[jax.ShapeDtypeStruct((n, OFF_G_R), BF16), jax.ShapeDtypeStruct((n, LANES), BF16)],
        compiler_params=_params(("parallel", "arbitrary")),
        name="in_proj",
    )(xb, w_in, w_r_pad)


SB_BLK = 128
SB_WIN = 2 * SB_BLK
SB_GROUP = 8


LOG2E = 1.4426950408889634
SB_EXIT_LOG2 = SB_EXIT * LOG2E


def _minus_later_keys_matrix(n_keys):
    r = lax.broadcasted_iota(jnp.int32, (2 * n_keys, n_keys), 0)
    c = lax.broadcasted_iota(jnp.int32, (2 * n_keys, n_keys), 1)
    return jnp.where((r % n_keys) > c, -1.0, 0.0).astype(BF16)


def _sb_scores(q, k, scale):
    y = lax.dot_general(q, k, (((1,), (1,)), ((), ())), preferred_element_type=F32) * (scale * LOG2E)
    sp = jnp.maximum(y, 0.0) + jnp.log2(1.0 + jnp.exp2(-jnp.abs(y)))
    return y, sp


def _sb_tail(sp_masked, minus_later):
    hi, lo = _split_bf16(sp_masked)
    return _dot(jnp.concatenate([hi, lo], axis=1), minus_later)


def _sb_kernel(q_ref, k_ref, v_ref, g_ref, o_ref, acc_ref, c_ref, *, group, scale):
    minus_later_win = _minus_later_keys_matrix(SB_WIN)
    minus_later_blk = _minus_later_keys_matrix(SB_BLK)
    row = lax.broadcasted_iota(jnp.int32, (SB_BLK, SB_WIN), 0)
    col = lax.broadcasted_iota(jnp.int32, (SB_BLK, SB_WIN), 1)
    causal_prev = col < row + SB_BLK
    gain = g_ref[...]

    def q_rows(qi):
        return pl.ds(pl.multiple_of(qi * SB_BLK, SB_BLK), SB_BLK)

    def windows(qis):
        wss = [pl.multiple_of(jnp.maximum(qi - 1, 0) * SB_BLK, SB_BLK) for qi in qis]
        causal = [col < row + (qis[0] * SB_BLK - wss[0])] + [causal_prev] * (len(qis) - 1)
        scores = [_sb_scores(q_ref[q_rows(qi), :], k_ref[pl.ds(ws, SB_WIN), :], scale) for qi, ws in zip(qis, wss)]
        sp_m = [jnp.where(m, sp, 0.0) for m, (_, sp) in zip(causal, scores)]
        tails = [_sb_tail(s, minus_later_win) for s in sp_m]
        c_max = None
        for u, (m, (y, sp), s, tail, ws) in enumerate(zip(causal, scores, sp_m, tails, wss)):
            a = jnp.where(m, jnp.exp2(y - sp + tail), 0.0)
            acc_ref[u] = _dot(a.astype(BF16), v_ref[pl.ds(ws, SB_WIN), :])
            c = tail[:, 0:1] - s[:, 0:1]
            c_ref[u] = c
            c_max = c if c_max is None else jnp.maximum(c_max, c)
        return c_max

    def earlier_blocks(qi, u):
        q = q_ref[q_rows(qi), :]

        def body(carry):
            j, _ = carry
            start = pl.multiple_of(j * SB_BLK, SB_BLK)
            y, sp = _sb_scores(q, k_ref[pl.ds(start, SB_BLK), :], scale)
            tail = _sb_tail(sp, minus_later_blk)
            c = c_ref[u]
            a = jnp.exp2(y - sp + tail + c)
            acc_ref[u] += _dot(a.astype(BF16), v_ref[pl.ds(start, SB_BLK), :])
            c_new = c + tail[:, 0:1] - sp[:, 0:1]
            c_ref[u] = c_new
            return j - 1, jnp.max(c_new) > SB_EXIT_LOG2

        def cond(carry):
            j, more = carry
            return jnp.logical_and(j >= 0, more)

        lax.while_loop(cond, body, (jnp.maximum(qi - 1, 0) - 1, jnp.max(c_ref[u]) > SB_EXIT_LOG2))

    def run_group(gi, carry):
        c_max = windows([gi * group + u for u in range(group)])

        @pl.when(jnp.max(c_max) > SB_EXIT_LOG2)
        def _():
            for u in range(group):
                earlier_blocks(gi * group + u, u)

        for u in range(group):
            o = acc_ref[u]
            ms = jnp.mean(o * o, axis=-1, keepdims=True)
            rows = pl.ds(pl.multiple_of((gi * group + u) * SB_BLK, SB_BLK), SB_BLK)
            o_ref[rows, :] = (o * lax.rsqrt(ms + RMS_EPS) * gain).astype(o_ref.dtype)
        return carry

    lax.fori_loop(0, q_ref.shape[0] // (SB_BLK * group), run_group, 0)


def sb_attention(h, norm_g, *, batch, seq):
    n = batch * seq
    qb, kb, vb = OFF_SB_Q // SB_DH, OFF_SB_K // SB_DH, OFF_SB_V // SB_DH
    group = min(SB_GROUP, seq // SB_BLK)
    kern = functools.partial(_sb_kernel, group=group, scale=SB_DH ** -0.5)
    return pl.pallas_call(
        kern,
        grid=(batch, SB_HEADS),
        in_specs=[pl.BlockSpec((seq, SB_DH), lambda b, hh: (b, qb + hh)),
                  pl.BlockSpec((seq, SB_DH), lambda b, hh: (b, kb + hh)),
                  pl.BlockSpec((seq, SB_DH), lambda b, hh: (b, vb + hh)),
                  pl.BlockSpec((1, SB_DH), lambda b, hh: (0, 0))],
        out_specs=pl.BlockSpec((seq, SB_DH), lambda b, hh: (b, hh)),
        out_shape=jax.ShapeDtypeStruct((n, SB_WIDTH), BF16),
        scratch_shapes=[pltpu.VMEM((group, SB_BLK, SB_DH), F32), pltpu.VMEM((group, SB_BLK, 1), F32)],
        compiler_params=_params(("parallel", "parallel")),
        name="sb_attention",
    )(h, h, h, norm_g.reshape(1, SB_DH))


def _conv_kernel(b_ref, c_ref, h_ref, w_ref, o_ref):
    u = c_ref[...].astype(F32) * h_ref[...].astype(F32)
    t = lax.broadcasted_iota(jnp.int32, u.shape, 0)
    u1 = jnp.where(t >= 1, pltpu.roll(u, 1, 0), 0.0)
    u2 = jnp.where(t >= 2, pltpu.roll(u, 2, 0), 0.0)
    w = w_ref[...]
    y = w[0:1, :] * u2 + w[1:2, :] * u1 + w[2:3, :] * u
    o_ref[...] = (b_ref[...].astype(F32) * y).astype(o_ref.dtype)


def gated_conv(h, conv_w, *, batch, seq):
    n = batch * seq
    nb = CONV_WIDTH // LANES
    bb, cb, hb = OFF_CV_B // LANES, OFF_CV_C // LANES, OFF_CV_H // LANES
    return pl.pallas_call(
        _conv_kernel,
        grid=(batch, nb),
        in_specs=[pl.BlockSpec((seq, LANES), lambda b, j: (b, bb + j)),
                  pl.BlockSpec((seq, LANES), lambda b, j: (b, cb + j)),
                  pl.BlockSpec((seq, LANES), lambda b, j: (b, hb + j)),
                  pl.BlockSpec((CONV_K, LANES), lambda b, j: (0, j))],
        out_specs=pl.BlockSpec((seq, LANES), lambda b, j: (b, j)),
        out_shape=jax.ShapeDtypeStruct((n, CONV_WIDTH), BF16),
        compiler_params=_params(("parallel", "parallel")),
        name="gated_conv",
    )(h, h, h, conv_w)


GLA_SEG = 128
GLA_SEGS_PER_STEP = 2


def _gla_kernel(q_ref, k_ref, v_ref, g_ref, r_ref, wup_ref, b_ref, gn_ref, o_ref, s_ref, *, n_seg):
    seg_len, chunk = GLA_SEG, GLA_CHUNK
    n_chunk = seg_len // chunk
    two_dk, two_dv = 2 * GLA_DK, 2 * GLA_DV

    row = lax.broadcasted_iota(jnp.int32, (seg_len, seg_len), 0)
    col = lax.broadcasted_iota(jnp.int32, (seg_len, seg_len), 1)
    same = (row // chunk) == (col // chunk)
    cum_m = jnp.logical_and(same, col <= row).astype(BF16)
    tot_m = same.astype(BF16)
    r2 = lax.broadcasted_iota(jnp.int32, (two_dk, two_dv), 0)
    c2 = lax.broadcasted_iota(jnp.int32, (two_dk, two_dv), 1)
    head_bd = (r2 // GLA_DK) == (c2 // GLA_DV)
    ones_bd = head_bd.astype(BF16)
    bd_f = head_bd.astype(F32)
    band = jnp.where(jnp.logical_and(same, col <= row), row - col, -1)
    lane_chunk = col // chunk

    wup = wup_ref[...]
    bias = b_ref[...]
    gn = gn_ref[...]
    s_ref[...] = jnp.zeros_like(s_ref)

    def rms(x):
        return x * lax.rsqrt(jnp.mean(x * x, axis=-1, keepdims=True) + RMS_EPS)

    def gates(rows):
        u = _dot(r_ref[rows, :], wup) + bias
        la = (jnp.minimum(u, 0.0) - jnp.log1p(jnp.exp(-jnp.abs(u)))) * (1.0 / GLA_TAU)
        hi, lo = _split_bf16(la)
        return _dot(cum_m, hi) + _dot(cum_m, lo), _dot(tot_m, hi) + _dot(tot_m, lo)

    def intra(q, k, v, bcum):
        score = _dot((q * k).astype(BF16), ones_bd)
        attn0 = jnp.where(band == 0, score[:, :GLA_DV], 0.0)
        attn1 = jnp.where(band == 0, score[:, GLA_DV:], 0.0)
        for d in range(1, chunk):
            kd = pltpu.roll(k, d, 0)
            bd = pltpu.roll(bcum, d, 0)
            p = q * kd * jnp.exp(jnp.minimum(bcum - bd, 0.0))
            score = _dot(p.astype(BF16), ones_bd)
            attn0 = jnp.where(band == d, score[:, :GLA_DV], attn0)
            attn1 = jnp.where(band == d, score[:, GLA_DV:], attn1)
        return jnp.concatenate([_dot(attn0.astype(BF16), v[:, :GLA_DV]),
                                _dot(attn1.astype(BF16), v[:, GLA_DV:])], axis=1)

    def seg_group(gi, carry):
        all_rows = [pl.ds(pl.multiple_of((gi * GLA_SEGS_PER_STEP + s) * seg_len, seg_len), seg_len)
                    for s in range(GLA_SEGS_PER_STEP)]
        qs = [q_ref[rows, :].astype(F32) * (GLA_DK ** -0.5) for rows in all_rows]
        ks = [k_ref[rows, :].astype(F32) for rows in all_rows]
        vs = [v_ref[rows, :] for rows in all_rows]
        bcums, blasts = zip(*[gates(rows) for rows in all_rows])
        o_accs = [intra(q, k, v, bcum) for q, k, v, bcum in zip(qs, ks, vs, bcums)]

        q_ins = [(q * jnp.exp(bcum)).astype(BF16) for q, bcum in zip(qs, bcums)]
        k_up_ts = [(k * jnp.exp(blast - bcum)).T for k, bcum, blast in zip(ks, bcums, blasts)]
        g_ts = [jnp.exp(blast.T) for blast in blasts]
        kvs = [[_dot(jnp.where(lane_chunk == c, k_up_t, 0.0).astype(BF16), v) for c in range(n_chunk)]
               for k_up_t, v in zip(k_up_ts, vs)]
        state = s_ref[...]
        states = []
        for g_t, kv in zip(g_ts, kvs):
            seg_states = []
            for c in range(n_chunk):
                seg_states.append(state.astype(BF16))
                state = g_t[:, c * chunk:c * chunk + 1] * state + kv[c] * bd_f
            states.append(seg_states)
        s_ref[...] = state

        for rows, o_acc, q_in, seg_states in zip(all_rows, o_accs, q_ins, states):
            o_inter = [_dot(q_in[c * chunk:(c + 1) * chunk, :], seg_states[c]) for c in range(n_chunk)]
            o_acc = o_acc + jnp.concatenate(o_inter, axis=0)
            gate = g_ref[rows, :].astype(F32)
            o0 = rms(o_acc[:, :GLA_DV]) * gn
            o1 = rms(o_acc[:, GLA_DV:]) * gn
            out = jnp.concatenate([o0, o1], axis=1) * _silu(gate)
            o_ref[rows, :] = out.astype(o_ref.dtype)
        return carry

    lax.fori_loop(0, n_seg // GLA_SEGS_PER_STEP, seg_group, 0)


def gla(h, h_r, w_up_pad, gla_b, gla_norm_g, *, batch, seq):
    n = batch * seq
    pairs = GLA_HEADS // 2
    qb, kb = OFF_G_Q // LANES, OFF_G_K // LANES
    vb, gb = OFF_G_V // (2 * GLA_DV), OFF_G_G // (2 * GLA_DV)
    kern = functools.partial(_gla_kernel, n_seg=seq // GLA_SEG)
    return pl.pallas_call(
        kern,
        grid=(batch, pairs),
        in_specs=[pl.BlockSpec((seq, LANES), lambda b, p: (b, qb + p)),
                  pl.BlockSpec((seq, LANES), lambda b, p: (b, kb + p)),
                  pl.BlockSpec((seq, 2 * GLA_DV), lambda b, p: (b, vb + p)),
                  pl.BlockSpec((seq, 2 * GLA_DV), lambda b, p: (b, gb + p)),
                  pl.BlockSpec((seq, LANES), lambda b, p: (b, 0)),
                  pl.BlockSpec((LANES, LANES), lambda b, p: (0, p)),
                  pl.BlockSpec((1, LANES), lambda b, p: (0, p)),
                  pl.BlockSpec((1, GLA_DV), lambda b, p: (0, 0))],
        out_specs=pl.BlockSpec((seq, 2 * GLA_DV), lambda b, p: (b, p)),
        out_shape=jax.ShapeDtypeStruct((n, GLA_VW), BF16),
        scratch_shapes=[pltpu.VMEM((2 * GLA_DK, 2 * GLA_DV), F32)],
        compiler_params=_params(("parallel", "parallel")),
        name="gla",
    )(h, h, h, h, h_r, w_up_pad, gla_b.reshape(1, GLA_KW), gla_norm_g.reshape(1, GLA_DV))


def _layer_norm(y, g, b):
    mu = jnp.mean(y, axis=-1, keepdims=True)
    yc = y - mu
    var = jnp.mean(yc * yc, axis=-1, keepdims=True)
    return yc * lax.rsqrt(var + LN_EPS) * g + b


def _top2_route(logits):
    col = lax.broadcasted_iota(jnp.int32, logits.shape, 1)
    neg = jnp.float32(-jnp.inf)
    l1 = jnp.where(col < N_EXPERTS, logits, neg)
    m1 = jnp.max(l1, axis=-1, keepdims=True)
    i1 = jnp.min(jnp.where(l1 == m1, col, LANES), axis=-1, keepdims=True)
    l2 = jnp.where(col == i1, neg, l1)
    m2 = jnp.max(l2, axis=-1, keepdims=True)
    i2 = jnp.min(jnp.where(l2 == m2, col, LANES), axis=-1, keepdims=True)
    e2 = jnp.exp(m2 - m1)
    w1 = 1.0 / (1.0 + e2)
    w2 = e2 / (1.0 + e2)
    out = jnp.where(col == 0, i1.astype(F32), 0.0) + jnp.where(col == 1, i2.astype(F32), 0.0)
    return out + jnp.where(col == 2, w1, 0.0) + jnp.where(col == 3, w2, 0.0)


def _outproj_ln_kernel(*refs, alpha, routed):
    if routed:
        sb_ref, cv_ref, gl_ref, w_ref, x_ref, g_ref, b_ref, wr_ref, xo_ref, xb_ref, route_ref, wb_ref = refs
    else:
        sb_ref, cv_ref, gl_ref, w_ref, x_ref, g_ref, b_ref, xo_ref, xb_ref, wb_ref = refs

    @pl.when(pl.program_id(0) == 0)
    def _():
        wb_ref[...] = w_ref[...].astype(BF16)

    mix = _dot(sb_ref[...], wb_ref[0:SB_WIDTH, :])
    mix = mix + _dot(cv_ref[...], wb_ref[SB_WIDTH:SB_WIDTH + CONV_WIDTH, :])
    mix = mix + _dot(gl_ref[...], wb_ref[SB_WIDTH + CONV_WIDTH:, :])
    out = _layer_norm(alpha * x_ref[...] + mix, g_ref[...], b_ref[...])
    xo_ref[...] = out
    xb_ref[...] = out.astype(BF16).reshape(xb_ref.shape)
    if routed:
        logits = jnp.dot(out, wr_ref[...], preferred_element_type=F32, precision=lax.Precision.HIGHEST)
        route_ref[...] = _top2_route(logits)


def _token_major_spec(tm, d, tiled):
    if tiled:
        return pl.BlockSpec((tm, d // LANES, LANES), lambda i: (i, 0, 0))
    return pl.BlockSpec((tm, d), lambda i: (i, 0))


def _token_major_shape(n, d, tiled):
    return (n, d // LANES, LANES) if tiled else (n, d)


def outproj_ln(o_sb, o_cv, o_gla, w_out, layer, x, ln_g, ln_b, w_router, *, alpha, tm):
    n, d = x.shape
    routed = w_router is not None
    kern = functools.partial(_outproj_ln_kernel, alpha=alpha, routed=routed)
    row = lambda i: (i, 0)
    fixed = lambda i: (0, 0)
    in_specs = [pl.BlockSpec((tm, SB_WIDTH), row),
                pl.BlockSpec((tm, CONV_WIDTH), row),
                pl.BlockSpec((tm, GLA_VW), row),
                pl.BlockSpec((None, d, d), lambda i: (layer, 0, 0), pipeline_mode=pl.Buffered(1)),
                pl.BlockSpec((tm, d), row),
                pl.BlockSpec((1, d), fixed),
                pl.BlockSpec((1, d), fixed)]
    args = [o_sb, o_cv, o_gla, w_out, x, ln_g.reshape(1, d), ln_b.reshape(1, d)]
    out_specs = [pl.BlockSpec((tm, d), row), _token_major_spec(tm, d, routed)]
    out_shape = [jax.ShapeDtypeStruct((n, d), F32), jax.ShapeDtypeStruct(_token_major_shape(n, d, routed), BF16)]
    if routed:
        in_specs.append(pl.BlockSpec((d, LANES), fixed))
        args.append(jnp.pad(w_router, ((0, 0), (0, LANES - w_router.shape[1]))))
        out_specs.append(pl.BlockSpec((tm, LANES), row))
        out_shape.append(jax.ShapeDtypeStruct((n, LANES), F32))
    return pl.pallas_call(
        kern,
        grid=(n // tm,),
        in_specs=in_specs,
        out_specs=out_specs,
        out_shape=out_shape,
        scratch_shapes=[pltpu.VMEM((d, d), BF16)],
        compiler_params=_params(("arbitrary",)),
        name="outproj_ln",
    )(*args)


def _residual_ln_kernel(x_ref, f_ref, g_ref, b_ref, xo_ref, xb_ref, *, alpha):
    f = f_ref[...].reshape(x_ref.shape).astype(F32)
    out = _layer_norm(alpha * x_ref[...] + f, g_ref[...], b_ref[...])
    xo_ref[...] = out
    xb_ref[...] = out.astype(BF16)


def residual_ln(x, f, ln_g, ln_b, *, alpha, tm):
    n, d = x.shape
    kern = functools.partial(_residual_ln_kernel, alpha=alpha)
    row = lambda i: (i, 0)
    fixed = lambda i: (0, 0)
    return pl.pallas_call(
        kern,
        grid=(n // tm,),
        in_specs=[pl.BlockSpec((tm, d), row), _token_major_spec(tm, d, True),
                  pl.BlockSpec((1, d), fixed), pl.BlockSpec((1, d), fixed)],
        out_specs=[pl.BlockSpec((tm, d), row), pl.BlockSpec((tm, d), row)],
        out_shape=[jax.ShapeDtypeStruct((n, d), F32), jax.ShapeDtypeStruct((n, d), BF16)],
        compiler_params=_params(("parallel",)),
        name="residual_ln",
    )(x, f, ln_g.reshape(1, d), ln_b.reshape(1, d))


FFN_TM = 1024
FFN_TF = 256
FFN_SUB = 256


def _ffn_kernel(te_ref, rows_ref, x_ref, wg_ref, wu_ref, wd_ref, o_ref, acc_ref):
    i, f = pl.program_id(0), pl.program_id(1)
    tm = x_ref.shape[0]

    @pl.when(f == 0)
    def _():
        acc_ref[...] = jnp.zeros_like(acc_ref)

    for m in range(FFN_SUB, tm + 1, FFN_SUB):
        @pl.when(rows_ref[i] == m)
        def _(m=m):
            x = x_ref[0:m, :]
            hg = _dot(x, wg_ref[...].astype(BF16))
            hu = _dot(x, wu_ref[...].astype(BF16))
            hidden = (_silu(hg) * hu).astype(BF16)
            acc_ref[0:m, :] += _dot(hidden, wd_ref[...].astype(BF16))

    @pl.when(f == pl.num_programs(1) - 1)
    def _():
        o_ref[...] = acc_ref[...].astype(BF16).reshape(o_ref.shape)


def ffn(xb, w_gate, w_up, w_down, layer, tile_expert, tile_rows, *, tm, tf):
    r, d = xb.shape
    d_ff = w_gate.shape[-1]
    n_f = d_ff // tf
    per_expert = w_gate.ndim == 4

    def chunk(i, f, rows):
        return jnp.where(rows[i] > 0, f, n_f - 1)

    def w_col(i, f, te, rows):
        return (layer, te[i], 0, chunk(i, f, rows)) if per_expert else (layer, 0, chunk(i, f, rows))

    def w_row(i, f, te, rows):
        return (layer, te[i], chunk(i, f, rows), 0) if per_expert else (layer, chunk(i, f, rows), 0)

    lead = (None, None) if per_expert else (None,)
    grid_spec = pltpu.PrefetchScalarGridSpec(
        num_scalar_prefetch=2,
        grid=(r // tm, n_f),
        in_specs=[pl.BlockSpec((tm, d), lambda i, f, te, rows: (i, 0)),
                  pl.BlockSpec((*lead, d, tf), w_col),
                  pl.BlockSpec((*lead, d, tf), w_col),
                  pl.BlockSpec((*lead, tf, d), w_row)],
        out_specs=pl.BlockSpec((tm, d // LANES, LANES), lambda i, f, te, rows: (i, 0, 0)),
        scratch_shapes=[pltpu.VMEM((tm, d), F32)],
    )
    return pl.pallas_call(
        _ffn_kernel,
        grid_spec=grid_spec,
        out_shape=jax.ShapeDtypeStruct((r, d // LANES, LANES), BF16),
        compiler_params=_params(("parallel", "arbitrary")),
        name="ffn",
    )(tile_expert, tile_rows, xb, w_gate, w_up, w_down)


DMA_UNROLL = 8


def _gather_kernel(idx_ref, nxt_ref, x_hbm, o_ref, buf, sem, *, rows):
    i = pl.program_id(0)
    slot = i % 2

    def row_copy(ref, s, r):
        return pltpu.make_async_copy(x_hbm.at[ref[0, 0, r]], buf.at[s, r], sem.at[s])

    def issue(ref, s):
        def body(r, c):
            row_copy(ref, s, r).start()
            return c
        lax.fori_loop(0, rows, body, 0, unroll=DMA_UNROLL)

    @pl.when(i == 0)
    def _():
        issue(idx_ref, slot)

    @pl.when(i + 1 < pl.num_programs(0))
    def _():
        issue(nxt_ref, 1 - slot)

    def wait(r, c):
        row_copy(idx_ref, slot, r).wait()
        return c

    lax.fori_loop(0, rows, wait, 0, unroll=DMA_UNROLL)
    o_ref[...] = buf[slot].reshape(o_ref.shape)


def _step_and_next_specs(n_steps, width):
    cur = pl.BlockSpec((1, 1, width), lambda i: (i, 0, 0), memory_space=pltpu.SMEM)
    nxt = pl.BlockSpec((1, 1, width), lambda i: (jnp.minimum(i + 1, n_steps - 1), 0, 0), memory_space=pltpu.SMEM)
    return cur, nxt


def gather_rows(x3, src, *, rows):
    _, s, l = x3.shape
    r = src.shape[0]
    n_steps = r // rows
    kern = functools.partial(_gather_kernel, rows=rows)
    idx = src.reshape(n_steps, 1, rows)
    return pl.pallas_call(
        kern,
        grid=(n_steps,),
        in_specs=[*_step_and_next_specs(n_steps, rows), pl.BlockSpec(memory_space=pl.ANY)],
        out_specs=pl.BlockSpec((rows, s * l), lambda i: (i, 0)),
        out_shape=jax.ShapeDtypeStruct((r, s * l), BF16),
        scratch_shapes=[pltpu.VMEM((2, rows, s, l), BF16), pltpu.SemaphoreType.DMA((2,))],
        compiler_params=_params(("arbitrary",)),
        name="gather_rows",
    )(idx, idx, x3)


def _combine_ln_kernel(pos_ref, nxt_ref, w_ref, x_ref, y_hbm, g_ref, b_ref, xo_ref, xb_ref,
                       buf, sem, *, rows, alpha):
    i = pl.program_id(0)
    slot = i % 2

    def row_copy(ref, s, k, r):
        return pltpu.make_async_copy(y_hbm.at[ref[0, 0, k * rows + r]], buf.at[s, k, r], sem.at[s, k])

    def issue(ref, s):
        def body(r, c):
            row_copy(ref, s, 0, r).start()
            row_copy(ref, s, 1, r).start()
            return c
        lax.fori_loop(0, rows, body, 0, unroll=DMA_UNROLL)

    @pl.when(i == 0)
    def _():
        issue(pos_ref, slot)

    @pl.when(i + 1 < pl.num_programs(0))
    def _():
        issue(nxt_ref, 1 - slot)

    def wait(r, c):
        row_copy(pos_ref, slot, 0, r).wait()
        row_copy(pos_ref, slot, 1, r).wait()
        return c

    lax.fori_loop(0, rows, wait, 0, unroll=DMA_UNROLL)
    w = w_ref[...]
    ya = buf[slot, 0].reshape(x_ref.shape).astype(F32)
    yb = buf[slot, 1].reshape(x_ref.shape).astype(F32)
    y = w[:, 2:3] * ya + w[:, 3:4] * yb
    out = _layer_norm(alpha * x_ref[...] + y, g_ref[...], b_ref[...])
    xo_ref[...] = out
    xb_ref[...] = out.astype(BF16)


def combine_ln(x, y, pos, route, ln_g, ln_b, *, alpha, rows):
    n, d = x.shape
    n_steps = n // rows
    kern = functools.partial(_combine_ln_kernel, rows=rows, alpha=alpha)
    row = lambda i: (i, 0)
    fixed = lambda i: (0, 0)
    return pl.pallas_call(
        kern,
        grid=(n_steps,),
        in_specs=[*_step_and_next_specs(n_steps, 2 * rows),
                  pl.BlockSpec((rows, LANES), row),
                  pl.BlockSpec((rows, d), row),
                  pl.BlockSpec(memory_space=pl.ANY),
                  pl.BlockSpec((1, d), fixed),
                  pl.BlockSpec((1, d), fixed)],
        out_specs=[pl.BlockSpec((rows, d), row), pl.BlockSpec((rows, d), row)],
        out_shape=[jax.ShapeDtypeStruct((n, d), F32), jax.ShapeDtypeStruct((n, d), BF16)],
        scratch_shapes=[pltpu.VMEM((2, 2, rows, d // LANES, LANES), BF16), pltpu.SemaphoreType.DMA((2, 2))],
        compiler_params=_params(("arbitrary",)),
        name="combine_ln",
    )(pos, pos, route, x, y, ln_g.reshape(1, d), ln_b.reshape(1, d))


def moe_block(xf, xb3, route, w_gate, w_up, w_down, layer, ln_g, ln_b, *, alpha, tm, tf, rows):
    n, d = xf.shape
    n_e = w_gate.shape[1]
    ef = route[:, 0:2].astype(jnp.int32).reshape(-1)
    onehot = (ef[:, None] == jnp.arange(n_e, dtype=jnp.int32)[None, :]).astype(jnp.int32)
    csum = jnp.cumsum(onehot, axis=0)
    rank = jnp.take_along_axis(csum, ef[:, None], axis=1)[:, 0] - 1
    counts = csum[-1]
    padded = ((counts + tm - 1) // tm) * tm
    ends = jnp.cumsum(padded)
    pos = (ends - padded)[ef] + rank
    r_rows = 2 * n + n_e * tm
    src = jnp.zeros((r_rows,), jnp.int32).at[pos].set(jnp.arange(2 * n, dtype=jnp.int32) // 2)
    tile_start = jnp.arange(r_rows // tm, dtype=jnp.int32) * tm
    te_raw = jnp.minimum(jnp.searchsorted(ends, tile_start, side="right").astype(jnp.int32), n_e - 1)
    last_e = te_raw[jnp.maximum(ends[-1] // tm - 1, 0)]
    tile_expert = jnp.where(tile_start < ends[-1], te_raw, last_e)
    counts_sub = ((counts + FFN_SUB - 1) // FFN_SUB) * FFN_SUB
    tile_rows = jnp.clip(((ends - padded) + counts_sub)[te_raw] - tile_start, 0, tm)

    xs = gather_rows(xb3, src, rows=rows)
    ys = ffn(xs, w_gate, w_up, w_down, layer, tile_expert, tile_rows, tm=tm, tf=tf)
    pos2 = pos.reshape(n // rows, rows, 2).transpose(0, 2, 1).reshape(n // rows, 1, 2 * rows)
    return combine_ln(xf, ys, pos2, route, ln_g, ln_b, alpha=alpha, rows=rows)


def kernel(x, w_in, sb_norm_g, conv_w, gla_w_up, gla_b, gla_norm_g, w_out, ln1_g, ln1_b,
           ffn_w_gate, ffn_w_up, ffn_w_down, moe_w_router, moe_w_gate, moe_w_up, moe_w_down,
           ln2_g, ln2_b):
    batch, seq, d = x.shape
    depth = w_in.shape[0]
    n = batch * seq
    alpha = (2 * depth) ** 0.25

    xf = x.reshape(n, d)
    xb = xf.astype(BF16)
    for layer in range(depth):
        w_r_pad = jnp.pad(w_in[layer, :, OFF_G_R:], ((0, 0), (0, LANES - GLA_RANK))).astype(BF16)
        h, h_r = in_proj(xb, w_in, w_r_pad, layer, tm=1024, tn=1024)
        o_sb = sb_attention(h, sb_norm_g[layer], batch=batch, seq=seq)
        o_cv = gated_conv(h, conv_w[layer], batch=batch, seq=seq)
        w_up_pad = jnp.pad(gla_w_up[layer], ((0, LANES - GLA_RANK), (0, 0))).astype(BF16)
        o_gla = gla(h, h_r, w_up_pad, gla_b[layer], gla_norm_g[layer], batch=batch, seq=seq)
        j = layer // 2
        if layer % 2 == 1:
            xf, xb, route = outproj_ln(o_sb, o_cv, o_gla, w_out, layer, xf, ln1_g[layer], ln1_b[layer],
                                       moe_w_router[j], alpha=alpha, tm=256)
            xf, xb = moe_block(xf, xb, route, moe_w_gate, moe_w_up, moe_w_down, j,
                               ln2_g[layer], ln2_b[layer], alpha=alpha, tm=FFN_TM, tf=FFN_TF, rows=256)
        else:
            xf, xb = outproj_ln(o_sb, o_cv, o_gla, w_out, layer, xf, ln1_g[layer], ln1_b[layer], None,
                                alpha=alpha, tm=256)
            n_tiles = n // FFN_TM
            f = ffn(xb, ffn_w_gate, ffn_w_up, ffn_w_down, j, jnp.zeros((n_tiles,), jnp.int32),
                    jnp.full((n_tiles,), FFN_TM, jnp.int32), tm=FFN_TM, tf=FFN_TF)
            xf, xb = residual_ln(xf, f, ln2_g[layer], ln2_b[layer], alpha=alpha, tm=512)
    return xf.reshape(batch, seq, d)
```

```python
import functools

import jax
import jax.numpy as jnp
from jax import lax
from jax.experimental import pallas as pl
from jax.experimental.pallas import tpu as pltpu

F32 = jnp.float32
BF16 = jnp.bfloat16

D_MODEL = 2048
SB_HEADS = 8
SB_DH = 128
SB_WIDTH = SB_HEADS * SB_DH
CONV_WIDTH = 512
CONV_K = 3
GLA_HEADS = 4
GLA_DK = 64
GLA_DV = 128
GLA_KW = GLA_HEADS * GLA_DK
GLA_VW = GLA_HEADS * GLA_DV
GLA_RANK = 16
GLA_TAU = 16.0
GLA_CHUNK = 16
IN_COLS = 3 * SB_WIDTH + 3 * CONV_WIDTH + 2 * GLA_KW + 2 * GLA_VW + GLA_RANK
N_EXPERTS = 8
LN_EPS = 1e-5
RMS_EPS = 1e-6

LANES = 128
VMEM_LIMIT = 56 * 1024 * 1024

OFF_SB_Q, OFF_SB_K, OFF_SB_V = 0, SB_WIDTH, 2 * SB_WIDTH
OFF_CV_B = 3 * SB_WIDTH
OFF_CV_C = OFF_CV_B + CONV_WIDTH
OFF_CV_H = OFF_CV_C + CONV_WIDTH
OFF_G_Q = OFF_CV_H + CONV_WIDTH
OFF_G_K = OFF_G_Q + GLA_KW
OFF_G_V = OFF_G_K + GLA_KW
OFF_G_G = OFF_G_V + GLA_VW
OFF_G_R = OFF_G_G + GLA_VW

SB_EXIT = -90.0


def _params(sem):
    return pltpu.CompilerParams(dimension_semantics=sem, vmem_limit_bytes=VMEM_LIMIT)


def _split_bf16(x):
    hi = x.astype(BF16)
    lo = (x - hi.astype(F32)).astype(BF16)
    return hi, lo


def _dot(a, b):
    return jnp.dot(a, b, preferred_element_type=F32)


def _silu(x):
    return x * (1.0 / (1.0 + jnp.exp(-x)))


def _in_proj_kernel(x_ref, w_ref, wr_ref, h_ref, r_ref):
    x = x_ref[...]
    h_ref[...] = _dot(x, w_ref[...].astype(BF16)).astype(h_ref.dtype)

    @pl.when(pl.program_id(1) == 0)
    def _():
        r_ref[...] = _dot(x, wr_ref[...]).astype(r_ref.dtype)


def in_proj(xb, w_in, w_r_pad, layer, *, tm, tn):
    n, d = xb.shape
    return pl.pallas_call(
        _in_proj_kernel,
        grid=(n // tm, OFF_G_R // tn),
        in_specs=[pl.BlockSpec((tm, d), lambda i, j: (i, 0)),
                  pl.BlockSpec((None, d, tn), lambda i, j: (layer, 0, j)),
                  pl.BlockSpec((d, LANES), lambda i, j: (0, 0))],
        out_specs=[pl.BlockSpec((tm, tn), lambda i, j: (i, j)),
                   pl.BlockSpec((tm, LANES), lambda i, j: (i, 0))],
        out_shape=[jax.ShapeDtypeStruct((n, OFF_G_R), BF16), jax.ShapeDtypeStruct((n, LANES), BF16)],
        compiler_params=_params(("parallel", "arbitrary")),
        name="in_proj",
    )(xb, w_in, w_r_pad)


SB_BLK = 128
SB_WIN = 2 * SB_BLK
SB_GROUP = 8


LOG2E = 1.4426950408889634
SB_EXIT_LOG2 = SB_EXIT * LOG2E


def _minus_later_keys_matrix(n_keys):
    r = lax.broadcasted_iota(jnp.int32, (2 * n_keys, n_keys), 0)
    c = lax.broadcasted_iota(jnp.int32, (2 * n_keys, n_keys), 1)
    return jnp.where((r % n_keys) > c, -1.0, 0.0).astype(BF16)


def _sb_scores(q, k, scale):
    y = lax.dot_general(q, k, (((1,), (1,)), ((), ())), preferred_element_type=F32) * (scale * LOG2E)
    sp = jnp.maximum(y, 0.0) + jnp.log2(1.0 + jnp.exp2(-jnp.abs(y)))
    return y, sp


def _sb_tail(sp_masked, minus_later):
    hi, lo = _split_bf16(sp_masked)
    return _dot(jnp.concatenate([hi, lo], axis=1), minus_later)


def _sb_kernel(q_ref, k_ref, v_ref, g_ref, o_ref, acc_ref, c_ref, *, group, scale):
    minus_later_win = _minus_later_keys_matrix(SB_WIN)
    minus_later_blk = _minus_later_keys_matrix(SB_BLK)
    row = lax.broadcasted_iota(jnp.int32, (SB_BLK, SB_WIN), 0)
    col = lax.broadcasted_iota(jnp.int32, (SB_BLK, SB_WIN), 1)
    causal_prev = col < row + SB_BLK
    gain = g_ref[...]

    def q_rows(qi):
        return pl.ds(pl.multiple_of(qi * SB_BLK, SB_BLK), SB_BLK)

    def windows(qis):
        wss = [pl.multiple_of(jnp.maximum(qi - 1, 0) * SB_BLK, SB_BLK) for qi in qis]
        causal = [col < row + (qis[0] * SB_BLK - wss[0])] + [causal_prev] * (len(qis) - 1)
        scores = [_sb_scores(q_ref[q_rows(qi), :], k_ref[pl.ds(ws, SB_WIN), :], scale) for qi, ws in zip(qis, wss)]
        sp_m = [jnp.where(m, sp, 0.0) for m, (_, sp) in zip(causal, scores)]
        tails = [_sb_tail(s, minus_later_win) for s in sp_m]
        c_max = None
        for u, (m, (y, sp), s, tail, ws) in enumerate(zip(causal, scores, sp_m, tails, wss)):
            a = jnp.where(m, jnp.exp2(y - sp + tail), 0.0)
            acc_ref[u] = _dot(a.astype(BF16), v_ref[pl.ds(ws, SB_WIN), :])
            c = tail[:, 0:1] - s[:, 0:1]
            c_ref[u] = c
            c_max = c if c_max is None else jnp.maximum(c_max, c)
        return c_max

    def earlier_blocks(qi, u):
        q = q_ref[q_rows(qi), :]

        def body(carry):
            j, _ = carry
            start = pl.multiple_of(j * SB_BLK, SB_BLK)
            y, sp = _sb_scores(q, k_ref[pl.ds(start, SB_BLK), :], scale)
            tail = _sb_tail(sp, minus_later_blk)
            c = c_ref[u]
            a = jnp.exp2(y - sp + tail + c)
            acc_ref[u] += _dot(a.astype(BF16), v_ref[pl.ds(start, SB_BLK), :])
            c_new = c + tail[:, 0:1] - sp[:, 0:1]
            c_ref[u] = c_new
            return j - 1, jnp.max(c_new) > SB_EXIT_LOG2

        def cond(carry):
            j, more = carry
            return jnp.logical_and(j >= 0, more)

        lax.while_loop(cond, body, (jnp.maximum(qi - 1, 0) - 1, jnp.max(c_ref[u]) > SB_EXIT_LOG2))

    def run_group(gi, carry):
        c_max = windows([gi * group + u for u in range(group)])

        @pl.when(jnp.max(c_max) > SB_EXIT_LOG2)
        def _():
            for u in range(group):
                earlier_blocks(gi * group + u, u)

        for u in range(group):
            o = acc_ref[u]
            ms = jnp.mean(o * o, axis=-1, keepdims=True)
            rows = pl.ds(pl.multiple_of((gi * group + u) * SB_BLK, SB_BLK), SB_BLK)
            o_ref[rows, :] = (o * lax.rsqrt(ms + RMS_EPS) * gain).astype(o_ref.dtype)
        return carry

    lax.fori_loop(0, q_ref.shape[0] // (SB_BLK * group), run_group, 0)


def sb_attention(h, norm_g, *, batch, seq):
    n = batch * seq
    qb, kb, vb = OFF_SB_Q // SB_DH, OFF_SB_K // SB_DH, OFF_SB_V // SB_DH
    group = min(SB_GROUP, seq // SB_BLK)
    kern = functools.partial(_sb_kernel, group=group, scale=SB_DH ** -0.5)
    return pl.pallas_call(
        kern,
        grid=(batch, SB_HEADS),
        in_specs=[pl.BlockSpec((seq, SB_DH), lambda b, hh: (b, qb + hh)),
                  pl.BlockSpec((seq, SB_DH), lambda b, hh: (b, kb + hh)),
                  pl.BlockSpec((seq, SB_DH), lambda b, hh: (b, vb + hh)),
                  pl.BlockSpec((1, SB_DH), lambda b, hh: (0, 0))],
        out_specs=pl.BlockSpec((seq, SB_DH), lambda b, hh: (b, hh)),
        out_shape=jax.ShapeDtypeStruct((n, SB_WIDTH), BF16),
        scratch_shapes=[pltpu.VMEM((group, SB_BLK, SB_DH), F32), pltpu.VMEM((group, SB_BLK, 1), F32)],
        compiler_params=_params(("parallel", "parallel")),
        name="sb_attention",
    )(h, h, h, norm_g.reshape(1, SB_DH))


def _conv_kernel(b_ref, c_ref, h_ref, w_ref, o_ref):
    u = c_ref[...].astype(F32) * h_ref[...].astype(F32)
    t = lax.broadcasted_iota(jnp.int32, u.shape, 0)
    u1 = jnp.where(t >= 1, pltpu.roll(u, 1, 0), 0.0)
    u2 = jnp.where(t >= 2, pltpu.roll(u, 2, 0), 0.0)
    w = w_ref[...]
    y = w[0:1, :] * u2 + w[1:2, :] * u1 + w[2:3, :] * u
    o_ref[...] = (b_ref[...].astype(F32) * y).astype(o_ref.dtype)


def gated_conv(h, conv_w, *, batch, seq):
    n = batch * seq
    nb = CONV_WIDTH // LANES
    bb, cb, hb = OFF_CV_B // LANES, OFF_CV_C // LANES, OFF_CV_H // LANES
    return pl.pallas_call(
        _conv_kernel,
        grid=(batch, nb),
        in_specs=[pl.BlockSpec((seq, LANES), lambda b, j: (b, bb + j)),
                  pl.BlockSpec((seq, LANES), lambda b, j: (b, cb + j)),
                  pl.BlockSpec((seq, LANES), lambda b, j: (b, hb + j)),
                  pl.BlockSpec((CONV_K, LANES), lambda b, j: (0, j))],
        out_specs=pl.BlockSpec((seq, LANES), lambda b, j: (b, j)),
        out_shape=jax.ShapeDtypeStruct((n, CONV_WIDTH), BF16),
        compiler_params=_params(("parallel", "parallel")),
        name="gated_conv",
    )(h, h, h, conv_w)


GLA_SEG = 128
GLA_SEGS_PER_STEP = 2


def _gla_kernel(q_ref, k_ref, v_ref, g_ref, r_ref, wup_ref, b_ref, gn_ref, o_ref, s_ref, *, n_seg):
    seg_len, chunk = GLA_SEG, GLA_CHUNK
    n_chunk = seg_len // chunk
    two_dk, two_dv = 2 * GLA_DK, 2 * GLA_DV

    row = lax.broadcasted_iota(jnp.int32, (seg_len, seg_len), 0)
    col = lax.broadcasted_iota(jnp.int32, (seg_len, seg_len), 1)
    same = (row // chunk) == (col // chunk)
    cum_m = jnp.logical_and(same, col <= row).astype(BF16)
    tot_m = same.astype(BF16)
    r2 = lax.broadcasted_iota(jnp.int32, (two_dk, two_dv), 0)
    c2 = lax.broadcasted_iota(jnp.int32, (two_dk, two_dv), 1)
    head_bd = (r2 // GLA_DK) == (c2 // GLA_DV)
    ones_bd = head_bd.astype(BF16)
    bd_f = head_bd.astype(F32)
    band = jnp.where(jnp.logical_and(same, col <= row), row - col, -1)
    lane_chunk = col // chunk

    wup = wup_ref[...]
    bias = b_ref[...]
    gn = gn_ref[...]
    s_ref[...] = jnp.zeros_like(s_ref)

    def rms(x):
        return x * lax.rsqrt(jnp.mean(x * x, axis=-1, keepdims=True) + RMS_EPS)

    def gates(rows):
        u = _dot(r_ref[rows, :], wup) + bias
        la = (jnp.minimum(u, 0.0) - jnp.log1p(jnp.exp(-jnp.abs(u)))) * (1.0 / GLA_TAU)
        hi, lo = _split_bf16(la)
        return _dot(cum_m, hi) + _dot(cum_m, lo), _dot(tot_m, hi) + _dot(tot_m, lo)

    def intra(q, k, v, bcum):
        score = _dot((q * k).astype(BF16), ones_bd)
        attn0 = jnp.where(band == 0, score[:, :GLA_DV], 0.0)
        attn1 = jnp.where(band == 0, score[:, GLA_DV:], 0.0)
        for d in range(1, chunk):
            kd = pltpu.roll(k, d, 0)
            bd = pltpu.roll(bcum, d, 0)
            p = q * kd * jnp.exp(jnp.minimum(bcum - bd, 0.0))
            score = _dot(p.astype(BF16), ones_bd)
            attn0 = jnp.where(band == d, score[:, :GLA_DV], attn0)
            attn1 = jnp.where(band == d, score[:, GLA_DV:], attn1)
        return jnp.concatenate([_dot(attn0.astype(BF16), v[:, :GLA_DV]),
                                _dot(attn1.astype(BF16), v[:, GLA_DV:])], axis=1)

    def seg_group(gi, carry):
        all_rows = [pl.ds(pl.multiple_of((gi * GLA_SEGS_PER_STEP + s) * seg_len, seg_len), seg_len)
                    for s in range(GLA_SEGS_PER_STEP)]
        qs = [q_ref[rows, :].astype(F32) * (GLA_DK ** -0.5) for rows in all_rows]
        ks = [k_ref[rows, :].astype(F32) for rows in all_rows]
        vs = [v_ref[rows, :] for rows in all_rows]
        bcums, blasts = zip(*[gates(rows) for rows in all_rows])
        o_accs = [intra(q, k, v, bcum) for q, k, v, bcum in zip(qs, ks, vs, bcums)]

        q_ins = [(q * jnp.exp(bcum)).astype(BF16) for q, bcum in zip(qs, bcums)]
        k_up_ts = [(k * jnp.exp(blast - bcum)).T for k, bcum, blast in zip(ks, bcums, blasts)]
        g_ts = [jnp.exp(blast.T) for blast in blasts]
        kvs = [[_dot(jnp.where(lane_chunk == c, k_up_t, 0.0).astype(BF16), v) for c in range(n_chunk)]
               for k_up_t, v in zip(k_up_ts, vs)]
        state = s_ref[...]
        states = []
        for g_t, kv in zip(g_ts, kvs):
            seg_states = []
            for c in range(n_chunk):
                seg_states.append(state.astype(BF16))
                state = g_t[:, c * chunk:c * chunk + 1] * state + kv[c] * bd_f
            states.append(seg_states)
        s_ref[...] = state

        for rows, o_acc, q_in, seg_states in zip(all_rows, o_accs, q_ins, states):
            o_inter = [_dot(q_in[c * chunk:(c + 1) * chunk, :], seg_states[c]) for c in range(n_chunk)]
            o_acc = o_acc + jnp.concatenate(o_inter, axis=0)
            gate = g_ref[rows, :].astype(F32)
            o0 = rms(o_acc[:, :GLA_DV]) * gn
            o1 = rms(o_acc[:, GLA_DV:]) * gn
            out = jnp.concatenate([o0, o1], axis=1) * _silu(gate)
            o_ref[rows, :] = out.astype(o_ref.dtype)
        return carry

    lax.fori_loop(0, n_seg // GLA_SEGS_PER_STEP, seg_group, 0)


def gla(h, h_r, w_up_pad, gla_b, gla_norm_g, *, batch, seq):
    n = batch * seq
    pairs = GLA_HEADS // 2
    qb, kb = OFF_G_Q // LANES, OFF_G_K // LANES
    vb, gb = OFF_G_V // (2 * GLA_DV), OFF_G_G // (2 * GLA_DV)
    kern = functools.partial(_gla_kernel, n_seg=seq // GLA_SEG)
    return pl.pallas_call(
        kern,
        grid=(batch, pairs),
        in_specs=[pl.BlockSpec((seq, LANES), lambda b, p: (b, qb + p)),
                  pl.BlockSpec((seq, LANES), lambda b, p: (b, kb + p)),
                  pl.BlockSpec((seq, 2 * GLA_DV), lambda b, p: (b, vb + p)),
                  pl.BlockSpec((seq, 2 * GLA_DV), lambda b, p: (b, gb + p)),
                  pl.BlockSpec((seq, LANES), lambda b, p: (b, 0)),
                  pl.BlockSpec((LANES, LANES), lambda b, p: (0, p)),
                  pl.BlockSpec((1, LANES), lambda b, p: (0, p)),
                  pl.BlockSpec((1, GLA_DV), lambda b, p: (0, 0))],
        out_specs=pl.BlockSpec((seq, 2 * GLA_DV), lambda b, p: (b, p)),
        out_shape=jax.ShapeDtypeStruct((n, GLA_VW), BF16),
        scratch_shapes=[pltpu.VMEM((2 * GLA_DK, 2 * GLA_DV), F32)],
        compiler_params=_params(("parallel", "parallel")),
        name="gla",
    )(h, h, h, h, h_r, w_up_pad, gla_b.reshape(1, GLA_KW), gla_norm_g.reshape(1, GLA_DV))


def _layer_norm(y, g, b):
    mu = jnp.mean(y, axis=-1, keepdims=True)
    yc = y - mu
    var = jnp.mean(yc * yc, axis=-1, keepdims=True)
    return yc * lax.rsqrt(var + LN_EPS) * g + b


def _top2_route(logits):
    col = lax.broadcasted_iota(jnp.int32, logits.shape, 1)
    neg = jnp.float32(-jnp.inf)
    l1 = jnp.where(col < N_EXPERTS, logits, neg)
    m1 = jnp.max(l1, axis=-1, keepdims=True)
    i1 = jnp.min(jnp.where(l1 == m1, col, LANES), axis=-1, keepdims=True)
    l2 = jnp.where(col == i1, neg, l1)
    m2 = jnp.max(l2, axis=-1, keepdims=True)
    i2 = jnp.min(jnp.where(l2 == m2, col, LANES), axis=-1, keepdims=True)
    e2 = jnp.exp(m2 - m1)
    w1 = 1.0 / (1.0 + e2)
    w2 = e2 / (1.0 + e2)
    out = jnp.where(col == 0, i1.astype(F32), 0.0) + jnp.where(col == 1, i2.astype(F32), 0.0)
    return out + jnp.where(col == 2, w1, 0.0) + jnp.where(col == 3, w2, 0.0)


def _outproj_ln_kernel(*refs, alpha, routed):
    if routed:
        sb_ref, cv_ref, gl_ref, w_ref, x_ref, g_ref, b_ref, wr_ref, xo_ref, xb_ref, route_ref, wb_ref = refs
    else:
        sb_ref, cv_ref, gl_ref, w_ref, x_ref, g_ref, b_ref, xo_ref, xb_ref, wb_ref = refs

    @pl.when(pl.program_id(0) == 0)
    def _():
        wb_ref[...] = w_ref[...].astype(BF16)

    mix = _dot(sb_ref[...], wb_ref[0:SB_WIDTH, :])
    mix = mix + _dot(cv_ref[...], wb_ref[SB_WIDTH:SB_WIDTH + CONV_WIDTH, :])
    mix = mix + _dot(gl_ref[...], wb_ref[SB_WIDTH + CONV_WIDTH:, :])
    out = _layer_norm(alpha * x_ref[...] + mix, g_ref[...], b_ref[...])
    xo_ref[...] = out
    xb_ref[...] = out.astype(BF16).reshape(xb_ref.shape)
    if routed:
        col = lax.broadcasted_iota(jnp.int32, route_ref.shape, 1)
        logits = jnp.zeros(route_ref.shape, F32)
        for e in range(N_EXPERTS):
            logit_e = jnp.sum(out * wr_ref[e:e + 1, :], axis=-1, keepdims=True)
            logits = jnp.where(col == e, logit_e, logits)
        route_ref[...] = _top2_route(logits)


def _token_major_spec(tm, d, tiled):
    if tiled:
        return pl.BlockSpec((tm, d // LANES, LANES), lambda i: (i, 0, 0))
    return pl.BlockSpec((tm, d), lambda i: (i, 0))


def _token_major_shape(n, d, tiled):
    return (n, d // LANES, LANES) if tiled else (n, d)


def outproj_ln(o_sb, o_cv, o_gla, w_out, layer, x, ln_g, ln_b, w_router, *, alpha, tm):
    n, d = x.shape
    routed = w_router is not None
    kern = functools.partial(_outproj_ln_kernel, alpha=alpha, routed=routed)
    row = lambda i: (i, 0)
    fixed = lambda i: (0, 0)
    in_specs = [pl.BlockSpec((tm, SB_WIDTH), row),
                pl.BlockSpec((tm, CONV_WIDTH), row),
                pl.BlockSpec((tm, GLA_VW), row),
                pl.BlockSpec((None, d, d), lambda i: (layer, 0, 0), pipeline_mode=pl.Buffered(1)),
                pl.BlockSpec((tm, d), row),
                pl.BlockSpec((1, d), fixed),
                pl.BlockSpec((1, d), fixed)]
    args = [o_sb, o_cv, o_gla, w_out, x, ln_g.reshape(1, d), ln_b.reshape(1, d)]
    out_specs = [pl.BlockSpec((tm, d), row), _token_major_spec(tm, d, routed)]
    out_shape = [jax.ShapeDtypeStruct((n, d), F32), jax.ShapeDtypeStruct(_token_major_shape(n, d, routed), BF16)]
    if routed:
        in_specs.append(pl.BlockSpec((N_EXPERTS, d), fixed))
        args.append(w_router.T)
        out_specs.append(pl.BlockSpec((tm, LANES), row))
        out_shape.append(jax.ShapeDtypeStruct((n, LANES), F32))
    return pl.pallas_call(
        kern,
        grid=(n // tm,),
        in_specs=in_specs,
        out_specs=out_specs,
        out_shape=out_shape,
        scratch_shapes=[pltpu.VMEM((d, d), BF16)],
        compiler_params=_params(("arbitrary",)),
        name="outproj_ln",
    )(*args)


def _residual_ln_kernel(x_ref, f_ref, g_ref, b_ref, xo_ref, xb_ref, *, alpha):
    f = f_ref[...].reshape(x_ref.shape).astype(F32)
    out = _layer_norm(alpha * x_ref[...] + f, g_ref[...], b_ref[...])
    xo_ref[...] = out
    xb_ref[...] = out.astype(BF16)


def residual_ln(x, f, ln_g, ln_b, *, alpha, tm):
    n, d = x.shape
    kern = functools.partial(_residual_ln_kernel, alpha=alpha)
    row = lambda i: (i, 0)
    fixed = lambda i: (0, 0)
    return pl.pallas_call(
        kern,
        grid=(n // tm,),
        in_specs=[pl.BlockSpec((tm, d), row), _token_major_spec(tm, d, True),
                  pl.BlockSpec((1, d), fixed), pl.BlockSpec((1, d), fixed)],
        out_specs=[pl.BlockSpec((tm, d), row), pl.BlockSpec((tm, d), row)],
        out_shape=[jax.ShapeDtypeStruct((n, d), F32), jax.ShapeDtypeStruct((n, d), BF16)],
        compiler_params=_params(("parallel",)),
        name="residual_ln",
    )(x, f, ln_g.reshape(1, d), ln_b.reshape(1, d))


FFN_TM = 1024
FFN_TM_DENSE = 2048
FFN_TF = 256
FFN_SUB = 256


def _ffn_kernel(te_ref, rows_ref, x_ref, wg_ref, wu_ref, wd_ref, o_ref, acc_ref, *, row_counts):
    i, f = pl.program_id(0), pl.program_id(1)

    @pl.when(f == 0)
    def _():
        acc_ref[...] = jnp.zeros_like(acc_ref)

    for m in row_counts:
        @pl.when(rows_ref[i] == m)
        def _(m=m):
            x = x_ref[0:m, :]
            hg = _dot(x, wg_ref[...].astype(BF16))
            hu = _dot(x, wu_ref[...].astype(BF16))
            hidden = (_silu(hg) * hu).astype(BF16)
            acc_ref[0:m, :] += _dot(hidden, wd_ref[...].astype(BF16))

    @pl.when(f == pl.num_programs(1) - 1)
    def _():
        o_ref[...] = acc_ref[...].astype(BF16).reshape(o_ref.shape)


def ffn(xb, w_gate, w_up, w_down, layer, tile_expert, tile_rows, *, tm, tf, row_counts, single_buffer_rows):
    r, d = xb.shape
    d_ff = w_gate.shape[-1]
    n_f = d_ff // tf
    per_expert = w_gate.ndim == 4

    def chunk(i, f, rows):
        return jnp.where(rows[i] > 0, f, n_f - 1)

    def w_col(i, f, te, rows):
        return (layer, te[i], 0, chunk(i, f, rows)) if per_expert else (layer, 0, chunk(i, f, rows))

    def w_row(i, f, te, rows):
        return (layer, te[i], chunk(i, f, rows), 0) if per_expert else (layer, chunk(i, f, rows), 0)

    lead = (None, None) if per_expert else (None,)
    row_mode = dict(pipeline_mode=pl.Buffered(1)) if single_buffer_rows else {}
    grid_spec = pltpu.PrefetchScalarGridSpec(
        num_scalar_prefetch=2,
        grid=(r // tm, n_f),
        in_specs=[pl.BlockSpec((tm, d), lambda i, f, te, rows: (i, 0), **row_mode),
                  pl.BlockSpec((*lead, d, tf), w_col),
                  pl.BlockSpec((*lead, d, tf), w_col),
                  pl.BlockSpec((*lead, tf, d), w_row)],
        out_specs=pl.BlockSpec((tm, d // LANES, LANES), lambda i, f, te, rows: (i, 0, 0), **row_mode),
        scratch_shapes=[pltpu.VMEM((tm, d), F32)],
    )
    return pl.pallas_call(
        functools.partial(_ffn_kernel, row_counts=row_counts),
        grid_spec=grid_spec,
        out_shape=jax.ShapeDtypeStruct((r, d // LANES, LANES), BF16),
        compiler_params=_params(("parallel", "arbitrary")),
        name="ffn",
    )(tile_expert, tile_rows, xb, w_gate, w_up, w_down)


DMA_UNROLL = 8


def _gather_kernel(idx_ref, nxt_ref, x_hbm, o_ref, buf, sem, *, rows):
    i = pl.program_id(0)
    slot = i % 2

    def row_copy(ref, s, r):
        return pltpu.make_async_copy(x_hbm.at[ref[0, 0, r]], buf.at[s, r], sem.at[s])

    def issue(ref, s):
        def body(r, c):
            row_copy(ref, s, r).start()
            return c
        lax.fori_loop(0, rows, body, 0, unroll=DMA_UNROLL)

    @pl.when(i == 0)
    def _():
        issue(idx_ref, slot)

    @pl.when(i + 1 < pl.num_programs(0))
    def _():
        issue(nxt_ref, 1 - slot)

    def wait(r, c):
        row_copy(idx_ref, slot, r).wait()
        return c

    lax.fori_loop(0, rows, wait, 0, unroll=DMA_UNROLL)
    o_ref[...] = buf[slot].reshape(o_ref.shape)


def _step_and_next_specs(n_steps, width):
    cur = pl.BlockSpec((1, 1, width), lambda i: (i, 0, 0), memory_space=pltpu.SMEM)
    nxt = pl.BlockSpec((1, 1, width), lambda i: (jnp.minimum(i + 1, n_steps - 1), 0, 0), memory_space=pltpu.SMEM)
    return cur, nxt


def gather_rows(x3, src, *, rows):
    _, s, l = x3.shape
    r = src.shape[0]
    n_steps = r // rows
    kern = functools.partial(_gather_kernel, rows=rows)
    idx = src.reshape(n_steps, 1, rows)
    return pl.pallas_call(
        kern,
        grid=(n_steps,),
        in_specs=[*_step_and_next_specs(n_steps, rows), pl.BlockSpec(memory_space=pl.ANY)],
        out_specs=pl.BlockSpec((rows, s * l), lambda i: (i, 0)),
        out_shape=jax.ShapeDtypeStruct((r, s * l), BF16),
        scratch_shapes=[pltpu.VMEM((2, rows, s, l), BF16), pltpu.SemaphoreType.DMA((2,))],
        compiler_params=_params(("arbitrary",)),
        name="gather_rows",
    )(idx, idx, x3)


def _combine_ln_kernel(pos_ref, nxt_ref, w_ref, x_ref, y_hbm, g_ref, b_ref, xo_ref, xb_ref,
                       buf, sem, *, rows, alpha):
    i = pl.program_id(0)
    slot = i % 2

    def row_copy(ref, s, k, r):
        return pltpu.make_async_copy(y_hbm.at[ref[0, 0, k * rows + r]], buf.at[s, k, r], sem.at[s, k])

    def issue(ref, s):
        def body(r, c):
            row_copy(ref, s, 0, r).start()
            row_copy(ref, s, 1, r).start()
            return c
        lax.fori_loop(0, rows, body, 0, unroll=DMA_UNROLL)

    @pl.when(i == 0)
    def _():
        issue(pos_ref, slot)

    @pl.when(i + 1 < pl.num_programs(0))
    def _():
        issue(nxt_ref, 1 - slot)

    def wait(r, c):
        row_copy(pos_ref, slot, 0, r).wait()
        row_copy(pos_ref, slot, 1, r).wait()
        return c

    lax.fori_loop(0, rows, wait, 0, unroll=DMA_UNROLL)
    w = w_ref[...]
    ya = buf[slot, 0].reshape(x_ref.shape).astype(F32)
    yb = buf[slot, 1].reshape(x_ref.shape).astype(F32)
    y = w[:, 2:3] * ya + w[:, 3:4] * yb
    out = _layer_norm(alpha * x_ref[...] + y, g_ref[...], b_ref[...])
    xo_ref[...] = out
    xb_ref[...] = out.astype(BF16)


def combine_ln(x, y, pos, route, ln_g, ln_b, *, alpha, rows):
    n, d = x.shape
    n_steps = n // rows
    kern = functools.partial(_combine_ln_kernel, rows=rows, alpha=alpha)
    row = lambda i: (i, 0)
    fixed = lambda i: (0, 0)
    return pl.pallas_call(
        kern,
        grid=(n_steps,),
        in_specs=[*_step_and_next_specs(n_steps, 2 * rows),
                  pl.BlockSpec((rows, LANES), row),
                  pl.BlockSpec((rows, d), row),
                  pl.BlockSpec(memory_space=pl.ANY),
                  pl.BlockSpec((1, d), fixed),
                  pl.BlockSpec((1, d), fixed)],
        out_specs=[pl.BlockSpec((rows, d), row), pl.BlockSpec((rows, d), row)],
        out_shape=[jax.ShapeDtypeStruct((n, d), F32), jax.ShapeDtypeStruct((n, d), BF16)],
        scratch_shapes=[pltpu.VMEM((2, 2, rows, d // LANES, LANES), BF16), pltpu.SemaphoreType.DMA((2, 2))],
        compiler_params=_params(("arbitrary",)),
        name="combine_ln",
    )(pos, pos, route, x, y, ln_g.reshape(1, d), ln_b.reshape(1, d))


def moe_block(xf, xb3, route, w_gate, w_up, w_down, layer, ln_g, ln_b, *, alpha, tm, tf, rows):
    n, d = xf.shape
    n_e = w_gate.shape[1]
    ef = route[:, 0:2].astype(jnp.int32).reshape(-1)
    onehot = (ef[:, None] == jnp.arange(n_e, dtype=jnp.int32)[None, :]).astype(jnp.int32)
    csum = jnp.cumsum(onehot, axis=0)
    rank = jnp.take_along_axis(csum, ef[:, None], axis=1)[:, 0] - 1
    counts = csum[-1]
    padded = ((counts + tm - 1) // tm) * tm
    ends = jnp.cumsum(padded)
    pos = (ends - padded)[ef] + rank
    r_rows = 2 * n + n_e * tm
    src = jnp.zeros((r_rows,), jnp.int32).at[pos].set(jnp.arange(2 * n, dtype=jnp.int32) // 2)
    tile_start = jnp.arange(r_rows // tm, dtype=jnp.int32) * tm
    te_raw = jnp.minimum(jnp.searchsorted(ends, tile_start, side="right").astype(jnp.int32), n_e - 1)
    last_e = te_raw[jnp.maximum(ends[-1] // tm - 1, 0)]
    tile_expert = jnp.where(tile_start < ends[-1], te_raw, last_e)
    counts_sub = ((counts + FFN_SUB - 1) // FFN_SUB) * FFN_SUB
    tile_rows = jnp.clip(((ends - padded) + counts_sub)[te_raw] - tile_start, 0, tm)

    xs = gather_rows(xb3, src, rows=rows)
    ys = ffn(xs, w_gate, w_up, w_down, layer, tile_expert, tile_rows, tm=tm, tf=tf,
             row_counts=tuple(range(FFN_SUB, tm + 1, FFN_SUB)), single_buffer_rows=False)
    pos2 = pos.reshape(n // rows, rows, 2).transpose(0, 2, 1).reshape(n // rows, 1, 2 * rows)
    return combine_ln(xf, ys, pos2, route, ln_g, ln_b, alpha=alpha, rows=rows)


def kernel(x, w_in, sb_norm_g, conv_w, gla_w_up, gla_b, gla_norm_g, w_out, ln1_g, ln1_b,
           ffn_w_gate, ffn_w_up, ffn_w_down, moe_w_router, moe_w_gate, moe_w_up, moe_w_down,
           ln2_g, ln2_b):
    batch, seq, d = x.shape
    depth = w_in.shape[0]
    n = batch * seq
    alpha = (2 * depth) ** 0.25

    xf = x.reshape(n, d)
    xb = xf.astype(BF16)
    w_r_all = lax.optimization_barrier(w_in[:, :, OFF_G_R:])
    for layer in range(depth):
        w_r_pad = jnp.pad(w_r_all[layer], ((0, 0), (0, LANES - GLA_RANK))).astype(BF16)
        h, h_r = in_proj(xb, w_in, w_r_pad, layer, tm=1024, tn=1024)
        o_sb = sb_attention(h, sb_norm_g[layer], batch=batch, seq=seq)
        o_cv = gated_conv(h, conv_w[layer], batch=batch, seq=seq)
        w_up_pad = jnp.pad(gla_w_up[layer], ((0, LANES - GLA_RANK), (0, 0))).astype(BF16)
        o_gla = gla(h, h_r, w_up_pad, gla_b[layer], gla_norm_g[layer], batch=batch, seq=seq)
        j = layer // 2
        if layer % 2 == 1:
            xf, xb, route = outproj_ln(o_sb, o_cv, o_gla, w_out, layer, xf, ln1_g[layer], ln1_b[layer],
                                       moe_w_router[j], alpha=alpha, tm=256)
            xf, xb = moe_block(xf, xb, route, moe_w_gate, moe_w_up, moe_w_down, j,
                               ln2_g[layer], ln2_b[layer], alpha=alpha, tm=FFN_TM, tf=FFN_TF, rows=256)
        else:
            xf, xb = outproj_ln(o_sb, o_cv, o_gla, w_out, layer, xf, ln1_g[layer], ln1_b[layer], None,
                                alpha=alpha, tm=256)
            n_tiles = n // FFN_TM_DENSE
            f = ffn(xb, ffn_w_gate, ffn_w_up, ffn_w_down, j, jnp.zeros((n_tiles,), jnp.int32),
                    jnp.full((n_tiles,), FFN_TM_DENSE, jnp.int32), tm=FFN_TM_DENSE, tf=FFN_TF,
                    row_counts=(FFN_TM_DENSE,), single_buffer_rows=True)
            xf, xb = residual_ln(xf, f, ln2_g[layer], ln2_b[layer], alpha=alpha, tm=512)
    return xf.reshape(batch, seq, d)
```

```python
import functools

import jax
import jax.numpy as jnp
from jax import lax
from jax.experimental import pallas as pl
from jax.experimental.pallas import tpu as pltpu

F32 = jnp.float32
BF16 = jnp.bfloat16

D_MODEL = 2048
SB_HEADS = 8
SB_DH = 128
SB_WIDTH = SB_HEADS * SB_DH
CONV_WIDTH = 512
CONV_K = 3
GLA_HEADS = 4
GLA_DK = 64
GLA_DV = 128
GLA_KW = GLA_HEADS * GLA_DK
GLA_VW = GLA_HEADS * GLA_DV
GLA_RANK = 16
GLA_TAU = 16.0
GLA_CHUNK = 16
IN_COLS = 3 * SB_WIDTH + 3 * CONV_WIDTH + 2 * GLA_KW + 2 * GLA_VW + GLA_RANK
N_EXPERTS = 8
LN_EPS = 1e-5
RMS_EPS = 1e-6

LANES = 128
VMEM_LIMIT = 56 * 1024 * 1024

OFF_SB_Q, OFF_SB_K, OFF_SB_V = 0, SB_WIDTH, 2 * SB_WIDTH
OFF_CV_B = 3 * SB_WIDTH
OFF_CV_C = OFF_CV_B + CONV_WIDTH
OFF_CV_H = OFF_CV_C + CONV_WIDTH
OFF_G_Q = OFF_CV_H + CONV_WIDTH
OFF_G_K = OFF_G_Q + GLA_KW
OFF_G_V = OFF_G_K + GLA_KW
OFF_G_G = OFF_G_V + GLA_VW
OFF_G_R = OFF_G_G + GLA_VW

SB_EXIT = -90.0


def _params(sem):
    return pltpu.CompilerParams(dimension_semantics=sem, vmem_limit_bytes=VMEM_LIMIT)


def _split_bf16(x):
    hi = x.astype(BF16)
    lo = (x - hi.astype(F32)).astype(BF16)
    return hi, lo


def _dot(a, b):
    return jnp.dot(a, b, preferred_element_type=F32)


def _silu(x):
    return x * (1.0 / (1.0 + jnp.exp(-x)))


def _in_proj_kernel(x_ref, w_ref, wr_ref, h_ref, r_ref):
    x = x_ref[...]
    h_ref[...] = _dot(x, w_ref[...].astype(BF16)).astype(h_ref.dtype)

    @pl.when(pl.program_id(1) == 0)
    def _():
        r_ref[...] = _dot(x, wr_ref[...]).astype(r_ref.dtype)


def in_proj(xb, w_in, w_r_pad, layer, *, tm, tn):
    n, d = xb.shape
    return pl.pallas_call(
        _in_proj_kernel,
        grid=(n // tm, OFF_G_R // tn),
        in_specs=[pl.BlockSpec((tm, d), lambda i, j: (i, 0)),
                  pl.BlockSpec((None, d, tn), lambda i, j: (layer, 0, j)),
                  pl.BlockSpec((d, LANES), lambda i, j: (0, 0))],
        out_specs=[pl.BlockSpec((tm, tn), lambda i, j: (i, j)),
                   pl.BlockSpec((tm, LANES), lambda i, j: (i, 0))],
        out_shape=[jax.ShapeDtypeStruct((n, OFF_G_R), BF16), jax.ShapeDtypeStruct((n, LANES), BF16)],
        compiler_params=_params(("parallel", "arbitrary")),
        name="in_proj",
    )(xb, w_in, w_r_pad)


SB_BLK = 128
SB_WIN = 2 * SB_BLK
SB_GROUP = 8


LOG2E = 1.4426950408889634
SB_EXIT_LOG2 = SB_EXIT * LOG2E


def _minus_later_keys_matrix(n_keys):
    r = lax.broadcasted_iota(jnp.int32, (2 * n_keys, n_keys), 0)
    c = lax.broadcasted_iota(jnp.int32, (2 * n_keys, n_keys), 1)
    return jnp.where((r % n_keys) > c, -1.0, 0.0).astype(BF16)


def _sb_scores(q, k, scale):
    y = lax.dot_general(q, k, (((1,), (1,)), ((), ())), preferred_element_type=F32) * (scale * LOG2E)
    sp = jnp.maximum(y, 0.0) + jnp.log2(1.0 + jnp.exp2(-jnp.abs(y)))
    return y, sp


def _sb_tail(sp_masked, minus_later):
    hi, lo = _split_bf16(sp_masked)
    return _dot(jnp.concatenate([hi, lo], axis=1), minus_later)


def _sb_kernel(q_ref, k_ref, v_ref, g_ref, o_ref, acc_ref, c_ref, *, group, scale):
    minus_later_win = _minus_later_keys_matrix(SB_WIN)
    minus_later_blk = _minus_later_keys_matrix(SB_BLK)
    row = lax.broadcasted_iota(jnp.int32, (SB_BLK, SB_WIN), 0)
    col = lax.broadcasted_iota(jnp.int32, (SB_BLK, SB_WIN), 1)
    causal_prev = col < row + SB_BLK
    gain = g_ref[...]

    def q_rows(qi):
        return pl.ds(pl.multiple_of(qi * SB_BLK, SB_BLK), SB_BLK)

    def windows(qis):
        wss = [pl.multiple_of(jnp.maximum(qi - 1, 0) * SB_BLK, SB_BLK) for qi in qis]
        causal = [col < row + (qis[0] * SB_BLK - wss[0])] + [causal_prev] * (len(qis) - 1)
        scores = [_sb_scores(q_ref[q_rows(qi), :], k_ref[pl.ds(ws, SB_WIN), :], scale) for qi, ws in zip(qis, wss)]
        sp_m = [jnp.where(m, sp, 0.0) for m, (_, sp) in zip(causal, scores)]
        tails = [_sb_tail(s, minus_later_win) for s in sp_m]
        c_max = None
        for u, (m, (y, sp), s, tail, ws) in enumerate(zip(causal, scores, sp_m, tails, wss)):
            a = jnp.where(m, jnp.exp2(y - sp + tail), 0.0)
            acc_ref[u] = _dot(a.astype(BF16), v_ref[pl.ds(ws, SB_WIN), :])
            c = tail[:, 0:1] - s[:, 0:1]
            c_ref[u] = c
            c_max = c if c_max is None else jnp.maximum(c_max, c)
        return c_max

    def earlier_blocks(qi, u):
        q = q_ref[q_rows(qi), :]

        def body(carry):
            j, _ = carry
            start = pl.multiple_of(j * SB_BLK, SB_BLK)
            y, sp = _sb_scores(q, k_ref[pl.ds(start, SB_BLK), :], scale)
            tail = _sb_tail(sp, minus_later_blk)
            c = c_ref[u]
            a = jnp.exp2(y - sp + tail + c)
            acc_ref[u] += _dot(a.astype(BF16), v_ref[pl.ds(start, SB_BLK), :])
            c_new = c + tail[:, 0:1] - sp[:, 0:1]
            c_ref[u] = c_new
            return j - 1, jnp.max(c_new) > SB_EXIT_LOG2

        def cond(carry):
            j, more = carry
            return jnp.logical_and(j >= 0, more)

        lax.while_loop(cond, body, (jnp.maximum(qi - 1, 0) - 1, jnp.max(c_ref[u]) > SB_EXIT_LOG2))

    def run_group(gi, carry):
        c_max = windows([gi * group + u for u in range(group)])

        @pl.when(jnp.max(c_max) > SB_EXIT_LOG2)
        def _():
            for u in range(group):
                earlier_blocks(gi * group + u, u)

        for u in range(group):
            o = acc_ref[u]
            ms = jnp.mean(o * o, axis=-1, keepdims=True)
            rows = pl.ds(pl.multiple_of((gi * group + u) * SB_BLK, SB_BLK), SB_BLK)
            o_ref[rows, :] = (o * lax.rsqrt(ms + RMS_EPS) * gain).astype(o_ref.dtype)
        return carry

    lax.fori_loop(0, q_ref.shape[0] // (SB_BLK * group), run_group, 0)


def sb_attention(h, norm_g, *, batch, seq):
    n = batch * seq
    qb, kb, vb = OFF_SB_Q // SB_DH, OFF_SB_K // SB_DH, OFF_SB_V // SB_DH
    group = min(SB_GROUP, seq // SB_BLK)
    kern = functools.partial(_sb_kernel, group=group, scale=SB_DH ** -0.5)
    return pl.pallas_call(
        kern,
        grid=(batch, SB_HEADS),
        in_specs=[pl.BlockSpec((seq, SB_DH), lambda b, hh: (b, qb + hh)),
                  pl.BlockSpec((seq, SB_DH), lambda b, hh: (b, kb + hh)),
                  pl.BlockSpec((seq, SB_DH), lambda b, hh: (b, vb + hh)),
                  pl.BlockSpec((1, SB_DH), lambda b, hh: (0, 0))],
        out_specs=pl.BlockSpec((seq, SB_DH), lambda b, hh: (b, hh)),
        out_shape=jax.ShapeDtypeStruct((n, SB_WIDTH), BF16),
        scratch_shapes=[pltpu.VMEM((group, SB_BLK, SB_DH), F32), pltpu.VMEM((group, SB_BLK, 1), F32)],
        compiler_params=_params(("parallel", "parallel")),
        name="sb_attention",
    )(h, h, h, norm_g.reshape(1, SB_DH))


def _conv_kernel(b_ref, c_ref, h_ref, w_ref, o_ref):
    u = c_ref[...].astype(F32) * h_ref[...].astype(F32)
    t = lax.broadcasted_iota(jnp.int32, u.shape, 0)
    u1 = jnp.where(t >= 1, pltpu.roll(u, 1, 0), 0.0)
    u2 = jnp.where(t >= 2, pltpu.roll(u, 2, 0), 0.0)
    w = w_ref[...]
    y = w[0:1, :] * u2 + w[1:2, :] * u1 + w[2:3, :] * u
    o_ref[...] = (b_ref[...].astype(F32) * y).astype(o_ref.dtype)


def gated_conv(h, conv_w, *, batch, seq):
    n = batch * seq
    nb = CONV_WIDTH // LANES
    bb, cb, hb = OFF_CV_B // LANES, OFF_CV_C // LANES, OFF_CV_H // LANES
    return pl.pallas_call(
        _conv_kernel,
        grid=(batch, nb),
        in_specs=[pl.BlockSpec((seq, LANES), lambda b, j: (b, bb + j)),
                  pl.BlockSpec((seq, LANES), lambda b, j: (b, cb + j)),
                  pl.BlockSpec((seq, LANES), lambda b, j: (b, hb + j)),
                  pl.BlockSpec((CONV_K, LANES), lambda b, j: (0, j))],
        out_specs=pl.BlockSpec((seq, LANES), lambda b, j: (b, j)),
        out_shape=jax.ShapeDtypeStruct((n, CONV_WIDTH), BF16),
        compiler_params=_params(("parallel", "parallel")),
        name="gated_conv",
    )(h, h, h, conv_w)


GLA_SEG = 128
GLA_SEGS_PER_STEP = 2


def _gla_kernel(q_ref, k_ref, v_ref, g_ref, r_ref, wup_ref, b_ref, gn_ref, o_ref, s_ref, *, n_seg):
    seg_len, chunk = GLA_SEG, GLA_CHUNK
    n_chunk = seg_len // chunk
    two_dk, two_dv = 2 * GLA_DK, 2 * GLA_DV

    row = lax.broadcasted_iota(jnp.int32, (seg_len, seg_len), 0)
    col = lax.broadcasted_iota(jnp.int32, (seg_len, seg_len), 1)
    same = (row // chunk) == (col // chunk)
    cum_m = jnp.logical_and(same, col <= row).astype(BF16)
    tot_m = same.astype(BF16)
    r2 = lax.broadcasted_iota(jnp.int32, (two_dk, two_dv), 0)
    c2 = lax.broadcasted_iota(jnp.int32, (two_dk, two_dv), 1)
    head_bd = (r2 // GLA_DK) == (c2 // GLA_DV)
    ones_bd = head_bd.astype(BF16)
    bd_f = head_bd.astype(F32)
    band = jnp.where(jnp.logical_and(same, col <= row), row - col, -1)
    lane_chunk = col // chunk

    wup = wup_ref[...]
    bias = b_ref[...]
    gn = gn_ref[...]
    s_ref[...] = jnp.zeros_like(s_ref)

    def rms(x):
        return x * lax.rsqrt(jnp.mean(x * x, axis=-1, keepdims=True) + RMS_EPS)

    def gates(rows):
        u = _dot(r_ref[rows, :], wup) + bias
        la = (jnp.minimum(u, 0.0) - jnp.log1p(jnp.exp(-jnp.abs(u)))) * (1.0 / GLA_TAU)
        hi, lo = _split_bf16(la)
        return _dot(cum_m, hi) + _dot(cum_m, lo), _dot(tot_m, hi) + _dot(tot_m, lo)

    def intra(q, k, v, bcum):
        score = _dot((q * k).astype(BF16), ones_bd)
        attn0 = jnp.where(band == 0, score[:, :GLA_DV], 0.0)
        attn1 = jnp.where(band == 0, score[:, GLA_DV:], 0.0)
        for d in range(1, chunk):
            kd = pltpu.roll(k, d, 0)
            bd = pltpu.roll(bcum, d, 0)
            p = q * kd * jnp.exp(jnp.minimum(bcum - bd, 0.0))
            score = _dot(p.astype(BF16), ones_bd)
            attn0 = jnp.where(band == d, score[:, :GLA_DV], attn0)
            attn1 = jnp.where(band == d, score[:, GLA_DV:], attn1)
        return jnp.concatenate([_dot(attn0.astype(BF16), v[:, :GLA_DV]),
                                _dot(attn1.astype(BF16), v[:, GLA_DV:])], axis=1)

    def seg_group(gi, carry):
        all_rows = [pl.ds(pl.multiple_of((gi * GLA_SEGS_PER_STEP + s) * seg_len, seg_len), seg_len)
                    for s in range(GLA_SEGS_PER_STEP)]
        qs = [q_ref[rows, :].astype(F32) * (GLA_DK ** -0.5) for rows in all_rows]
        ks = [k_ref[rows, :].astype(F32) for rows in all_rows]
        vs = [v_ref[rows, :] for rows in all_rows]
        bcums, blasts = zip(*[gates(rows) for rows in all_rows])
        o_accs = [intra(q, k, v, bcum) for q, k, v, bcum in zip(qs, ks, vs, bcums)]

        q_ins = [(q * jnp.exp(bcum)).astype(BF16) for q, bcum in zip(qs, bcums)]
        k_up_ts = [(k * jnp.exp(blast - bcum)).T for k, bcum, blast in zip(ks, bcums, blasts)]
        g_ts = [jnp.exp(blast.T) for blast in blasts]
        kvs = [[_dot(jnp.where(lane_chunk == c, k_up_t, 0.0).astype(BF16), v) for c in range(n_chunk)]
               for k_up_t, v in zip(k_up_ts, vs)]
        state = s_ref[...]
        states = []
        for g_t, kv in zip(g_ts, kvs):
            seg_states = []
            for c in range(n_chunk):
                seg_states.append(state.astype(BF16))
                state = g_t[:, c * chunk:c * chunk + 1] * state + kv[c] * bd_f
            states.append(seg_states)
        s_ref[...] = state

        for rows, o_acc, q_in, seg_states in zip(all_rows, o_accs, q_ins, states):
            o_inter = [_dot(q_in[c * chunk:(c + 1) * chunk, :], seg_states[c]) for c in range(n_chunk)]
            o_acc = o_acc + jnp.concatenate(o_inter, axis=0)
            gate = g_ref[rows, :].astype(F32)
            o0 = rms(o_acc[:, :GLA_DV]) * gn
            o1 = rms(o_acc[:, GLA_DV:]) * gn
            out = jnp.concatenate([o0, o1], axis=1) * _silu(gate)
            o_ref[rows, :] = out.astype(o_ref.dtype)
        return carry

    lax.fori_loop(0, n_seg // GLA_SEGS_PER_STEP, seg_group, 0)


def gla(h, h_r, w_up_pad, gla_b, gla_norm_g, *, batch, seq):
    n = batch * seq
    pairs = GLA_HEADS // 2
    qb, kb = OFF_G_Q // LANES, OFF_G_K // LANES
    vb, gb = OFF_G_V // (2 * GLA_DV), OFF_G_G // (2 * GLA_DV)
    kern = functools.partial(_gla_kernel, n_seg=seq // GLA_SEG)
    return pl.pallas_call(
        kern,
        grid=(batch, pairs),
        in_specs=[pl.BlockSpec((seq, LANES), lambda b, p: (b, qb + p)),
                  pl.BlockSpec((seq, LANES), lambda b, p: (b, kb + p)),
                  pl.BlockSpec((seq, 2 * GLA_DV), lambda b, p: (b, vb + p)),
                  pl.BlockSpec((seq, 2 * GLA_DV), lambda b, p: (b, gb + p)),
                  pl.BlockSpec((seq, LANES), lambda b, p: (b, 0)),
                  pl.BlockSpec((LANES, LANES), lambda b, p: (0, p)),
                  pl.BlockSpec((1, LANES), lambda b, p: (0, p)),
                  pl.BlockSpec((1, GLA_DV), lambda b, p: (0, 0))],
        out_specs=pl.BlockSpec((seq, 2 * GLA_DV), lambda b, p: (b, p)),
        out_shape=jax.ShapeDtypeStruct((n, GLA_VW), BF16),
        scratch_shapes=[pltpu.VMEM((2 * GLA_DK, 2 * GLA_DV), F32)],
        compiler_params=_params(("parallel", "parallel")),
        name="gla",
    )(h, h, h, h, h_r, w_up_pad, gla_b.reshape(1, GLA_KW), gla_norm_g.reshape(1, GLA_DV))


def _layer_norm(y, g, b):
    mu = jnp.mean(y, axis=-1, keepdims=True)
    yc = y - mu
    var = jnp.mean(yc * yc, axis=-1, keepdims=True)
    return yc * lax.rsqrt(var + LN_EPS) * g + b


def _top2_route(logits):
    col = lax.broadcasted_iota(jnp.int32, logits.shape, 1)
    neg = jnp.float32(-jnp.inf)
    l1 = jnp.where(col < N_EXPERTS, logits, neg)
    m1 = jnp.max(l1, axis=-1, keepdims=True)
    i1 = jnp.min(jnp.where(l1 == m1, col, LANES), axis=-1, keepdims=True)
    l2 = jnp.where(col == i1, neg, l1)
    m2 = jnp.max(l2, axis=-1, keepdims=True)
    i2 = jnp.min(jnp.where(l2 == m2, col, LANES), axis=-1, keepdims=True)
    e2 = jnp.exp(m2 - m1)
    w1 = 1.0 / (1.0 + e2)
    w2 = e2 / (1.0 + e2)
    out = jnp.where(col == 0, i1.astype(F32), 0.0) + jnp.where(col == 1, i2.astype(F32), 0.0)
    return out + jnp.where(col == 2, w1, 0.0) + jnp.where(col == 3, w2, 0.0)


def _outproj_ln_kernel(*refs, alpha, routed):
    if routed:
        sb_ref, cv_ref, gl_ref, w_ref, x_ref, g_ref, b_ref, wr_ref, xo_ref, xb_ref, route_ref, wb_ref = refs
    else:
        sb_ref, cv_ref, gl_ref, w_ref, x_ref, g_ref, b_ref, xo_ref, xb_ref, wb_ref = refs

    @pl.when(pl.program_id(0) == 0)
    def _():
        wb_ref[...] = w_ref[...].astype(BF16)

    mix = _dot(sb_ref[...], wb_ref[0:SB_WIDTH, :])
    mix = mix + _dot(cv_ref[...], wb_ref[SB_WIDTH:SB_WIDTH + CONV_WIDTH, :])
    mix = mix + _dot(gl_ref[...], wb_ref[SB_WIDTH + CONV_WIDTH:, :])
    out = _layer_norm(alpha * x_ref[...] + mix, g_ref[...], b_ref[...])
    xo_ref[...] = out
    xb_ref[...] = out.astype(BF16).reshape(xb_ref.shape)
    if routed:
        col = lax.broadcasted_iota(jnp.int32, route_ref.shape, 1)
        logits = jnp.zeros(route_ref.shape, F32)
        for e in range(N_EXPERTS):
            logit_e = jnp.sum(out * wr_ref[e:e + 1, :], axis=-1, keepdims=True)
            logits = jnp.where(col == e, logit_e, logits)
        route_ref[...] = _top2_route(logits)


def _token_major_spec(tm, d, tiled):
    if tiled:
        return pl.BlockSpec((tm, d // LANES, LANES), lambda i: (i, 0, 0))
    return pl.BlockSpec((tm, d), lambda i: (i, 0))


def _token_major_shape(n, d, tiled):
    return (n, d // LANES, LANES) if tiled else (n, d)


def outproj_ln(o_sb, o_cv, o_gla, w_out, layer, x, ln_g, ln_b, w_router, *, alpha, tm):
    n, d = x.shape
    routed = w_router is not None
    kern = functools.partial(_outproj_ln_kernel, alpha=alpha, routed=routed)
    row = lambda i: (i, 0)
    fixed = lambda i: (0, 0)
    in_specs = [pl.BlockSpec((tm, SB_WIDTH), row),
                pl.BlockSpec((tm, CONV_WIDTH), row),
                pl.BlockSpec((tm, GLA_VW), row),
                pl.BlockSpec((None, d, d), lambda i: (layer, 0, 0), pipeline_mode=pl.Buffered(1)),
                pl.BlockSpec((tm, d), row),
                pl.BlockSpec((1, d), fixed),
                pl.BlockSpec((1, d), fixed)]
    args = [o_sb, o_cv, o_gla, w_out, x, ln_g.reshape(1, d), ln_b.reshape(1, d)]
    out_specs = [pl.BlockSpec((tm, d), row), _token_major_spec(tm, d, routed)]
    out_shape = [jax.ShapeDtypeStruct((n, d), F32), jax.ShapeDtypeStruct(_token_major_shape(n, d, routed), BF16)]
    if routed:
        in_specs.append(pl.BlockSpec((N_EXPERTS, d), fixed))
        args.append(w_router.T)
        out_specs.append(pl.BlockSpec((tm, LANES), row))
        out_shape.append(jax.ShapeDtypeStruct((n, LANES), F32))
    return pl.pallas_call(
        kern,
        grid=(n // tm,),
        in_specs=in_specs,
        out_specs=out_specs,
        out_shape=out_shape,
        scratch_shapes=[pltpu.VMEM((d, d), BF16)],
        compiler_params=_params(("arbitrary",)),
        name="outproj_ln",
    )(*args)


def _residual_ln_kernel(x_ref, f_ref, g_ref, b_ref, xo_ref, xb_ref, *, alpha):
    f = f_ref[...].reshape(x_ref.shape).astype(F32)
    out = _layer_norm(alpha * x_ref[...] + f, g_ref[...], b_ref[...])
    xo_ref[...] = out
    xb_ref[...] = out.astype(BF16)


def residual_ln(x, f, ln_g, ln_b, *, alpha, tm):
    n, d = x.shape
    kern = functools.partial(_residual_ln_kernel, alpha=alpha)
    row = lambda i: (i, 0)
    fixed = lambda i: (0, 0)
    return pl.pallas_call(
        kern,
        grid=(n // tm,),
        in_specs=[pl.BlockSpec((tm, d), row), _token_major_spec(tm, d, True),
                  pl.BlockSpec((1, d), fixed), pl.BlockSpec((1, d), fixed)],
        out_specs=[pl.BlockSpec((tm, d), row), pl.BlockSpec((tm, d), row)],
        out_shape=[jax.ShapeDtypeStruct((n, d), F32), jax.ShapeDtypeStruct((n, d), BF16)],
        compiler_params=_params(("parallel",)),
        name="residual_ln",
    )(x, f, ln_g.reshape(1, d), ln_b.reshape(1, d))


FFN_TM = 1024
FFN_TF = 256
FFN_SUB = 256


def _swiglu_chunk(x, wg_ref, wu_ref, wd_ref):
    hg = _dot(x, wg_ref[...].astype(BF16))
    hu = _dot(x, wu_ref[...].astype(BF16))
    hidden = (_silu(hg) * hu).astype(BF16)
    return _dot(hidden, wd_ref[...].astype(BF16))


def _ffn_kernel(te_ref, rows_ref, x_ref, wg_ref, wu_ref, wd_ref, o_ref, acc_ref, *, row_counts):
    i, f = pl.program_id(0), pl.program_id(1)

    @pl.when(f == 0)
    def _():
        acc_ref[...] = jnp.zeros_like(acc_ref)

    for m in row_counts:
        @pl.when(rows_ref[i] == m)
        def _(m=m):
            acc_ref[0:m, :] += _swiglu_chunk(x_ref[0:m, :], wg_ref, wu_ref, wd_ref)

    @pl.when(f == pl.num_programs(1) - 1)
    def _():
        o_ref[...] = acc_ref[...].astype(BF16).reshape(o_ref.shape)


def _ffn_gather_kernel(te_ref, rows_ref, idx_ref, nxt_ref, x_hbm, wg_ref, wu_ref, wd_ref, o_ref,
                       acc_ref, xbuf, x2d, sem, *, row_counts, per_step):
    i, f = pl.program_id(0), pl.program_id(1)
    n_i, n_f = pl.num_programs(0), pl.num_programs(1)
    tm, d = x2d.shape
    n_rows = xbuf.shape[1]
    slot = i % 2

    def row_copy(ref, s, r):
        return pltpu.make_async_copy(x_hbm.at[ref[0, 0, r]], xbuf.at[s, r], sem.at[s])

    def for_all_rows(fn):
        def body(r, c):
            fn(r)
            return c
        lax.fori_loop(0, n_rows, body, 0, unroll=DMA_UNROLL)

    @pl.when(f == 0)
    def _():
        acc_ref[...] = jnp.zeros_like(acc_ref)

        @pl.when(i == 0)
        def _():
            for_all_rows(lambda r: row_copy(idx_ref, slot, r).start())

        @pl.when(jnp.logical_or(i == 0, rows_ref[jnp.maximum(i - 1, 0)] > 0))
        def _():
            for_all_rows(lambda r: row_copy(idx_ref, slot, r).wait())
            x2d[...] = xbuf[slot, 0:tm].reshape(tm, d)

    for m in row_counts:
        @pl.when(rows_ref[i] == m)
        def _(m=m):
            for u in range(per_step):
                row_copy(nxt_ref, 1 - slot, f * per_step + u).start()
            acc_ref[0:m, :] += _swiglu_chunk(x2d[0:m, :], wg_ref, wu_ref, wd_ref)

    @pl.when(f == n_f - 1)
    def _():
        o_ref[...] = acc_ref[...].astype(BF16).reshape(o_ref.shape)

        @pl.when(jnp.logical_and(i == n_i - 1, rows_ref[i] > 0))
        def _():
            for_all_rows(lambda r: row_copy(nxt_ref, 1 - slot, r).wait())


def ffn(xb, w_gate, w_up, w_down, layer, tile_expert, tile_rows, *, tm, tf, row_counts):
    r, d = xb.shape
    d_ff = w_gate.shape[-1]
    n_f = d_ff // tf
    per_expert = w_gate.ndim == 4

    def chunk(i, f, rows):
        return jnp.where(rows[i] > 0, f, n_f - 1)

    def w_col(i, f, te, rows):
        return (layer, te[i], 0, chunk(i, f, rows)) if per_expert else (layer, 0, chunk(i, f, rows))

    def w_row(i, f, te, rows):
        return (layer, te[i], chunk(i, f, rows), 0) if per_expert else (layer, chunk(i, f, rows), 0)

    lead = (None, None) if per_expert else (None,)
    grid_spec = pltpu.PrefetchScalarGridSpec(
        num_scalar_prefetch=2,
        grid=(r // tm, n_f),
        in_specs=[pl.BlockSpec((tm, d), lambda i, f, te, rows: (i, 0)),
                  pl.BlockSpec((*lead, d, tf), w_col),
                  pl.BlockSpec((*lead, d, tf), w_col),
                  pl.BlockSpec((*lead, tf, d), w_row)],
        out_specs=pl.BlockSpec((tm, d // LANES, LANES), lambda i, f, te, rows: (i, 0, 0)),
        scratch_shapes=[pltpu.VMEM((tm, d), F32)],
    )
    return pl.pallas_call(
        functools.partial(_ffn_kernel, row_counts=row_counts),
        grid_spec=grid_spec,
        out_shape=jax.ShapeDtypeStruct((r, d // LANES, LANES), BF16),
        compiler_params=_params(("parallel", "arbitrary")),
        name="ffn",
    )(tile_expert, tile_rows, xb, w_gate, w_up, w_down)


def ffn_gathered(x3, src, w_gate, w_up, w_down, layer, tile_expert, tile_rows, *, tm, tf, row_counts):
    _, s_dim, l_dim = x3.shape
    d = s_dim * l_dim
    n_tiles, _, n_rows = src.shape
    d_ff = w_gate.shape[-1]
    n_f = d_ff // tf

    def chunk(i, f, rows):
        return jnp.where(rows[i] > 0, f, n_f - 1)

    smem = dict(memory_space=pltpu.SMEM)
    grid_spec = pltpu.PrefetchScalarGridSpec(
        num_scalar_prefetch=2,
        grid=(n_tiles, n_f),
        in_specs=[pl.BlockSpec((1, 1, n_rows), lambda i, f, te, rows: (i, 0, 0), **smem),
                  pl.BlockSpec((1, 1, n_rows), lambda i, f, te, rows: (jnp.minimum(i + 1, n_tiles - 1), 0, 0), **smem),
                  pl.BlockSpec(memory_space=pl.ANY),
                  pl.BlockSpec((None, None, d, tf), lambda i, f, te, rows: (layer, te[i], 0, chunk(i, f, rows))),
                  pl.BlockSpec((None, None, d, tf), lambda i, f, te, rows: (layer, te[i], 0, chunk(i, f, rows))),
                  pl.BlockSpec((None, None, tf, d), lambda i, f, te, rows: (layer, te[i], chunk(i, f, rows), 0))],
        out_specs=pl.BlockSpec((tm, s_dim, l_dim), lambda i, f, te, rows: (i, 0, 0)),
        scratch_shapes=[pltpu.VMEM((tm, d), F32),
                        pltpu.VMEM((2, n_rows, s_dim, l_dim), BF16),
                        pltpu.VMEM((tm, d), BF16),
                        pltpu.SemaphoreType.DMA((2,))],
    )
    return pl.pallas_call(
        functools.partial(_ffn_gather_kernel, row_counts=row_counts, per_step=n_rows // n_f),
        grid_spec=grid_spec,
        out_shape=jax.ShapeDtypeStruct((n_tiles * tm, s_dim, l_dim), BF16),
        compiler_params=_params(("arbitrary", "arbitrary")),
        name="ffn_gathered",
    )(tile_expert, tile_rows, src, src, x3, w_gate, w_up, w_down)


DMA_UNROLL = 8


def _step_and_next_specs(n_steps, width):
    cur = pl.BlockSpec((1, 1, width), lambda i: (i, 0, 0), memory_space=pltpu.SMEM)
    nxt = pl.BlockSpec((1, 1, width), lambda i: (jnp.minimum(i + 1, n_steps - 1), 0, 0), memory_space=pltpu.SMEM)
    return cur, nxt


def _combine_ln_kernel(pos_ref, nxt_ref, w_ref, x_ref, y_hbm, g_ref, b_ref, xo_ref, xb_ref,
                       buf, sem, *, rows, alpha):
    i = pl.program_id(0)
    slot = i % 2

    def row_copy(ref, s, k, r):
        return pltpu.make_async_copy(y_hbm.at[ref[0, 0, k * rows + r]], buf.at[s, k, r], sem.at[s, k])

    def issue(ref, s):
        def body(r, c):
            row_copy(ref, s, 0, r).start()
            row_copy(ref, s, 1, r).start()
            return c
        lax.fori_loop(0, rows, body, 0, unroll=DMA_UNROLL)

    @pl.when(i == 0)
    def _():
        issue(pos_ref, slot)

    @pl.when(i + 1 < pl.num_programs(0))
    def _():
        issue(nxt_ref, 1 - slot)

    def wait(r, c):
        row_copy(pos_ref, slot, 0, r).wait()
        row_copy(pos_ref, slot, 1, r).wait()
        return c

    lax.fori_loop(0, rows, wait, 0, unroll=DMA_UNROLL)
    w = w_ref[...]
    ya = buf[slot, 0].reshape(x_ref.shape).astype(F32)
    yb = buf[slot, 1].reshape(x_ref.shape).astype(F32)
    y = w[:, 2:3] * ya + w[:, 3:4] * yb
    out = _layer_norm(alpha * x_ref[...] + y, g_ref[...], b_ref[...])
    xo_ref[...] = out
    xb_ref[...] = out.astype(BF16)


def combine_ln(x, y, pos, route, ln_g, ln_b, *, alpha, rows):
    n, d = x.shape
    n_steps = n // rows
    kern = functools.partial(_combine_ln_kernel, rows=rows, alpha=alpha)
    row = lambda i: (i, 0)
    fixed = lambda i: (0, 0)
    return pl.pallas_call(
        kern,
        grid=(n_steps,),
        in_specs=[*_step_and_next_specs(n_steps, 2 * rows),
                  pl.BlockSpec((rows, LANES), row),
                  pl.BlockSpec((rows, d), row),
                  pl.BlockSpec(memory_space=pl.ANY),
                  pl.BlockSpec((1, d), fixed),
                  pl.BlockSpec((1, d), fixed)],
        out_specs=[pl.BlockSpec((rows, d), row), pl.BlockSpec((rows, d), row)],
        out_shape=[jax.ShapeDtypeStruct((n, d), F32), jax.ShapeDtypeStruct((n, d), BF16)],
        scratch_shapes=[pltpu.VMEM((2, 2, rows, d // LANES, LANES), BF16), pltpu.SemaphoreType.DMA((2, 2))],
        compiler_params=_params(("arbitrary",)),
        name="combine_ln",
    )(pos, pos, route, x, y, ln_g.reshape(1, d), ln_b.reshape(1, d))


def moe_block(xf, xb3, route, w_gate, w_up, w_down, layer, ln_g, ln_b, *, alpha, tm, tf, rows):
    n, d = xf.shape
    n_e = w_gate.shape[1]
    ef = route[:, 0:2].astype(jnp.int32).reshape(-1)
    onehot = (ef[:, None] == jnp.arange(n_e, dtype=jnp.int32)[None, :]).astype(jnp.int32)
    csum = jnp.cumsum(onehot, axis=0)
    rank = jnp.take_along_axis(csum, ef[:, None], axis=1)[:, 0] - 1
    counts = csum[-1]
    padded = ((counts + tm - 1) // tm) * tm
    ends = jnp.cumsum(padded)
    pos = (ends - padded)[ef] + rank
    r_rows = 2 * n + n_e * tm
    src = jnp.zeros((r_rows,), jnp.int32).at[pos].set(jnp.arange(2 * n, dtype=jnp.int32) // 2)
    tile_start = jnp.arange(r_rows // tm, dtype=jnp.int32) * tm
    te_raw = jnp.minimum(jnp.searchsorted(ends, tile_start, side="right").astype(jnp.int32), n_e - 1)
    last_e = te_raw[jnp.maximum(ends[-1] // tm - 1, 0)]
    tile_expert = jnp.where(tile_start < ends[-1], te_raw, last_e)
    counts_sub = ((counts + FFN_SUB - 1) // FFN_SUB) * FFN_SUB
    tile_rows = jnp.clip(((ends - padded) + counts_sub)[te_raw] - tile_start, 0, tm)

    n_f = w_gate.shape[-1] // tf
    n_req = -(-tm // n_f) * n_f
    src_tiles = jnp.pad(src.reshape(r_rows // tm, tm), ((0, 0), (0, n_req - tm))).reshape(r_rows // tm, 1, n_req)
    ys = ffn_gathered(xb3, src_tiles, w_gate, w_up, w_down, layer, tile_expert, tile_rows, tm=tm, tf=tf,
                      row_counts=tuple(range(FFN_SUB, tm + 1, FFN_SUB)))
    pos2 = pos.reshape(n // rows, rows, 2).transpose(0, 2, 1).reshape(n // rows, 1, 2 * rows)
    return combine_ln(xf, ys, pos2, route, ln_g, ln_b, alpha=alpha, rows=rows)


def kernel(x, w_in, sb_norm_g, conv_w, gla_w_up, gla_b, gla_norm_g, w_out, ln1_g, ln1_b,
           ffn_w_gate, ffn_w_up, ffn_w_down, moe_w_router, moe_w_gate, moe_w_up, moe_w_down,
           ln2_g, ln2_b):
    batch, seq, d = x.shape
    depth = w_in.shape[0]
    n = batch * seq
    alpha = (2 * depth) ** 0.25

    xf = x.reshape(n, d)
    xb = xf.astype(BF16)
    w_r_all = lax.optimization_barrier(w_in[:, :, OFF_G_R:])
    for layer in range(depth):
        w_r_pad = jnp.pad(w_r_all[layer], ((0, 0), (0, LANES - GLA_RANK))).astype(BF16)
        h, h_r = in_proj(xb, w_in, w_r_pad, layer, tm=1024, tn=1024)
        o_sb = sb_attention(h, sb_norm_g[layer], batch=batch, seq=seq)
        o_cv = gated_conv(h, conv_w[layer], batch=batch, seq=seq)
        w_up_pad = jnp.pad(gla_w_up[layer], ((0, LANES - GLA_RANK), (0, 0))).astype(BF16)
        o_gla = gla(h, h_r, w_up_pad, gla_b[layer], gla_norm_g[layer], batch=batch, seq=seq)
        j = layer // 2
        if layer % 2 == 1:
            xf, xb, route = outproj_ln(o_sb, o_cv, o_gla, w_out, layer, xf, ln1_g[layer], ln1_b[layer],
                                       moe_w_router[j], alpha=alpha, tm=256)
            xf, xb = moe_block(xf, xb, route, moe_w_gate, moe_w_up, moe_w_down, j,
                               ln2_g[layer], ln2_b[layer], alpha=alpha, tm=FFN_TM, tf=FFN_TF, rows=256)
        else:
            xf, xb = outproj_ln(o_sb, o_cv, o_gla, w_out, layer, xf, ln1_g[layer], ln1_b[layer], None,
                                alpha=alpha, tm=256)
            n_tiles = n // FFN_TM
            f = ffn(xb, ffn_w_gate, ffn_w_up, ffn_w_down, j, jnp.zeros((n_tiles,), jnp.int32),
                    jnp.full((n_tiles,), FFN_TM, jnp.int32), tm=FFN_TM, tf=FFN_TF, row_counts=(FFN_TM,))
            xf, xb = residual_ln(xf, f, ln2_g[layer], ln2_b[layer], alpha=alpha, tm=512)
    return xf.reshape(batch, seq, d)
```

```python
import functools

import jax
import jax.numpy as jnp
from jax import lax
from jax.experimental import pallas as pl
from jax.experimental.pallas import tpu as pltpu

F32 = jnp.float32
BF16 = jnp.bfloat16

D_MODEL = 2048
SB_HEADS = 8
SB_DH = 128
SB_WIDTH = SB_HEADS * SB_DH
CONV_WIDTH = 512
CONV_K = 3
GLA_HEADS = 4
GLA_DK = 64
GLA_DV = 128
GLA_KW = GLA_HEADS * GLA_DK
GLA_VW = GLA_HEADS * GLA_DV
GLA_RANK = 16
GLA_TAU = 16.0
GLA_CHUNK = 16
IN_COLS = 3 * SB_WIDTH + 3 * CONV_WIDTH + 2 * GLA_KW + 2 * GLA_VW + GLA_RANK
N_EXPERTS = 8
LN_EPS = 1e-5
RMS_EPS = 1e-6

LANES = 128
VMEM_LIMIT = 56 * 1024 * 1024

OFF_SB_Q, OFF_SB_K, OFF_SB_V = 0, SB_WIDTH, 2 * SB_WIDTH
OFF_CV_B = 3 * SB_WIDTH
OFF_CV_C = OFF_CV_B + CONV_WIDTH
OFF_CV_H = OFF_CV_C + CONV_WIDTH
OFF_G_Q = OFF_CV_H + CONV_WIDTH
OFF_G_K = OFF_G_Q + GLA_KW
OFF_G_V = OFF_G_K + GLA_KW
OFF_G_G = OFF_G_V + GLA_VW
OFF_G_R = OFF_G_G + GLA_VW

SB_EXIT = -90.0


def _params(sem):
    return pltpu.CompilerParams(dimension_semantics=sem, vmem_limit_bytes=VMEM_LIMIT)


def _split_bf16(x):
    hi = x.astype(BF16)
    lo = (x - hi.astype(F32)).astype(BF16)
    return hi, lo


def _dot(a, b):
    return jnp.dot(a, b, preferred_element_type=F32)


def _silu(x):
    return x * (1.0 / (1.0 + jnp.exp(-x)))


def _in_proj_kernel(x_ref, w_ref, wr_ref, h_ref, r_ref):
    x = x_ref[...]
    h_ref[...] = _dot(x, w_ref[...].astype(BF16)).astype(h_ref.dtype)

    @pl.when(pl.program_id(1) == 0)
    def _():
        r_ref[...] = _dot(x, wr_ref[...]).astype(r_ref.dtype)


def in_proj(xb, w_in, w_r_pad, layer, *, tm, tn):
    n, d = xb.shape
    return pl.pallas_call(
        _in_proj_kernel,
        grid=(n // tm, OFF_G_R // tn),
        in_specs=[pl.BlockSpec((tm, d), lambda i, j: (i, 0)),
                  pl.BlockSpec((None, d, tn), lambda i, j: (layer, 0, j)),
                  pl.BlockSpec((d, LANES), lambda i, j: (0, 0))],
        out_specs=[pl.BlockSpec((tm, tn), lambda i, j: (i, j)),
                   pl.BlockSpec((tm, LANES), lambda i, j: (i, 0))],
        out_shape=[jax.ShapeDtypeStruct((n, OFF_G_R), BF16), jax.ShapeDtypeStruct((n, LANES), BF16)],
        compiler_params=_params(("parallel", "arbitrary")),
        name="in_proj",
    )(xb, w_in, w_r_pad)


SB_BLK = 64
SB_WIN = 256
SB_GROUP = 16


LOG2E = 1.4426950408889634
SB_EXIT_LOG2 = SB_EXIT * LOG2E


def _minus_later_keys_matrix(n_keys):
    r = lax.broadcasted_iota(jnp.int32, (2 * n_keys, n_keys), 0)
    c = lax.broadcasted_iota(jnp.int32, (2 * n_keys, n_keys), 1)
    return jnp.where((r % n_keys) > c, -1.0, 0.0).astype(BF16)


def _sb_scores(q, k, scale):
    y = lax.dot_general(q, k, (((1,), (1,)), ((), ())), preferred_element_type=F32) * (scale * LOG2E)
    sp = jnp.maximum(y, 0.0) + jnp.log2(1.0 + jnp.exp2(-jnp.abs(y)))
    return y, sp


def _sb_tail(sp_masked, minus_later):
    hi, lo = _split_bf16(sp_masked)
    return _dot(jnp.concatenate([hi, lo], axis=1), minus_later)


def _sb_kernel(q_ref, k_ref, v_ref, g_ref, o_ref, acc_ref, c_ref, *, group, scale):
    minus_later_win = _minus_later_keys_matrix(SB_WIN)
    minus_later_blk = _minus_later_keys_matrix(SB_BLK)
    row = lax.broadcasted_iota(jnp.int32, (SB_BLK, SB_WIN), 0)
    col = lax.broadcasted_iota(jnp.int32, (SB_BLK, SB_WIN), 1)
    causal_full = col < row + (SB_WIN - SB_BLK)
    n_clamped = (SB_WIN - SB_BLK) // SB_BLK
    gain = g_ref[...]

    def q_rows(qi):
        return pl.ds(pl.multiple_of(qi * SB_BLK, SB_BLK), SB_BLK)

    def windows(qis):
        wss = [pl.multiple_of(jnp.maximum(qi - n_clamped, 0) * SB_BLK, SB_BLK) for qi in qis]
        causal = [col < row + (qi * SB_BLK - ws) for qi, ws in zip(qis[:n_clamped], wss)]
        causal += [causal_full] * (len(qis) - n_clamped)
        scores = [_sb_scores(q_ref[q_rows(qi), :], k_ref[pl.ds(ws, SB_WIN), :], scale) for qi, ws in zip(qis, wss)]
        sp_m = [jnp.where(m, sp, 0.0) for m, (_, sp) in zip(causal, scores)]
        tails = [_sb_tail(s, minus_later_win) for s in sp_m]
        c_max = None
        for u, (m, (y, sp), s, tail, ws) in enumerate(zip(causal, scores, sp_m, tails, wss)):
            a = jnp.where(m, jnp.exp2(y - sp + tail), 0.0)
            acc_ref[u] = _dot(a.astype(BF16), v_ref[pl.ds(ws, SB_WIN), :])
            c = tail[:, 0:1] - s[:, 0:1]
            c_ref[u] = c
            c_max = c if c_max is None else jnp.maximum(c_max, c)
        return c_max

    def earlier_blocks(qi, u):
        q = q_ref[q_rows(qi), :]

        def body(carry):
            j, _ = carry
            start = pl.multiple_of(j * SB_BLK, SB_BLK)
            y, sp = _sb_scores(q, k_ref[pl.ds(start, SB_BLK), :], scale)
            tail = _sb_tail(sp, minus_later_blk)
            c = c_ref[u]
            a = jnp.exp2(y - sp + tail + c)
            acc_ref[u] += _dot(a.astype(BF16), v_ref[pl.ds(start, SB_BLK), :])
            c_new = c + tail[:, 0:1] - sp[:, 0:1]
            c_ref[u] = c_new
            return j - 1, jnp.max(c_new) > SB_EXIT_LOG2

        def cond(carry):
            j, more = carry
            return jnp.logical_and(j >= 0, more)

        lax.while_loop(cond, body, (jnp.maximum(qi - n_clamped, 0) - 1, jnp.max(c_ref[u]) > SB_EXIT_LOG2))

    def run_group(gi, carry):
        c_max = windows([gi * group + u for u in range(group)])

        @pl.when(jnp.max(c_max) > SB_EXIT_LOG2)
        def _():
            for u in range(group):
                earlier_blocks(gi * group + u, u)

        for u in range(group):
            o = acc_ref[u]
            ms = jnp.mean(o * o, axis=-1, keepdims=True)
            rows = pl.ds(pl.multiple_of((gi * group + u) * SB_BLK, SB_BLK), SB_BLK)
            o_ref[rows, :] = (o * lax.rsqrt(ms + RMS_EPS) * gain).astype(o_ref.dtype)
        return carry

    lax.fori_loop(0, q_ref.shape[0] // (SB_BLK * group), run_group, 0)


def sb_attention(h, norm_g, *, batch, seq):
    n = batch * seq
    qb, kb, vb = OFF_SB_Q // SB_DH, OFF_SB_K // SB_DH, OFF_SB_V // SB_DH
    group = min(SB_GROUP, seq // SB_BLK)
    kern = functools.partial(_sb_kernel, group=group, scale=SB_DH ** -0.5)
    return pl.pallas_call(
        kern,
        grid=(batch, SB_HEADS),
        in_specs=[pl.BlockSpec((seq, SB_DH), lambda b, hh: (b, qb + hh)),
                  pl.BlockSpec((seq, SB_DH), lambda b, hh: (b, kb + hh)),
                  pl.BlockSpec((seq, SB_DH), lambda b, hh: (b, vb + hh)),
                  pl.BlockSpec((1, SB_DH), lambda b, hh: (0, 0))],
        out_specs=pl.BlockSpec((seq, SB_DH), lambda b, hh: (b, hh)),
        out_shape=jax.ShapeDtypeStruct((n, SB_WIDTH), BF16),
        scratch_shapes=[pltpu.VMEM((group, SB_BLK, SB_DH), F32), pltpu.VMEM((group, SB_BLK, 1), F32)],
        compiler_params=_params(("parallel", "parallel")),
        name="sb_attention",
    )(h, h, h, norm_g.reshape(1, SB_DH))


def _conv_kernel(b_ref, c_ref, h_ref, w_ref, o_ref):
    u = c_ref[...].astype(F32) * h_ref[...].astype(F32)
    t = lax.broadcasted_iota(jnp.int32, u.shape, 0)
    u1 = jnp.where(t >= 1, pltpu.roll(u, 1, 0), 0.0)
    u2 = jnp.where(t >= 2, pltpu.roll(u, 2, 0), 0.0)
    w = w_ref[...]
    y = w[0:1, :] * u2 + w[1:2, :] * u1 + w[2:3, :] * u
    o_ref[...] = (b_ref[...].astype(F32) * y).astype(o_ref.dtype)


def gated_conv(h, conv_w, *, batch, seq):
    n = batch * seq
    nb = CONV_WIDTH // LANES
    bb, cb, hb = OFF_CV_B // LANES, OFF_CV_C // LANES, OFF_CV_H // LANES
    return pl.pallas_call(
        _conv_kernel,
        grid=(batch, nb),
        in_specs=[pl.BlockSpec((seq, LANES), lambda b, j: (b, bb + j)),
                  pl.BlockSpec((seq, LANES), lambda b, j: (b, cb + j)),
                  pl.BlockSpec((seq, LANES), lambda b, j: (b, hb + j)),
                  pl.BlockSpec((CONV_K, LANES), lambda b, j: (0, j))],
        out_specs=pl.BlockSpec((seq, LANES), lambda b, j: (b, j)),
        out_shape=jax.ShapeDtypeStruct((n, CONV_WIDTH), BF16),
        compiler_params=_params(("parallel", "parallel")),
        name="gated_conv",
    )(h, h, h, conv_w)


GLA_SEG = 128
GLA_SEGS_PER_STEP = 2


def _gla_kernel(q_ref, k_ref, v_ref, g_ref, r_ref, wup_ref, b_ref, gn_ref, o_ref, s_ref, *, n_seg):
    seg_len, chunk = GLA_SEG, GLA_CHUNK
    n_chunk = seg_len // chunk
    two_dk, two_dv = 2 * GLA_DK, 2 * GLA_DV

    row = lax.broadcasted_iota(jnp.int32, (seg_len, seg_len), 0)
    col = lax.broadcasted_iota(jnp.int32, (seg_len, seg_len), 1)
    same = (row // chunk) == (col // chunk)
    cum_m = jnp.logical_and(same, col <= row).astype(BF16)
    tot_m = same.astype(BF16)
    r2 = lax.broadcasted_iota(jnp.int32, (two_dk, two_dv), 0)
    c2 = lax.broadcasted_iota(jnp.int32, (two_dk, two_dv), 1)
    head_bd = (r2 // GLA_DK) == (c2 // GLA_DV)
    ones_bd = head_bd.astype(BF16)
    bd_f = head_bd.astype(F32)
    band = jnp.where(jnp.logical_and(same, col <= row), row - col, -1)
    lane_chunk = col // chunk

    wup = wup_ref[...]
    bias = b_ref[...]
    gn = gn_ref[...]
    s_ref[...] = jnp.zeros_like(s_ref)

    def rms(x):
        return x * lax.rsqrt(jnp.mean(x * x, axis=-1, keepdims=True) + RMS_EPS)

    def gates(rows):
        u = _dot(r_ref[rows, :], wup) + bias
        la = (jnp.minimum(u, 0.0) - jnp.log1p(jnp.exp(-jnp.abs(u)))) * (1.0 / GLA_TAU)
        hi, lo = _split_bf16(la)
        return _dot(cum_m, hi) + _dot(cum_m, lo), _dot(tot_m, hi) + _dot(tot_m, lo)

    def intra(q, k, v, bcum):
        score = _dot((q * k).astype(BF16), ones_bd)
        attn0 = jnp.where(band == 0, score[:, :GLA_DV], 0.0)
        attn1 = jnp.where(band == 0, score[:, GLA_DV:], 0.0)
        for d in range(1, chunk):
            kd = pltpu.roll(k, d, 0)
            bd = pltpu.roll(bcum, d, 0)
            p = q * kd * jnp.exp(bcum - bd)
            score = _dot(p.astype(BF16), ones_bd)
            attn0 = jnp.where(band == d, score[:, :GLA_DV], attn0)
            attn1 = jnp.where(band == d, score[:, GLA_DV:], attn1)
        return jnp.concatenate([_dot(attn0.astype(BF16), v[:, :GLA_DV]),
                                _dot(attn1.astype(BF16), v[:, GLA_DV:])], axis=1)

    def seg_group(gi, carry):
        all_rows = [pl.ds(pl.multiple_of((gi * GLA_SEGS_PER_STEP + s) * seg_len, seg_len), seg_len)
                    for s in range(GLA_SEGS_PER_STEP)]
        qs = [q_ref[rows, :].astype(F32) * (GLA_DK ** -0.5) for rows in all_rows]
        ks = [k_ref[rows, :].astype(F32) for rows in all_rows]
        vs = [v_ref[rows, :] for rows in all_rows]
        bcums, blasts = zip(*[gates(rows) for rows in all_rows])
        o_accs = [intra(q, k, v, bcum) for q, k, v, bcum in zip(qs, ks, vs, bcums)]

        q_ins = [(q * jnp.exp(bcum)).astype(BF16) for q, bcum in zip(qs, bcums)]
        k_up_ts = [(k * jnp.exp(blast - bcum)).T for k, bcum, blast in zip(ks, bcums, blasts)]
        g_ts = [jnp.exp(blast.T) for blast in blasts]
        kvs = [[_dot(jnp.where(lane_chunk == c, k_up_t, 0.0).astype(BF16), v) for c in range(n_chunk)]
               for k_up_t, v in zip(k_up_ts, vs)]
        state = s_ref[...]
        states = []
        for g_t, kv in zip(g_ts, kvs):
            seg_states = []
            for c in range(n_chunk):
                seg_states.append(state.astype(BF16))
                state = g_t[:, c * chunk:c * chunk + 1] * state + kv[c] * bd_f
            states.append(seg_states)
        s_ref[...] = state

        for rows, o_acc, q_in, seg_states in zip(all_rows, o_accs, q_ins, states):
            o_inter = [_dot(q_in[c * chunk:(c + 1) * chunk, :], seg_states[c]) for c in range(n_chunk)]
            o_acc = o_acc + jnp.concatenate(o_inter, axis=0)
            gate = g_ref[rows, :].astype(F32)
            o0 = rms(o_acc[:, :GLA_DV]) * gn
            o1 = rms(o_acc[:, GLA_DV:]) * gn
            out = jnp.concatenate([o0, o1], axis=1) * _silu(gate)
            o_ref[rows, :] = out.astype(o_ref.dtype)
        return carry

    lax.fori_loop(0, n_seg // GLA_SEGS_PER_STEP, seg_group, 0)


def gla(h, h_r, w_up_pad, gla_b, gla_norm_g, *, batch, seq):
    n = batch * seq
    pairs = GLA_HEADS // 2
    qb, kb = OFF_G_Q // LANES, OFF_G_K // LANES
    vb, gb = OFF_G_V // (2 * GLA_DV), OFF_G_G // (2 * GLA_DV)
    kern = functools.partial(_gla_kernel, n_seg=seq // GLA_SEG)
    return pl.pallas_call(
        kern,
        grid=(batch, pairs),
        in_specs=[pl.BlockSpec((seq, LANES), lambda b, p: (b, qb + p)),
                  pl.BlockSpec((seq, LANES), lambda b, p: (b, kb + p)),
                  pl.BlockSpec((seq, 2 * GLA_DV), lambda b, p: (b, vb + p)),
                  pl.BlockSpec((seq, 2 * GLA_DV), lambda b, p: (b, gb + p)),
                  pl.BlockSpec((seq, LANES), lambda b, p: (b, 0)),
                  pl.BlockSpec((LANES, LANES), lambda b, p: (0, p)),
                  pl.BlockSpec((1, LANES), lambda b, p: (0, p)),
                  pl.BlockSpec((1, GLA_DV), lambda b, p: (0, 0))],
        out_specs=pl.BlockSpec((seq, 2 * GLA_DV), lambda b, p: (b, p)),
        out_shape=jax.ShapeDtypeStruct((n, GLA_VW), BF16),
        scratch_shapes=[pltpu.VMEM((2 * GLA_DK, 2 * GLA_DV), F32)],
        compiler_params=_params(("parallel", "parallel")),
        name="gla",
    )(h, h, h, h, h_r, w_up_pad, gla_b.reshape(1, GLA_KW), gla_norm_g.reshape(1, GLA_DV))


def _layer_norm(y, g, b):
    mu = jnp.mean(y, axis=-1, keepdims=True)
    yc = y - mu
    var = jnp.mean(yc * yc, axis=-1, keepdims=True)
    return yc * lax.rsqrt(var + LN_EPS) * g + b


def _top2_route(logits):
    col = lax.broadcasted_iota(jnp.int32, logits.shape, 1)
    neg = jnp.float32(-jnp.inf)
    l1 = jnp.where(col < N_EXPERTS, logits, neg)
    m1 = jnp.max(l1, axis=-1, keepdims=True)
    i1 = jnp.min(jnp.where(l1 == m1, col, LANES), axis=-1, keepdims=True)
    l2 = jnp.where(col == i1, neg, l1)
    m2 = jnp.max(l2, axis=-1, keepdims=True)
    i2 = jnp.min(jnp.where(l2 == m2, col, LANES), axis=-1, keepdims=True)
    e2 = jnp.exp(m2 - m1)
    w1 = 1.0 / (1.0 + e2)
    w2 = e2 / (1.0 + e2)
    out = jnp.where(col == 0, i1.astype(F32), 0.0) + jnp.where(col == 1, i2.astype(F32), 0.0)
    return out + jnp.where(col == 2, w1, 0.0) + jnp.where(col == 3, w2, 0.0)


def _outproj_ln_kernel(*refs, alpha, routed):
    if routed:
        sb_ref, cv_ref, gl_ref, w_ref, x_ref, g_ref, b_ref, wr_ref, xo_ref, xb_ref, route_ref, wb_ref = refs
    else:
        sb_ref, cv_ref, gl_ref, w_ref, x_ref, g_ref, b_ref, xo_ref, xb_ref, wb_ref = refs

    @pl.when(pl.program_id(0) == 0)
    def _():
        wb_ref[...] = w_ref[...].astype(BF16)

    mix = _dot(sb_ref[...], wb_ref[0:SB_WIDTH, :])
    mix = mix + _dot(cv_ref[...], wb_ref[SB_WIDTH:SB_WIDTH + CONV_WIDTH, :])
    mix = mix + _dot(gl_ref[...], wb_ref[SB_WIDTH + CONV_WIDTH:, :])
    out = _layer_norm(alpha * x_ref[...] + mix, g_ref[...], b_ref[...])
    xo_ref[...] = out
    xb_ref[...] = out.astype(BF16).reshape(xb_ref.shape)
    if routed:
        col = lax.broadcasted_iota(jnp.int32, route_ref.shape, 1)
        logits = jnp.zeros(route_ref.shape, F32)
        for e in range(N_EXPERTS):
            logit_e = jnp.sum(out * wr_ref[e:e + 1, :], axis=-1, keepdims=True)
            logits = jnp.where(col == e, logit_e, logits)
        route_ref[...] = _top2_route(logits)


def _token_major_spec(tm, d, tiled):
    if tiled:
        return pl.BlockSpec((tm, d // LANES, LANES), lambda i: (i, 0, 0))
    return pl.BlockSpec((tm, d), lambda i: (i, 0))


def _token_major_shape(n, d, tiled):
    return (n, d // LANES, LANES) if tiled else (n, d)


def outproj_ln(o_sb, o_cv, o_gla, w_out, layer, x, ln_g, ln_b, w_router, *, alpha, tm):
    n, d = x.shape
    routed = w_router is not None
    kern = functools.partial(_outproj_ln_kernel, alpha=alpha, routed=routed)
    row = lambda i: (i, 0)
    fixed = lambda i: (0, 0)
    in_specs = [pl.BlockSpec((tm, SB_WIDTH), row),
                pl.BlockSpec((tm, CONV_WIDTH), row),
                pl.BlockSpec((tm, GLA_VW), row),
                pl.BlockSpec((None, d, d), lambda i: (layer, 0, 0), pipeline_mode=pl.Buffered(1)),
                pl.BlockSpec((tm, d), row),
                pl.BlockSpec((1, d), fixed),
                pl.BlockSpec((1, d), fixed)]
    args = [o_sb, o_cv, o_gla, w_out, x, ln_g.reshape(1, d), ln_b.reshape(1, d)]
    out_specs = [pl.BlockSpec((tm, d), row), _token_major_spec(tm, d, routed)]
    out_shape = [jax.ShapeDtypeStruct((n, d), F32), jax.ShapeDtypeStruct(_token_major_shape(n, d, routed), BF16)]
    if routed:
        in_specs.append(pl.BlockSpec((N_EXPERTS, d), fixed))
        args.append(w_router.T)
        out_specs.append(pl.BlockSpec((tm, LANES), row))
        out_shape.append(jax.ShapeDtypeStruct((n, LANES), F32))
    return pl.pallas_call(
        kern,
        grid=(n // tm,),
        in_specs=in_specs,
        out_specs=out_specs,
        out_shape=out_shape,
        scratch_shapes=[pltpu.VMEM((d, d), BF16)],
        compiler_params=_params(("arbitrary",)),
        name="outproj_ln",
    )(*args)


def _residual_ln_kernel(x_ref, f_ref, g_ref, b_ref, xo_ref, xb_ref, *, alpha):
    f = f_ref[...].reshape(x_ref.shape).astype(F32)
    out = _layer_norm(alpha * x_ref[...] + f, g_ref[...], b_ref[...])
    xo_ref[...] = out
    xb_ref[...] = out.astype(BF16)


def residual_ln(x, f, ln_g, ln_b, *, alpha, tm):
    n, d = x.shape
    kern = functools.partial(_residual_ln_kernel, alpha=alpha)
    row = lambda i: (i, 0)
    fixed = lambda i: (0, 0)
    return pl.pallas_call(
        kern,
        grid=(n // tm,),
        in_specs=[pl.BlockSpec((tm, d), row), _token_major_spec(tm, d, True),
                  pl.BlockSpec((1, d), fixed), pl.BlockSpec((1, d), fixed)],
        out_specs=[pl.BlockSpec((tm, d), row), pl.BlockSpec((tm, d), row)],
        out_shape=[jax.ShapeDtypeStruct((n, d), F32), jax.ShapeDtypeStruct((n, d), BF16)],
        compiler_params=_params(("parallel",)),
        name="residual_ln",
    )(x, f, ln_g.reshape(1, d), ln_b.reshape(1, d))


FFN_TM = 1024
FFN_TF = 256
FFN_SUB = 256


def _swiglu_chunk(x, wg_ref, wu_ref, wd_ref):
    hg = _dot(x, wg_ref[...].astype(BF16))
    hu = _dot(x, wu_ref[...].astype(BF16))
    hidden = (_silu(hg) * hu).astype(BF16)
    return _dot(hidden, wd_ref[...].astype(BF16))


def _ffn_kernel(te_ref, rows_ref, x_ref, wg_ref, wu_ref, wd_ref, o_ref, acc_ref, *, row_counts):
    i, f = pl.program_id(0), pl.program_id(1)

    @pl.when(f == 0)
    def _():
        acc_ref[...] = jnp.zeros_like(acc_ref)

    for m in row_counts:
        @pl.when(rows_ref[i] == m)
        def _(m=m):
            acc_ref[0:m, :] += _swiglu_chunk(x_ref[0:m, :], wg_ref, wu_ref, wd_ref)

    @pl.when(f == pl.num_programs(1) - 1)
    def _():
        o_ref[...] = acc_ref[...].astype(BF16).reshape(o_ref.shape)


def _ffn_gather_kernel(te_ref, rows_ref, idx_ref, nxt_ref, x_hbm, wg_ref, wu_ref, wd_ref, o_ref,
                       acc_ref, xbuf, x2d, sem, *, row_counts, per_step):
    i, f = pl.program_id(0), pl.program_id(1)
    n_i, n_f = pl.num_programs(0), pl.num_programs(1)
    tm, d = x2d.shape
    n_rows = xbuf.shape[1]
    slot = i % 2

    def row_copy(ref, s, r):
        return pltpu.make_async_copy(x_hbm.at[ref[0, 0, r]], xbuf.at[s, r], sem.at[s])

    def for_all_rows(fn):
        def body(r, c):
            fn(r)
            return c
        lax.fori_loop(0, n_rows, body, 0, unroll=DMA_UNROLL)

    @pl.when(f == 0)
    def _():
        acc_ref[...] = jnp.zeros_like(acc_ref)

        @pl.when(i == 0)
        def _():
            for_all_rows(lambda r: row_copy(idx_ref, slot, r).start())

        @pl.when(jnp.logical_or(i == 0, rows_ref[jnp.maximum(i - 1, 0)] > 0))
        def _():
            for_all_rows(lambda r: row_copy(idx_ref, slot, r).wait())
            x2d[...] = xbuf[slot, 0:tm].reshape(tm, d)

    for m in row_counts:
        @pl.when(rows_ref[i] == m)
        def _(m=m):
            for u in range(per_step):
                row_copy(nxt_ref, 1 - slot, f * per_step + u).start()
            acc_ref[0:m, :] += _swiglu_chunk(x2d[0:m, :], wg_ref, wu_ref, wd_ref)

    @pl.when(f == n_f - 1)
    def _():
        o_ref[...] = acc_ref[...].astype(BF16).reshape(o_ref.shape)

        @pl.when(jnp.logical_and(i == n_i - 1, rows_ref[i] > 0))
        def _():
            for_all_rows(lambda r: row_copy(nxt_ref, 1 - slot, r).wait())


def ffn(xb, w_gate, w_up, w_down, layer, tile_expert, tile_rows, *, tm, tf, row_counts):
    r, d = xb.shape
    d_ff = w_gate.shape[-1]
    n_f = d_ff // tf
    per_expert = w_gate.ndim == 4

    def chunk(i, f, rows):
        return jnp.where(rows[i] > 0, f, n_f - 1)

    def w_col(i, f, te, rows):
        return (layer, te[i], 0, chunk(i, f, rows)) if per_expert else (layer, 0, chunk(i, f, rows))

    def w_row(i, f, te, rows):
        return (layer, te[i], chunk(i, f, rows), 0) if per_expert else (layer, chunk(i, f, rows), 0)

    lead = (None, None) if per_expert else (None,)
    grid_spec = pltpu.PrefetchScalarGridSpec(
        num_scalar_prefetch=2,
        grid=(r // tm, n_f),
        in_specs=[pl.BlockSpec((tm, d), lambda i, f, te, rows: (i, 0)),
                  pl.BlockSpec((*lead, d, tf), w_col),
                  pl.BlockSpec((*lead, d, tf), w_col),
                  pl.BlockSpec((*lead, tf, d), w_row)],
        out_specs=pl.BlockSpec((tm, d // LANES, LANES), lambda i, f, te, rows: (i, 0, 0)),
        scratch_shapes=[pltpu.VMEM((tm, d), F32)],
    )
    return pl.pallas_call(
        functools.partial(_ffn_kernel, row_counts=row_counts),
        grid_spec=grid_spec,
        out_shape=jax.ShapeDtypeStruct((r, d // LANES, LANES), BF16),
        compiler_params=_params(("parallel", "arbitrary")),
        name="ffn",
    )(tile_expert, tile_rows, xb, w_gate, w_up, w_down)


def ffn_gathered(x3, src, w_gate, w_up, w_down, layer, tile_expert, tile_rows, *, tm, tf, row_counts):
    _, s_dim, l_dim = x3.shape
    d = s_dim * l_dim
    n_tiles, _, n_rows = src.shape
    d_ff = w_gate.shape[-1]
    n_f = d_ff // tf

    def chunk(i, f, rows):
        return jnp.where(rows[i] > 0, f, n_f - 1)

    smem = dict(memory_space=pltpu.SMEM)
    grid_spec = pltpu.PrefetchScalarGridSpec(
        num_scalar_prefetch=2,
        grid=(n_tiles, n_f),
        in_specs=[pl.BlockSpec((1, 1, n_rows), lambda i, f, te, rows: (i, 0, 0), **smem),
                  pl.BlockSpec((1, 1, n_rows), lambda i, f, te, rows: (jnp.minimum(i + 1, n_tiles - 1), 0, 0), **smem),
                  pl.BlockSpec(memory_space=pl.ANY),
                  pl.BlockSpec((None, None, d, tf), lambda i, f, te, rows: (layer, te[i], 0, chunk(i, f, rows))),
                  pl.BlockSpec((None, None, d, tf), lambda i, f, te, rows: (layer, te[i], 0, chunk(i, f, rows))),
                  pl.BlockSpec((None, None, tf, d), lambda i, f, te, rows: (layer, te[i], chunk(i, f, rows), 0))],
        out_specs=pl.BlockSpec((tm, s_dim, l_dim), lambda i, f, te, rows: (i, 0, 0)),
        scratch_shapes=[pltpu.VMEM((tm, d), F32),
                        pltpu.VMEM((2, n_rows, s_dim, l_dim), BF16),
                        pltpu.VMEM((tm, d), BF16),
                        pltpu.SemaphoreType.DMA((2,))],
    )
    return pl.pallas_call(
        functools.partial(_ffn_gather_kernel, row_counts=row_counts, per_step=n_rows // n_f),
        grid_spec=grid_spec,
        out_shape=jax.ShapeDtypeStruct((n_tiles * tm, s_dim, l_dim), BF16),
        compiler_params=_params(("arbitrary", "arbitrary")),
        name="ffn_gathered",
    )(tile_expert, tile_rows, src, src, x3, w_gate, w_up, w_down)


DMA_UNROLL = 8


def _step_and_next_specs(n_steps, width):
    cur = pl.BlockSpec((1, 1, width), lambda i: (i, 0, 0), memory_space=pltpu.SMEM)
    nxt = pl.BlockSpec((1, 1, width), lambda i: (jnp.minimum(i + 1, n_steps - 1), 0, 0), memory_space=pltpu.SMEM)
    return cur, nxt


def _combine_ln_kernel(pos_ref, nxt_ref, w_ref, x_ref, y_hbm, g_ref, b_ref, xo_ref, xb_ref,
                       buf, sem, *, rows, alpha):
    i = pl.program_id(0)
    slot = i % 2

    def row_copy(ref, s, k, r):
        return pltpu.make_async_copy(y_hbm.at[ref[0, 0, k * rows + r]], buf.at[s, k, r], sem.at[s, k])

    def issue(ref, s):
        def body(r, c):
            row_copy(ref, s, 0, r).start()
            row_copy(ref, s, 1, r).start()
            return c
        lax.fori_loop(0, rows, body, 0, unroll=DMA_UNROLL)

    @pl.when(i == 0)
    def _():
        issue(pos_ref, slot)

    @pl.when(i + 1 < pl.num_programs(0))
    def _():
        issue(nxt_ref, 1 - slot)

    def wait(r, c):
        row_copy(pos_ref, slot, 0, r).wait()
        row_copy(pos_ref, slot, 1, r).wait()
        return c

    lax.fori_loop(0, rows, wait, 0, unroll=DMA_UNROLL)
    w = w_ref[...]
    ya = buf[slot, 0].reshape(x_ref.shape).astype(F32)
    yb = buf[slot, 1].reshape(x_ref.shape).astype(F32)
    y = w[:, 2:3] * ya + w[:, 3:4] * yb
    out = _layer_norm(alpha * x_ref[...] + y, g_ref[...], b_ref[...])
    xo_ref[...] = out
    xb_ref[...] = out.astype(BF16)


def combine_ln(x, y, pos, route, ln_g, ln_b, *, alpha, rows):
    n, d = x.shape
    n_steps = n // rows
    kern = functools.partial(_combine_ln_kernel, rows=rows, alpha=alpha)
    row = lambda i: (i, 0)
    fixed = lambda i: (0, 0)
    return pl.pallas_call(
        kern,
        grid=(n_steps,),
        in_specs=[*_step_and_next_specs(n_steps, 2 * rows),
                  pl.BlockSpec((rows, LANES), row),
                  pl.BlockSpec((rows, d), row),
                  pl.BlockSpec(memory_space=pl.ANY),
                  pl.BlockSpec((1, d), fixed),
                  pl.BlockSpec((1, d), fixed)],
        out_specs=[pl.BlockSpec((rows, d), row), pl.BlockSpec((rows, d), row)],
        out_shape=[jax.ShapeDtypeStruct((n, d), F32), jax.ShapeDtypeStruct((n, d), BF16)],
        scratch_shapes=[pltpu.VMEM((2, 2, rows, d // LANES, LANES), BF16), pltpu.SemaphoreType.DMA((2, 2))],
        compiler_params=_params(("arbitrary",)),
        name="combine_ln",
    )(pos, pos, route, x, y, ln_g.reshape(1, d), ln_b.reshape(1, d))


def moe_block(xf, xb3, route, w_gate, w_up, w_down, layer, ln_g, ln_b, *, alpha, tm, tf, rows):
    n, d = xf.shape
    n_e = w_gate.shape[1]
    ef = route[:, 0:2].astype(jnp.int32).reshape(-1)
    onehot = (ef[:, None] == jnp.arange(n_e, dtype=jnp.int32)[None, :]).astype(jnp.int32)
    csum = jnp.cumsum(onehot, axis=0)
    rank = jnp.take_along_axis(csum, ef[:, None], axis=1)[:, 0] - 1
    counts = csum[-1]
    padded = ((counts + tm - 1) // tm) * tm
    ends = jnp.cumsum(padded)
    pos = (ends - padded)[ef] + rank
    r_rows = 2 * n + n_e * tm
    src = jnp.zeros((r_rows,), jnp.int32).at[pos].set(jnp.arange(2 * n, dtype=jnp.int32) // 2)
    tile_start = jnp.arange(r_rows // tm, dtype=jnp.int32) * tm
    te_raw = jnp.minimum(jnp.searchsorted(ends, tile_start, side="right").astype(jnp.int32), n_e - 1)
    last_e = te_raw[jnp.maximum(ends[-1] // tm - 1, 0)]
    tile_expert = jnp.where(tile_start < ends[-1], te_raw, last_e)
    counts_sub = ((counts + FFN_SUB - 1) // FFN_SUB) * FFN_SUB
    tile_rows = jnp.clip(((ends - padded) + counts_sub)[te_raw] - tile_start, 0, tm)

    n_f = w_gate.shape[-1] // tf
    n_req = -(-tm // n_f) * n_f
    src_tiles = jnp.pad(src.reshape(r_rows // tm, tm), ((0, 0), (0, n_req - tm))).reshape(r_rows // tm, 1, n_req)
    ys = ffn_gathered(xb3, src_tiles, w_gate, w_up, w_down, layer, tile_expert, tile_rows, tm=tm, tf=tf,
                      row_counts=tuple(range(FFN_SUB, tm + 1, FFN_SUB)))
    pos2 = pos.reshape(n // rows, rows, 2).transpose(0, 2, 1).reshape(n // rows, 1, 2 * rows)
    return combine_ln(xf, ys, pos2, route, ln_g, ln_b, alpha=alpha, rows=rows)


def kernel(x, w_in, sb_norm_g, conv_w, gla_w_up, gla_b, gla_norm_g, w_out, ln1_g, ln1_b,
           ffn_w_gate, ffn_w_up, ffn_w_down, moe_w_router, moe_w_gate, moe_w_up, moe_w_down,
           ln2_g, ln2_b):
    batch, seq, d = x.shape
    depth = w_in.shape[0]
    n = batch * seq
    alpha = (2 * depth) ** 0.25

    xf = x.reshape(n, d)
    xb = xf.astype(BF16)
    w_r_all = lax.optimization_barrier(w_in[:, :, OFF_G_R:])
    for layer in range(depth):
        w_r_pad = jnp.pad(w_r_all[layer], ((0, 0), (0, LANES - GLA_RANK))).astype(BF16)
        h, h_r = in_proj(xb, w_in, w_r_pad, layer, tm=1024, tn=1024)
        o_sb = sb_attention(h, sb_norm_g[layer], batch=batch, seq=seq)
        o_cv = gated_conv(h, conv_w[layer], batch=batch, seq=seq)
        w_up_pad = jnp.pad(gla_w_up[layer], ((0, LANES - GLA_RANK), (0, 0))).astype(BF16)
        o_gla = gla(h, h_r, w_up_pad, gla_b[layer], gla_norm_g[layer], batch=batch, seq=seq)
        j = layer // 2
        if layer % 2 == 1:
            xf, xb, route = outproj_ln(o_sb, o_cv, o_gla, w_out, layer, xf, ln1_g[layer], ln1_b[layer],
                                       moe_w_router[j], alpha=alpha, tm=256)
            xf, xb = moe_block(xf, xb, route, moe_w_gate, moe_w_up, moe_w_down, j,
                               ln2_g[layer], ln2_b[layer], alpha=alpha, tm=FFN_TM, tf=FFN_TF, rows=256)
        else:
            xf, xb = outproj_ln(o_sb, o_cv, o_gla, w_out, layer, xf, ln1_g[layer], ln1_b[layer], None,
                                alpha=alpha, tm=256)
            n_tiles = n // FFN_TM
            f = ffn(xb, ffn_w_gate, ffn_w_up, ffn_w_down, j, jnp.zeros((n_tiles,), jnp.int32),
                    jnp.full((n_tiles,), FFN_TM, jnp.int32), tm=FFN_TM, tf=FFN_TF, row_counts=(FFN_TM,))
            xf, xb = residual_ln(xf, f, ln2_g[layer], ln2_b[layer], alpha=alpha, tm=512)
    return xf.reshape(batch, seq, d)
```

```python
import functools

import jax
import jax.numpy as jnp
from jax import lax
from jax.experimental import pallas as pl
from jax.experimental.pallas import tpu as pltpu

F32 = jnp.float32
BF16 = jnp.bfloat16

D_MODEL = 2048
SB_HEADS = 8
SB_DH = 128
SB_WIDTH = SB_HEADS * SB_DH
CONV_WIDTH = 512
CONV_K = 3
GLA_HEADS = 4
GLA_DK = 64
GLA_DV = 128
GLA_KW = GLA_HEADS * GLA_DK
GLA_VW = GLA_HEADS * GLA_DV
GLA_RANK = 16
GLA_TAU = 16.0
GLA_CHUNK = 16
IN_COLS = 3 * SB_WIDTH + 3 * CONV_WIDTH + 2 * GLA_KW + 2 * GLA_VW + GLA_RANK
N_EXPERTS = 8
LN_EPS = 1e-5
RMS_EPS = 1e-6

LANES = 128
VMEM_LIMIT = 56 * 1024 * 1024

OFF_SB_Q, OFF_SB_K, OFF_SB_V = 0, SB_WIDTH, 2 * SB_WIDTH
OFF_CV_B = 3 * SB_WIDTH
OFF_CV_C = OFF_CV_B + CONV_WIDTH
OFF_CV_H = OFF_CV_C + CONV_WIDTH
OFF_G_Q = OFF_CV_H + CONV_WIDTH
OFF_G_K = OFF_G_Q + GLA_KW
OFF_G_V = OFF_G_K + GLA_KW
OFF_G_G = OFF_G_V + GLA_VW
OFF_G_R = OFF_G_G + GLA_VW

SB_EXIT = -90.0


def _params(sem):
    return pltpu.CompilerParams(dimension_semantics=sem, vmem_limit_bytes=VMEM_LIMIT)


def _split_bf16(x):
    hi = x.astype(BF16)
    lo = (x - hi.astype(F32)).astype(BF16)
    return hi, lo


def _dot(a, b):
    return jnp.dot(a, b, preferred_element_type=F32)


def _silu(x):
    return x * (1.0 / (1.0 + jnp.exp(-x)))


def _in_proj_kernel(x_ref, w_ref, wr_ref, h_ref, r_ref):
    x = x_ref[...]
    w_t = w_ref[...].astype(BF16)
    h_ref[...] = lax.dot_general(x, w_t, (((1,), (1,)), ((), ())), preferred_element_type=F32).astype(h_ref.dtype)

    @pl.when(pl.program_id(1) == 0)
    def _():
        r_ref[...] = _dot(x, wr_ref[...]).astype(r_ref.dtype)


def in_proj(xb, w_in_t, w_r_pad, layer, *, tm, tn):
    n, d = xb.shape
    return pl.pallas_call(
        _in_proj_kernel,
        grid=(n // tm, OFF_G_R // tn),
        in_specs=[pl.BlockSpec((tm, d), lambda i, j: (i, 0)),
                  pl.BlockSpec((None, tn, d), lambda i, j: (layer, j, 0)),
                  pl.BlockSpec((d, LANES), lambda i, j: (0, 0))],
        out_specs=[pl.BlockSpec((tm, tn), lambda i, j: (i, j)),
                   pl.BlockSpec((tm, LANES), lambda i, j: (i, 0))],
        out_shape=[jax.ShapeDtypeStruct((n, OFF_G_R), BF16), jax.ShapeDtypeStruct((n, LANES), BF16)],
        compiler_params=_params(("parallel", "arbitrary")),
        name="in_proj",
    )(xb, w_in_t, w_r_pad)


SB_BLK = 64
SB_WIN = 256
SB_GROUP = 16


LOG2E = 1.4426950408889634
SB_EXIT_LOG2 = SB_EXIT * LOG2E


def _minus_later_keys_matrix(n_keys):
    r = lax.broadcasted_iota(jnp.int32, (2 * n_keys, n_keys), 0)
    c = lax.broadcasted_iota(jnp.int32, (2 * n_keys, n_keys), 1)
    return jnp.where((r % n_keys) > c, -1.0, 0.0).astype(BF16)


def _sb_scores(q, k, scale):
    y = lax.dot_general(q, k, (((1,), (1,)), ((), ())), preferred_element_type=F32) * (scale * LOG2E)
    sp = jnp.maximum(y, 0.0) + jnp.log2(1.0 + jnp.exp2(-jnp.abs(y)))
    return y, sp


def _sb_tail(sp_masked, minus_later):
    hi, lo = _split_bf16(sp_masked)
    return _dot(jnp.concatenate([hi, lo], axis=1), minus_later)


def _sb_kernel(q_ref, k_ref, v_ref, g_ref, o_ref, acc_ref, c_ref, *, group, scale):
    minus_later_win = _minus_later_keys_matrix(SB_WIN)
    minus_later_blk = _minus_later_keys_matrix(SB_BLK)
    row = lax.broadcasted_iota(jnp.int32, (SB_BLK, SB_WIN), 0)
    col = lax.broadcasted_iota(jnp.int32, (SB_BLK, SB_WIN), 1)
    causal_full = col < row + (SB_WIN - SB_BLK)
    n_clamped = (SB_WIN - SB_BLK) // SB_BLK
    gain = g_ref[...]

    def q_rows(qi):
        return pl.ds(pl.multiple_of(qi * SB_BLK, SB_BLK), SB_BLK)

    def windows(qis):
        wss = [pl.multiple_of(jnp.maximum(qi - n_clamped, 0) * SB_BLK, SB_BLK) for qi in qis]
        causal = [col < row + (qi * SB_BLK - ws) for qi, ws in zip(qis[:n_clamped], wss)]
        causal += [causal_full] * (len(qis) - n_clamped)
        scores = [_sb_scores(q_ref[q_rows(qi), :], k_ref[pl.ds(ws, SB_WIN), :], scale) for qi, ws in zip(qis, wss)]
        sp_m = [jnp.where(m, sp, 0.0) for m, (_, sp) in zip(causal, scores)]
        tails = [_sb_tail(s, minus_later_win) for s in sp_m]
        c_max = None
        for u, (m, (y, sp), s, tail, ws) in enumerate(zip(causal, scores, sp_m, tails, wss)):
            a = jnp.where(m, jnp.exp2(y - sp + tail), 0.0)
            acc_ref[u] = _dot(a.astype(BF16), v_ref[pl.ds(ws, SB_WIN), :])
            c = tail[:, 0:1] - s[:, 0:1]
            c_ref[u] = c
            if u <= n_clamped:
                c = jnp.where(ws > 0, c, -jnp.inf)
            c_max = c if c_max is None else jnp.maximum(c_max, c)
        return c_max

    def earlier_blocks(qi, u):
        q = q_ref[q_rows(qi), :]

        def body(carry):
            j, _ = carry
            start = pl.multiple_of(j * SB_BLK, SB_BLK)
            y, sp = _sb_scores(q, k_ref[pl.ds(start, SB_BLK), :], scale)
            tail = _sb_tail(sp, minus_later_blk)
            c = c_ref[u]
            a = jnp.exp2(y - sp + tail + c)
            acc_ref[u] += _dot(a.astype(BF16), v_ref[pl.ds(start, SB_BLK), :])
            c_new = c + tail[:, 0:1] - sp[:, 0:1]
            c_ref[u] = c_new
            return j - 1, jnp.max(c_new) > SB_EXIT_LOG2

        def cond(carry):
            j, more = carry
            return jnp.logical_and(j >= 0, more)

        lax.while_loop(cond, body, (jnp.maximum(qi - n_clamped, 0) - 1, jnp.max(c_ref[u]) > SB_EXIT_LOG2))

    def run_group(gi, carry):
        c_max = windows([gi * group + u for u in range(group)])

        @pl.when(jnp.max(c_max) > SB_EXIT_LOG2)
        def _():
            for u in range(group):
                earlier_blocks(gi * group + u, u)

        for u in range(group):
            o = acc_ref[u]
            ms = jnp.mean(o * o, axis=-1, keepdims=True)
            rows = pl.ds(pl.multiple_of((gi * group + u) * SB_BLK, SB_BLK), SB_BLK)
            o_ref[rows, :] = (o * lax.rsqrt(ms + RMS_EPS) * gain).astype(o_ref.dtype)
        return carry

    lax.fori_loop(0, q_ref.shape[0] // (SB_BLK * group), run_group, 0)


def sb_attention(h, norm_g, *, batch, seq):
    n = batch * seq
    qb, kb, vb = OFF_SB_Q // SB_DH, OFF_SB_K // SB_DH, OFF_SB_V // SB_DH
    group = min(SB_GROUP, seq // SB_BLK)
    kern = functools.partial(_sb_kernel, group=group, scale=SB_DH ** -0.5)
    return pl.pallas_call(
        kern,
        grid=(batch, SB_HEADS),
        in_specs=[pl.BlockSpec((seq, SB_DH), lambda b, hh: (b, qb + hh)),
                  pl.BlockSpec((seq, SB_DH), lambda b, hh: (b, kb + hh)),
                  pl.BlockSpec((seq, SB_DH), lambda b, hh: (b, vb + hh)),
                  pl.BlockSpec((1, SB_DH), lambda b, hh: (0, 0))],
        out_specs=pl.BlockSpec((seq, SB_DH), lambda b, hh: (b, hh)),
        out_shape=jax.ShapeDtypeStruct((n, SB_WIDTH), BF16),
        scratch_shapes=[pltpu.VMEM((group, SB_BLK, SB_DH), F32), pltpu.VMEM((group, SB_BLK, 1), F32)],
        compiler_params=_params(("parallel", "parallel")),
        name="sb_attention",
    )(h, h, h, norm_g.reshape(1, SB_DH))


def _conv_kernel(b_ref, c_ref, h_ref, w_ref, o_ref):
    u = c_ref[...].astype(F32) * h_ref[...].astype(F32)
    t = lax.broadcasted_iota(jnp.int32, u.shape, 0)
    u1 = jnp.where(t >= 1, pltpu.roll(u, 1, 0), 0.0)
    u2 = jnp.where(t >= 2, pltpu.roll(u, 2, 0), 0.0)
    w = w_ref[...]
    y = w[0:1, :] * u2 + w[1:2, :] * u1 + w[2:3, :] * u
    o_ref[...] = (b_ref[...].astype(F32) * y).astype(o_ref.dtype)


def gated_conv(h, conv_w, *, batch, seq):
    n = batch * seq
    nb = CONV_WIDTH // LANES
    bb, cb, hb = OFF_CV_B // LANES, OFF_CV_C // LANES, OFF_CV_H // LANES
    return pl.pallas_call(
        _conv_kernel,
        grid=(batch, nb),
        in_specs=[pl.BlockSpec((seq, LANES), lambda b, j: (b, bb + j)),
                  pl.BlockSpec((seq, LANES), lambda b, j: (b, cb + j)),
                  pl.BlockSpec((seq, LANES), lambda b, j: (b, hb + j)),
                  pl.BlockSpec((CONV_K, LANES), lambda b, j: (0, j))],
        out_specs=pl.BlockSpec((seq, LANES), lambda b, j: (b, j)),
        out_shape=jax.ShapeDtypeStruct((n, CONV_WIDTH), BF16),
        compiler_params=_params(("parallel", "parallel")),
        name="gated_conv",
    )(h, h, h, conv_w)


GLA_SEG = 128
GLA_SEGS_PER_STEP = 2


def _gla_kernel(q_ref, k_ref, v_ref, g_ref, r_ref, wup_ref, b_ref, gn_ref, o_ref, s_ref, *, n_seg):
    seg_len, chunk = GLA_SEG, GLA_CHUNK
    n_chunk = seg_len // chunk
    two_dk, two_dv = 2 * GLA_DK, 2 * GLA_DV

    row = lax.broadcasted_iota(jnp.int32, (seg_len, seg_len), 0)
    col = lax.broadcasted_iota(jnp.int32, (seg_len, seg_len), 1)
    same = (row // chunk) == (col // chunk)
    cum_m = jnp.logical_and(same, col <= row).astype(BF16)
    tot_m = same.astype(BF16)
    r2 = lax.broadcasted_iota(jnp.int32, (two_dk, two_dv), 0)
    c2 = lax.broadcasted_iota(jnp.int32, (two_dk, two_dv), 1)
    head_bd = (r2 // GLA_DK) == (c2 // GLA_DV)
    ones_bd = head_bd.astype(BF16)
    bd_f = head_bd.astype(F32)
    band = jnp.where(jnp.logical_and(same, col <= row), row - col, -1)
    lane_chunk = col // chunk

    wup = wup_ref[...]
    bias = b_ref[...]
    gn = gn_ref[...]
    s_ref[...] = jnp.zeros_like(s_ref)

    def rms(x):
        return x * lax.rsqrt(jnp.mean(x * x, axis=-1, keepdims=True) + RMS_EPS)

    def gates(rows):
        u = _dot(r_ref[rows, :], wup) + bias
        la = (jnp.minimum(u, 0.0) - jnp.log1p(jnp.exp(-jnp.abs(u)))) * (1.0 / GLA_TAU)
        hi, lo = _split_bf16(la)
        return _dot(cum_m, hi) + _dot(cum_m, lo), _dot(tot_m, hi) + _dot(tot_m, lo)

    def intra(q, k, v, bcum):
        score = _dot((q * k).astype(BF16), ones_bd)
        attn0 = jnp.where(band == 0, score[:, :GLA_DV], 0.0)
        attn1 = jnp.where(band == 0, score[:, GLA_DV:], 0.0)
        for d in range(1, chunk):
            kd = pltpu.roll(k, d, 0)
            bd = pltpu.roll(bcum, d, 0)
            p = q * kd * jnp.exp(bcum - bd)
            score = _dot(p.astype(BF16), ones_bd)
            attn0 = jnp.where(band == d, score[:, :GLA_DV], attn0)
            attn1 = jnp.where(band == d, score[:, GLA_DV:], attn1)
        return jnp.concatenate([_dot(attn0.astype(BF16), v[:, :GLA_DV]),
                                _dot(attn1.astype(BF16), v[:, GLA_DV:])], axis=1)

    def seg_group(gi, carry):
        all_rows = [pl.ds(pl.multiple_of((gi * GLA_SEGS_PER_STEP + s) * seg_len, seg_len), seg_len)
                    for s in range(GLA_SEGS_PER_STEP)]
        qs = [q_ref[rows, :].astype(F32) * (GLA_DK ** -0.5) for rows in all_rows]
        ks = [k_ref[rows, :].astype(F32) for rows in all_rows]
        vs = [v_ref[rows, :] for rows in all_rows]
        bcums, blasts = zip(*[gates(rows) for rows in all_rows])
        o_accs = [intra(q, k, v, bcum) for q, k, v, bcum in zip(qs, ks, vs, bcums)]

        q_ins = [(q * jnp.exp(bcum)).astype(BF16) for q, bcum in zip(qs, bcums)]
        k_up_ts = [(k * jnp.exp(blast - bcum)).T for k, bcum, blast in zip(ks, bcums, blasts)]
        g_ts = [jnp.exp(blast.T) for blast in blasts]
        kvs = [[_dot(jnp.where(lane_chunk == c, k_up_t, 0.0).astype(BF16), v) for c in range(n_chunk)]
               for k_up_t, v in zip(k_up_ts, vs)]
        state = s_ref[...]
        states = []
        for g_t, kv in zip(g_ts, kvs):
            seg_states = []
            for c in range(n_chunk):
                seg_states.append(state.astype(BF16))
                state = g_t[:, c * chunk:c * chunk + 1] * state + kv[c] * bd_f
            states.append(seg_states)
        s_ref[...] = state

        for rows, o_acc, q_in, seg_states in zip(all_rows, o_accs, q_ins, states):
            o_inter = [_dot(q_in[c * chunk:(c + 1) * chunk, :], seg_states[c]) for c in range(n_chunk)]
            o_acc = o_acc + jnp.concatenate(o_inter, axis=0)
            gate = g_ref[rows, :].astype(F32)
            o0 = rms(o_acc[:, :GLA_DV]) * gn
            o1 = rms(o_acc[:, GLA_DV:]) * gn
            out = jnp.concatenate([o0, o1], axis=1) * _silu(gate)
            o_ref[rows, :] = out.astype(o_ref.dtype)
        return carry

    lax.fori_loop(0, n_seg // GLA_SEGS_PER_STEP, seg_group, 0)


def gla(h, h_r, w_up_pad, gla_b, gla_norm_g, *, batch, seq):
    n = batch * seq
    pairs = GLA_HEADS // 2
    qb, kb = OFF_G_Q // LANES, OFF_G_K // LANES
    vb, gb = OFF_G_V // (2 * GLA_DV), OFF_G_G // (2 * GLA_DV)
    kern = functools.partial(_gla_kernel, n_seg=seq // GLA_SEG)
    return pl.pallas_call(
        kern,
        grid=(batch, pairs),
        in_specs=[pl.BlockSpec((seq, LANES), lambda b, p: (b, qb + p)),
                  pl.BlockSpec((seq, LANES), lambda b, p: (b, kb + p)),
                  pl.BlockSpec((seq, 2 * GLA_DV), lambda b, p: (b, vb + p)),
                  pl.BlockSpec((seq, 2 * GLA_DV), lambda b, p: (b, gb + p)),
                  pl.BlockSpec((seq, LANES), lambda b, p: (b, 0)),
                  pl.BlockSpec((LANES, LANES), lambda b, p: (0, p)),
                  pl.BlockSpec((1, LANES), lambda b, p: (0, p)),
                  pl.BlockSpec((1, GLA_DV), lambda b, p: (0, 0))],
        out_specs=pl.BlockSpec((seq, 2 * GLA_DV), lambda b, p: (b, p)),
        out_shape=jax.ShapeDtypeStruct((n, GLA_VW), BF16),
        scratch_shapes=[pltpu.VMEM((2 * GLA_DK, 2 * GLA_DV), F32)],
        compiler_params=_params(("parallel", "parallel")),
        name="gla",
    )(h, h, h, h, h_r, w_up_pad, gla_b.reshape(1, GLA_KW), gla_norm_g.reshape(1, GLA_DV))


def _layer_norm(y, g, b):
    mu = jnp.mean(y, axis=-1, keepdims=True)
    yc = y - mu
    var = jnp.mean(yc * yc, axis=-1, keepdims=True)
    return yc * lax.rsqrt(var + LN_EPS) * g + b


def _top2_route(logits):
    col = lax.broadcasted_iota(jnp.int32, logits.shape, 1)
    neg = jnp.float32(-jnp.inf)
    l1 = jnp.where(col < N_EXPERTS, logits, neg)
    m1 = jnp.max(l1, axis=-1, keepdims=True)
    i1 = jnp.min(jnp.where(l1 == m1, col, LANES), axis=-1, keepdims=True)
    l2 = jnp.where(col == i1, neg, l1)
    m2 = jnp.max(l2, axis=-1, keepdims=True)
    i2 = jnp.min(jnp.where(l2 == m2, col, LANES), axis=-1, keepdims=True)
    e2 = jnp.exp(m2 - m1)
    w1 = 1.0 / (1.0 + e2)
    w2 = e2 / (1.0 + e2)
    out = jnp.where(col == 0, i1.astype(F32), 0.0) + jnp.where(col == 1, i2.astype(F32), 0.0)
    return out + jnp.where(col == 2, w1, 0.0) + jnp.where(col == 3, w2, 0.0)


def _outproj_ln_kernel(*refs, alpha, routed):
    if routed:
        sb_ref, cv_ref, gl_ref, w_ref, x_ref, g_ref, b_ref, wr_ref, xo_ref, xb_ref, route_ref, wb_ref = refs
    else:
        sb_ref, cv_ref, gl_ref, w_ref, x_ref, g_ref, b_ref, xo_ref, xb_ref, wb_ref = refs

    @pl.when(pl.program_id(0) == 0)
    def _():
        wb_ref[...] = w_ref[...].astype(BF16)

    mix = _dot(sb_ref[...], wb_ref[0:SB_WIDTH, :])
    mix = mix + _dot(cv_ref[...], wb_ref[SB_WIDTH:SB_WIDTH + CONV_WIDTH, :])
    mix = mix + _dot(gl_ref[...], wb_ref[SB_WIDTH + CONV_WIDTH:, :])
    out = _layer_norm(alpha * x_ref[...] + mix, g_ref[...], b_ref[...])
    xo_ref[...] = out
    xb_ref[...] = out.astype(BF16).reshape(xb_ref.shape)
    if routed:
        col = lax.broadcasted_iota(jnp.int32, route_ref.shape, 1)
        logits = jnp.zeros(route_ref.shape, F32)
        for e in range(N_EXPERTS):
            logit_e = jnp.sum(out * wr_ref[e:e + 1, :], axis=-1, keepdims=True)
            logits = jnp.where(col == e, logit_e, logits)
        route_ref[...] = _top2_route(logits)


def _token_major_spec(tm, d, tiled):
    if tiled:
        return pl.BlockSpec((tm, d // LANES, LANES), lambda i: (i, 0, 0))
    return pl.BlockSpec((tm, d), lambda i: (i, 0))


def _token_major_shape(n, d, tiled):
    return (n, d // LANES, LANES) if tiled else (n, d)


def outproj_ln(o_sb, o_cv, o_gla, w_out, layer, x, ln_g, ln_b, w_router, *, alpha, tm):
    n, d = x.shape
    routed = w_router is not None
    kern = functools.partial(_outproj_ln_kernel, alpha=alpha, routed=routed)
    row = lambda i: (i, 0)
    fixed = lambda i: (0, 0)
    in_specs = [pl.BlockSpec((tm, SB_WIDTH), row),
                pl.BlockSpec((tm, CONV_WIDTH), row),
                pl.BlockSpec((tm, GLA_VW), row),
                pl.BlockSpec((None, d, d), lambda i: (layer, 0, 0), pipeline_mode=pl.Buffered(1)),
                pl.BlockSpec((tm, d), row),
                pl.BlockSpec((1, d), fixed),
                pl.BlockSpec((1, d), fixed)]
    args = [o_sb, o_cv, o_gla, w_out, x, ln_g.reshape(1, d), ln_b.reshape(1, d)]
    out_specs = [pl.BlockSpec((tm, d), row), _token_major_spec(tm, d, routed)]
    out_shape = [jax.ShapeDtypeStruct((n, d), F32), jax.ShapeDtypeStruct(_token_major_shape(n, d, routed), BF16)]
    if routed:
        in_specs.append(pl.BlockSpec((N_EXPERTS, d), fixed))
        args.append(w_router.T)
        out_specs.append(pl.BlockSpec((tm, LANES), row))
        out_shape.append(jax.ShapeDtypeStruct((n, LANES), F32))
    return pl.pallas_call(
        kern,
        grid=(n // tm,),
        in_specs=in_specs,
        out_specs=out_specs,
        out_shape=out_shape,
        scratch_shapes=[pltpu.VMEM((d, d), BF16)],
        compiler_params=_params(("arbitrary",)),
        name="outproj_ln",
    )(*args)


def _residual_ln_kernel(x_ref, f_ref, g_ref, b_ref, xo_ref, xb_ref, *, alpha):
    f = f_ref[...].reshape(x_ref.shape).astype(F32)
    out = _layer_norm(alpha * x_ref[...] + f, g_ref[...], b_ref[...])
    xo_ref[...] = out
    xb_ref[...] = out.astype(BF16)


def residual_ln(x, f, ln_g, ln_b, *, alpha, tm):
    n, d = x.shape
    kern = functools.partial(_residual_ln_kernel, alpha=alpha)
    row = lambda i: (i, 0)
    fixed = lambda i: (0, 0)
    return pl.pallas_call(
        kern,
        grid=(n // tm,),
        in_specs=[pl.BlockSpec((tm, d), row), _token_major_spec(tm, d, True),
                  pl.BlockSpec((1, d), fixed), pl.BlockSpec((1, d), fixed)],
        out_specs=[pl.BlockSpec((tm, d), row), pl.BlockSpec((tm, d), row)],
        out_shape=[jax.ShapeDtypeStruct((n, d), F32), jax.ShapeDtypeStruct((n, d), BF16)],
        compiler_params=_params(("parallel",)),
        name="residual_ln",
    )(x, f, ln_g.reshape(1, d), ln_b.reshape(1, d))


FFN_TM = 1024
FFN_TF = 256
FFN_SUB = 128


def _swiglu_chunk(x, wg_ref, wu_ref, wd_ref):
    hg = _dot(x, wg_ref[...].astype(BF16))
    hu = _dot(x, wu_ref[...].astype(BF16))
    hidden = (_silu(hg) * hu).astype(BF16)
    return _dot(hidden, wd_ref[...].astype(BF16))


def _ffn_kernel(te_ref, rows_ref, x_ref, wg_ref, wu_ref, wd_ref, o_ref, acc_ref, *, row_counts):
    i, f = pl.program_id(0), pl.program_id(1)

    @pl.when(f == 0)
    def _():
        acc_ref[...] = jnp.zeros_like(acc_ref)

    for m in row_counts:
        @pl.when(rows_ref[i] == m)
        def _(m=m):
            acc_ref[0:m, :] += _swiglu_chunk(x_ref[0:m, :], wg_ref, wu_ref, wd_ref)

    @pl.when(f == pl.num_programs(1) - 1)
    def _():
        o_ref[...] = acc_ref[...].astype(BF16).reshape(o_ref.shape)


def _ffn_gather_kernel(te_ref, rows_ref, idx_ref, nxt_ref, x_hbm, wg_ref, wu_ref, wd_ref, o_ref,
                       acc_ref, xbuf, x2d, sem, *, row_counts, per_step):
    i, f = pl.program_id(0), pl.program_id(1)
    n_i, n_f = pl.num_programs(0), pl.num_programs(1)
    tm, d = x2d.shape
    n_rows = xbuf.shape[1]
    slot = i % 2

    def row_copy(ref, s, r):
        return pltpu.make_async_copy(x_hbm.at[ref[0, 0, r]], xbuf.at[s, r], sem.at[s])

    def for_all_rows(fn):
        def body(r, c):
            fn(r)
            return c
        lax.fori_loop(0, n_rows, body, 0, unroll=DMA_UNROLL)

    @pl.when(f == 0)
    def _():
        acc_ref[...] = jnp.zeros_like(acc_ref)

        @pl.when(i == 0)
        def _():
            for_all_rows(lambda r: row_copy(idx_ref, slot, r).start())

        @pl.when(jnp.logical_or(i == 0, rows_ref[jnp.maximum(i - 1, 0)] > 0))
        def _():
            for_all_rows(lambda r: row_copy(idx_ref, slot, r).wait())
            x2d[...] = xbuf[slot, 0:tm].reshape(tm, d)

    for m in row_counts:
        @pl.when(rows_ref[i] == m)
        def _(m=m):
            for u in range(per_step):
                row_copy(nxt_ref, 1 - slot, f * per_step + u).start()
            acc_ref[0:m, :] += _swiglu_chunk(x2d[0:m, :], wg_ref, wu_ref, wd_ref)

    @pl.when(f == n_f - 1)
    def _():
        o_ref[...] = acc_ref[...].astype(BF16).reshape(o_ref.shape)

        @pl.when(jnp.logical_and(i == n_i - 1, rows_ref[i] > 0))
        def _():
            for_all_rows(lambda r: row_copy(nxt_ref, 1 - slot, r).wait())


def ffn(xb, w_gate, w_up, w_down, layer, tile_expert, tile_rows, *, tm, tf, row_counts):
    r, d = xb.shape
    d_ff = w_gate.shape[-1]
    n_f = d_ff // tf
    per_expert = w_gate.ndim == 4

    def chunk(i, f, rows):
        return jnp.where(rows[i] > 0, f, n_f - 1)

    def w_col(i, f, te, rows):
        return (layer, te[i], 0, chunk(i, f, rows)) if per_expert else (layer, 0, chunk(i, f, rows))

    def w_row(i, f, te, rows):
        return (layer, te[i], chunk(i, f, rows), 0) if per_expert else (layer, chunk(i, f, rows), 0)

    lead = (None, None) if per_expert else (None,)
    grid_spec = pltpu.PrefetchScalarGridSpec(
        num_scalar_prefetch=2,
        grid=(r // tm, n_f),
        in_specs=[pl.BlockSpec((tm, d), lambda i, f, te, rows: (i, 0)),
                  pl.BlockSpec((*lead, d, tf), w_col),
                  pl.BlockSpec((*lead, d, tf), w_col),
                  pl.BlockSpec((*lead, tf, d), w_row)],
        out_specs=pl.BlockSpec((tm, d // LANES, LANES), lambda i, f, te, rows: (i, 0, 0)),
        scratch_shapes=[pltpu.VMEM((tm, d), F32)],
    )
    return pl.pallas_call(
        functools.partial(_ffn_kernel, row_counts=row_counts),
        grid_spec=grid_spec,
        out_shape=jax.ShapeDtypeStruct((r, d // LANES, LANES), BF16),
        compiler_params=_params(("parallel", "arbitrary")),
        name="ffn",
    )(tile_expert, tile_rows, xb, w_gate, w_up, w_down)


def ffn_gathered(x3, src, w_gate, w_up, w_down, layer, tile_expert, tile_rows, *, tm, tf, row_counts):
    _, s_dim, l_dim = x3.shape
    d = s_dim * l_dim
    n_tiles, _, n_rows = src.shape
    d_ff = w_gate.shape[-1]
    n_f = d_ff // tf

    def chunk(i, f, rows):
        return jnp.where(rows[i] > 0, f, n_f - 1)

    smem = dict(memory_space=pltpu.SMEM)
    grid_spec = pltpu.PrefetchScalarGridSpec(
        num_scalar_prefetch=2,
        grid=(n_tiles, n_f),
        in_specs=[pl.BlockSpec((1, 1, n_rows), lambda i, f, te, rows: (i, 0, 0), **smem),
                  pl.BlockSpec((1, 1, n_rows), lambda i, f, te, rows: (jnp.minimum(i + 1, n_tiles - 1), 0, 0), **smem),
                  pl.BlockSpec(memory_space=pl.ANY),
                  pl.BlockSpec((None, None, d, tf), lambda i, f, te, rows: (layer, te[i], 0, chunk(i, f, rows))),
                  pl.BlockSpec((None, None, d, tf), lambda i, f, te, rows: (layer, te[i], 0, chunk(i, f, rows))),
                  pl.BlockSpec((None, None, tf, d), lambda i, f, te, rows: (layer, te[i], chunk(i, f, rows), 0))],
        out_specs=pl.BlockSpec((tm, s_dim, l_dim), lambda i, f, te, rows: (i, 0, 0)),
        scratch_shapes=[pltpu.VMEM((tm, d), F32),
                        pltpu.VMEM((2, n_rows, s_dim, l_dim), BF16),
                        pltpu.VMEM((tm, d), BF16),
                        pltpu.SemaphoreType.DMA((2,))],
    )
    return pl.pallas_call(
        functools.partial(_ffn_gather_kernel, row_counts=row_counts, per_step=n_rows // n_f),
        grid_spec=grid_spec,
        out_shape=jax.ShapeDtypeStruct((n_tiles * tm, s_dim, l_dim), BF16),
        compiler_params=_params(("arbitrary", "arbitrary")),
        name="ffn_gathered",
    )(tile_expert, tile_rows, src, src, x3, w_gate, w_up, w_down)


DMA_UNROLL = 8


def _step_and_next_specs(n_steps, width):
    cur = pl.BlockSpec((1, 1, width), lambda i: (i, 0, 0), memory_space=pltpu.SMEM)
    nxt = pl.BlockSpec((1, 1, width), lambda i: (jnp.minimum(i + 1, n_steps - 1), 0, 0), memory_space=pltpu.SMEM)
    return cur, nxt


def _combine_ln_kernel(pos_ref, nxt_ref, w_ref, x_ref, y_hbm, g_ref, b_ref, xo_ref, xb_ref,
                       buf, sem, *, rows, alpha):
    i = pl.program_id(0)
    slot = i % 2

    def row_copy(ref, s, k, r):
        return pltpu.make_async_copy(y_hbm.at[ref[0, 0, k * rows + r]], buf.at[s, k, r], sem.at[s, k])

    def issue(ref, s):
        def body(r, c):
            row_copy(ref, s, 0, r).start()
            row_copy(ref, s, 1, r).start()
            return c
        lax.fori_loop(0, rows, body, 0, unroll=DMA_UNROLL)

    @pl.when(i == 0)
    def _():
        issue(pos_ref, slot)

    @pl.when(i + 1 < pl.num_programs(0))
    def _():
        issue(nxt_ref, 1 - slot)

    def wait(r, c):
        row_copy(pos_ref, slot, 0, r).wait()
        row_copy(pos_ref, slot, 1, r).wait()
        return c

    lax.fori_loop(0, rows, wait, 0, unroll=DMA_UNROLL)
    w = w_ref[...]
    ya = buf[slot, 0].reshape(x_ref.shape).astype(F32)
    yb = buf[slot, 1].reshape(x_ref.shape).astype(F32)
    y = w[:, 2:3] * ya + w[:, 3:4] * yb
    out = _layer_norm(alpha * x_ref[...] + y, g_ref[...], b_ref[...])
    xo_ref[...] = out
    xb_ref[...] = out.astype(BF16)


def combine_ln(x, y, pos, route, ln_g, ln_b, *, alpha, rows):
    n, d = x.shape
    n_steps = n // rows
    kern = functools.partial(_combine_ln_kernel, rows=rows, alpha=alpha)
    row = lambda i: (i, 0)
    fixed = lambda i: (0, 0)
    return pl.pallas_call(
        kern,
        grid=(n_steps,),
        in_specs=[*_step_and_next_specs(n_steps, 2 * rows),
                  pl.BlockSpec((rows, LANES), row),
                  pl.BlockSpec((rows, d), row),
                  pl.BlockSpec(memory_space=pl.ANY),
                  pl.BlockSpec((1, d), fixed),
                  pl.BlockSpec((1, d), fixed)],
        out_specs=[pl.BlockSpec((rows, d), row), pl.BlockSpec((rows, d), row)],
        out_shape=[jax.ShapeDtypeStruct((n, d), F32), jax.ShapeDtypeStruct((n, d), BF16)],
        scratch_shapes=[pltpu.VMEM((2, 2, rows, d // LANES, LANES), BF16), pltpu.SemaphoreType.DMA((2, 2))],
        compiler_params=_params(("arbitrary",)),
        name="combine_ln",
    )(pos, pos, route, x, y, ln_g.reshape(1, d), ln_b.reshape(1, d))


def moe_block(xf, xb3, route, w_gate, w_up, w_down, layer, ln_g, ln_b, *, alpha, tm, tf, rows):
    n, d = xf.shape
    n_e = w_gate.shape[1]
    ef = route[:, 0:2].astype(jnp.int32).reshape(-1)
    onehot = (ef[:, None] == jnp.arange(n_e, dtype=jnp.int32)[None, :]).astype(jnp.int32)
    csum = jnp.cumsum(onehot, axis=0)
    rank = jnp.take_along_axis(csum, ef[:, None], axis=1)[:, 0] - 1
    counts = csum[-1]
    padded = ((counts + tm - 1) // tm) * tm
    ends = jnp.cumsum(padded)
    pos = (ends - padded)[ef] + rank
    r_rows = 2 * n + n_e * tm
    src = jnp.zeros((r_rows,), jnp.int32).at[pos].set(jnp.arange(2 * n, dtype=jnp.int32) // 2)
    tile_start = jnp.arange(r_rows // tm, dtype=jnp.int32) * tm
    te_raw = jnp.minimum(jnp.searchsorted(ends, tile_start, side="right").astype(jnp.int32), n_e - 1)
    last_e = te_raw[jnp.maximum(ends[-1] // tm - 1, 0)]
    tile_expert = jnp.where(tile_start < ends[-1], te_raw, last_e)
    counts_sub = ((counts + FFN_SUB - 1) // FFN_SUB) * FFN_SUB
    tile_rows = jnp.clip(((ends - padded) + counts_sub)[te_raw] - tile_start, 0, tm)

    n_f = w_gate.shape[-1] // tf
    n_req = -(-tm // n_f) * n_f
    src_tiles = jnp.pad(src.reshape(r_rows // tm, tm), ((0, 0), (0, n_req - tm))).reshape(r_rows // tm, 1, n_req)
    ys = ffn_gathered(xb3, src_tiles, w_gate, w_up, w_down, layer, tile_expert, tile_rows, tm=tm, tf=tf,
                      row_counts=tuple(range(FFN_SUB, tm + 1, FFN_SUB)))
    pos2 = pos.reshape(n // rows, rows, 2).transpose(0, 2, 1).reshape(n // rows, 1, 2 * rows)
    return combine_ln(xf, ys, pos2, route, ln_g, ln_b, alpha=alpha, rows=rows)


def kernel(x, w_in, sb_norm_g, conv_w, gla_w_up, gla_b, gla_norm_g, w_out, ln1_g, ln1_b,
           ffn_w_gate, ffn_w_up, ffn_w_down, moe_w_router, moe_w_gate, moe_w_up, moe_w_down,
           ln2_g, ln2_b):
    batch, seq, d = x.shape
    depth = w_in.shape[0]
    n = batch * seq
    alpha = (2 * depth) ** 0.25

    xf = x.reshape(n, d)
    xb = xf.astype(BF16)
    w_r_all = lax.optimization_barrier(w_in[:, :, OFF_G_R:])
    w_in_t = jnp.swapaxes(w_in, 1, 2)
    for layer in range(depth):
        w_r_pad = jnp.pad(w_r_all[layer], ((0, 0), (0, LANES - GLA_RANK))).astype(BF16)
        h, h_r = in_proj(xb, w_in_t, w_r_pad, layer, tm=1024, tn=1024)
        o_sb = sb_attention(h, sb_norm_g[layer], batch=batch, seq=seq)
        o_cv = gated_conv(h, conv_w[layer], batch=batch, seq=seq)
        w_up_pad = jnp.pad(gla_w_up[layer], ((0, LANES - GLA_RANK), (0, 0))).astype(BF16)
        o_gla = gla(h, h_r, w_up_pad, gla_b[layer], gla_norm_g[layer], batch=batch, seq=seq)
        j = layer // 2
        if layer % 2 == 1:
            xf, xb, route = outproj_ln(o_sb, o_cv, o_gla, w_out, layer, xf, ln1_g[layer], ln1_b[layer],
                                       moe_w_router[j], alpha=alpha, tm=256)
            xf, xb = moe_block(xf, xb, route, moe_w_gate, moe_w_up, moe_w_down, j,
                               ln2_g[layer], ln2_b[layer], alpha=alpha, tm=FFN_TM, tf=FFN_TF, rows=256)
        else:
            xf, xb = outproj_ln(o_sb, o_cv, o_gla, w_out, layer, xf, ln1_g[layer], ln1_b[layer], None,
                                alpha=alpha, tm=256)
            n_tiles = n // FFN_TM
            f = ffn(xb, ffn_w_gate, ffn_w_up, ffn_w_down, j, jnp.zeros((n_tiles,), jnp.int32),
                    jnp.full((n_tiles,), FFN_TM, jnp.int32), tm=FFN_TM, tf=FFN_TF, row_counts=(FFN_TM,))
            xf, xb = residual_ln(xf, f, ln2_g[layer], ln2_b[layer], alpha=alpha, tm=512)
    return xf.reshape(batch, seq, d)
```

```python
import functools

import jax
import jax.numpy as jnp
from jax import lax
from jax.experimental import pallas as pl
from jax.experimental.pallas import tpu as pltpu

F32 = jnp.float32
BF16 = jnp.bfloat16

D_MODEL = 2048
SB_HEADS = 8
SB_DH = 128
SB_WIDTH = SB_HEADS * SB_DH
CONV_WIDTH = 512
CONV_K = 3
GLA_HEADS = 4
GLA_DK = 64
GLA_DV = 128
GLA_KW = GLA_HEADS * GLA_DK
GLA_VW = GLA_HEADS * GLA_DV
GLA_RANK = 16
GLA_TAU = 16.0
GLA_CHUNK = 16
IN_COLS = 3 * SB_WIDTH + 3 * CONV_WIDTH + 2 * GLA_KW + 2 * GLA_VW + GLA_RANK
N_EXPERTS = 8
LN_EPS = 1e-5
RMS_EPS = 1e-6

LANES = 128
VMEM_LIMIT = 56 * 1024 * 1024

OFF_SB_Q, OFF_SB_K, OFF_SB_V = 0, SB_WIDTH, 2 * SB_WIDTH
OFF_CV_B = 3 * SB_WIDTH
OFF_CV_C = OFF_CV_B + CONV_WIDTH
OFF_CV_H = OFF_CV_C + CONV_WIDTH
OFF_G_Q = OFF_CV_H + CONV_WIDTH
OFF_G_K = OFF_G_Q + GLA_KW
OFF_G_V = OFF_G_K + GLA_KW
OFF_G_G = OFF_G_V + GLA_VW
OFF_G_R = OFF_G_G + GLA_VW

SB_EXIT = -90.0

IN_PROJ_TM = 1024
IN_PROJ_TN = 1024
ROW_TILE = 256
RESIDUAL_TM = 512


def _params(sem):
    return pltpu.CompilerParams(dimension_semantics=sem, vmem_limit_bytes=VMEM_LIMIT)


def _split_bf16(x):
    hi = x.astype(BF16)
    lo = (x - hi.astype(F32)).astype(BF16)
    return hi, lo


def _dot(a, b):
    return jnp.dot(a, b, preferred_element_type=F32)


def _silu(x):
    return x * (1.0 / (1.0 + jnp.exp(-x)))


def _in_proj_kernel(x_ref, w_ref, wr_ref, h_ref, r_ref):
    x = x_ref[...]
    w_t = w_ref[...].astype(BF16)
    h_ref[...] = lax.dot_general(x, w_t, (((1,), (1,)), ((), ())), preferred_element_type=F32).astype(h_ref.dtype)

    @pl.when(pl.program_id(1) == 0)
    def _():
        r_ref[...] = _dot(x, wr_ref[...]).astype(r_ref.dtype)


def in_proj(xb, w_in_t, w_r_pad, layer, *, tm, tn):
    n, d = xb.shape
    return pl.pallas_call(
        _in_proj_kernel,
        grid=(n // tm, OFF_G_R // tn),
        in_specs=[pl.BlockSpec((tm, d), lambda i, j: (i, 0)),
                  pl.BlockSpec((None, tn, d), lambda i, j: (layer, j, 0)),
                  pl.BlockSpec((d, LANES), lambda i, j: (0, 0))],
        out_specs=[pl.BlockSpec((tm, tn), lambda i, j: (i, j)),
                   pl.BlockSpec((tm, LANES), lambda i, j: (i, 0))],
        out_shape=[jax.ShapeDtypeStruct((n, OFF_G_R), BF16), jax.ShapeDtypeStruct((n, LANES), BF16)],
        compiler_params=_params(("parallel", "arbitrary")),
        name="in_proj",
    )(xb, w_in_t, w_r_pad)


SB_BLK = 64
SB_WIN = 256
SB_GROUP = 16


LOG2E = 1.4426950408889634
SB_EXIT_LOG2 = SB_EXIT * LOG2E


def _minus_later_keys_matrix(n_keys):
    r = lax.broadcasted_iota(jnp.int32, (2 * n_keys, n_keys), 0)
    c = lax.broadcasted_iota(jnp.int32, (2 * n_keys, n_keys), 1)
    return jnp.where((r % n_keys) > c, -1.0, 0.0).astype(BF16)


def _sb_scores(q, k, scale):
    y = lax.dot_general(q, k, (((1,), (1,)), ((), ())), preferred_element_type=F32) * (scale * LOG2E)
    sp = jnp.maximum(y, 0.0) + jnp.log2(1.0 + jnp.exp2(-jnp.abs(y)))
    return y, sp


def _sb_tail(sp_masked, minus_later):
    hi, lo = _split_bf16(sp_masked)
    return _dot(jnp.concatenate([hi, lo], axis=1), minus_later)


def _sb_kernel(q_ref, k_ref, v_ref, g_ref, o_ref, acc_ref, c_ref, *, group, scale):
    minus_later_win = _minus_later_keys_matrix(SB_WIN)
    minus_later_blk = _minus_later_keys_matrix(SB_BLK)
    row = lax.broadcasted_iota(jnp.int32, (SB_BLK, SB_WIN), 0)
    col = lax.broadcasted_iota(jnp.int32, (SB_BLK, SB_WIN), 1)
    causal_full = col < row + (SB_WIN - SB_BLK)
    n_clamped = (SB_WIN - SB_BLK) // SB_BLK
    gain = g_ref[...]

    def q_rows(qi):
        return pl.ds(pl.multiple_of(qi * SB_BLK, SB_BLK), SB_BLK)

    def windows(qis):
        wss = [pl.multiple_of(jnp.maximum(qi - n_clamped, 0) * SB_BLK, SB_BLK) for qi in qis]
        causal = [col < row + (qi * SB_BLK - ws) for qi, ws in zip(qis[:n_clamped], wss)]
        causal += [causal_full] * (len(qis) - n_clamped)
        scores = [_sb_scores(q_ref[q_rows(qi), :], k_ref[pl.ds(ws, SB_WIN), :], scale) for qi, ws in zip(qis, wss)]
        sp_m = [jnp.where(m, sp, 0.0) for m, (_, sp) in zip(causal, scores)]
        tails = [_sb_tail(s, minus_later_win) for s in sp_m]
        c_max = None
        for u, (m, (y, sp), s, tail, ws) in enumerate(zip(causal, scores, sp_m, tails, wss)):
            a = jnp.where(m, jnp.exp2(y - sp + tail), 0.0)
            acc_ref[u] = _dot(a.astype(BF16), v_ref[pl.ds(ws, SB_WIN), :])
            c = tail[:, 0:1] - s[:, 0:1]
            c_ref[u] = c
            if u <= n_clamped:
                c = jnp.where(ws > 0, c, -jnp.inf)
            c_max = c if c_max is None else jnp.maximum(c_max, c)
        return c_max

    def earlier_blocks(qi, u):
        q = q_ref[q_rows(qi), :]

        def body(carry):
            j, _ = carry
            start = pl.multiple_of(j * SB_BLK, SB_BLK)
            y, sp = _sb_scores(q, k_ref[pl.ds(start, SB_BLK), :], scale)
            tail = _sb_tail(sp, minus_later_blk)
            c = c_ref[u]
            a = jnp.exp2(y - sp + tail + c)
            acc_ref[u] += _dot(a.astype(BF16), v_ref[pl.ds(start, SB_BLK), :])
            c_new = c + tail[:, 0:1] - sp[:, 0:1]
            c_ref[u] = c_new
            return j - 1, jnp.max(c_new) > SB_EXIT_LOG2

        def cond(carry):
            j, more = carry
            return jnp.logical_and(j >= 0, more)

        lax.while_loop(cond, body, (jnp.maximum(qi - n_clamped, 0) - 1, jnp.max(c_ref[u]) > SB_EXIT_LOG2))

    def run_group(gi, carry):
        c_max = windows([gi * group + u for u in range(group)])

        @pl.when(jnp.max(c_max) > SB_EXIT_LOG2)
        def _():
            for u in range(group):
                earlier_blocks(gi * group + u, u)

        for u in range(group):
            o = acc_ref[u]
            ms = jnp.mean(o * o, axis=-1, keepdims=True)
            rows = pl.ds(pl.multiple_of((gi * group + u) * SB_BLK, SB_BLK), SB_BLK)
            o_ref[rows, :] = (o * lax.rsqrt(ms + RMS_EPS) * gain).astype(o_ref.dtype)
        return carry

    lax.fori_loop(0, q_ref.shape[0] // (SB_BLK * group), run_group, 0)


def sb_attention(h, norm_g, *, batch, seq):
    n = batch * seq
    qb, kb, vb = OFF_SB_Q // SB_DH, OFF_SB_K // SB_DH, OFF_SB_V // SB_DH
    group = min(SB_GROUP, seq // SB_BLK)
    kern = functools.partial(_sb_kernel, group=group, scale=SB_DH ** -0.5)
    return pl.pallas_call(
        kern,
        grid=(batch, SB_HEADS),
        in_specs=[pl.BlockSpec((seq, SB_DH), lambda b, hh: (b, qb + hh)),
                  pl.BlockSpec((seq, SB_DH), lambda b, hh: (b, kb + hh)),
                  pl.BlockSpec((seq, SB_DH), lambda b, hh: (b, vb + hh)),
                  pl.BlockSpec((1, SB_DH), lambda b, hh: (0, 0))],
        out_specs=pl.BlockSpec((seq, SB_DH), lambda b, hh: (b, hh)),
        out_shape=jax.ShapeDtypeStruct((n, SB_WIDTH), BF16),
        scratch_shapes=[pltpu.VMEM((group, SB_BLK, SB_DH), F32), pltpu.VMEM((group, SB_BLK, 1), F32)],
        compiler_params=_params(("parallel", "parallel")),
        name="sb_attention",
    )(h, h, h, norm_g.reshape(1, SB_DH))


def _conv_kernel(b_ref, c_ref, h_ref, w_ref, o_ref):
    u = c_ref[...].astype(F32) * h_ref[...].astype(F32)
    t = lax.broadcasted_iota(jnp.int32, u.shape, 0)
    u1 = jnp.where(t >= 1, pltpu.roll(u, 1, 0), 0.0)
    u2 = jnp.where(t >= 2, pltpu.roll(u, 2, 0), 0.0)
    w = w_ref[...]
    y = w[0:1, :] * u2 + w[1:2, :] * u1 + w[2:3, :] * u
    o_ref[...] = (b_ref[...].astype(F32) * y).astype(o_ref.dtype)


def gated_conv(h, conv_w, *, batch, seq):
    n = batch * seq
    nb = CONV_WIDTH // LANES
    bb, cb, hb = OFF_CV_B // LANES, OFF_CV_C // LANES, OFF_CV_H // LANES
    return pl.pallas_call(
        _conv_kernel,
        grid=(batch, nb),
        in_specs=[pl.BlockSpec((seq, LANES), lambda b, j: (b, bb + j)),
                  pl.BlockSpec((seq, LANES), lambda b, j: (b, cb + j)),
                  pl.BlockSpec((seq, LANES), lambda b, j: (b, hb + j)),
                  pl.BlockSpec((CONV_K, LANES), lambda b, j: (0, j))],
        out_specs=pl.BlockSpec((seq, LANES), lambda b, j: (b, j)),
        out_shape=jax.ShapeDtypeStruct((n, CONV_WIDTH), BF16),
        compiler_params=_params(("parallel", "parallel")),
        name="gated_conv",
    )(h, h, h, conv_w)


GLA_SEG = 128
GLA_SEGS_PER_STEP = 4


def _gla_kernel(q_ref, k_ref, v_ref, g_ref, r_ref, wup_ref, b_ref, gn_ref, o_ref, s_ref, *, n_seg):
    seg_len, chunk = GLA_SEG, GLA_CHUNK
    n_chunk = seg_len // chunk
    two_dk, two_dv = 2 * GLA_DK, 2 * GLA_DV

    row = lax.broadcasted_iota(jnp.int32, (seg_len, seg_len), 0)
    col = lax.broadcasted_iota(jnp.int32, (seg_len, seg_len), 1)
    same = (row // chunk) == (col // chunk)
    cum_m = jnp.logical_and(same, col <= row).astype(BF16)
    tot_m = same.astype(BF16)
    r2 = lax.broadcasted_iota(jnp.int32, (two_dk, two_dv), 0)
    c2 = lax.broadcasted_iota(jnp.int32, (two_dk, two_dv), 1)
    head_bd = (r2 // GLA_DK) == (c2 // GLA_DV)
    ones_bd = head_bd.astype(BF16)
    bd_f = head_bd.astype(F32)
    band = jnp.where(jnp.logical_and(same, col <= row), row - col, -1)
    lane_chunk = col // chunk

    wup = wup_ref[...]
    bias = b_ref[...]
    gn = gn_ref[...]
    s_ref[...] = jnp.zeros_like(s_ref)

    def rms(x):
        return x * lax.rsqrt(jnp.mean(x * x, axis=-1, keepdims=True) + RMS_EPS)

    def gates(rows):
        u = _dot(r_ref[rows, :], wup) + bias
        la = (jnp.minimum(u, 0.0) - jnp.log1p(jnp.exp(-jnp.abs(u)))) * (1.0 / GLA_TAU)
        hi, lo = _split_bf16(la)
        return _dot(cum_m, hi) + _dot(cum_m, lo), _dot(tot_m, hi) + _dot(tot_m, lo)

    def intra(q, k, v, bcum):
        score = _dot((q * k).astype(BF16), ones_bd)
        attn0 = jnp.where(band == 0, score[:, :GLA_DV], 0.0)
        attn1 = jnp.where(band == 0, score[:, GLA_DV:], 0.0)
        for d in range(1, chunk):
            kd = pltpu.roll(k, d, 0)
            bd = pltpu.roll(bcum, d, 0)
            p = q * kd * jnp.exp(bcum - bd)
            score = _dot(p.astype(BF16), ones_bd)
            attn0 = jnp.where(band == d, score[:, :GLA_DV], attn0)
            attn1 = jnp.where(band == d, score[:, GLA_DV:], attn1)
        return jnp.concatenate([_dot(attn0.astype(BF16), v[:, :GLA_DV]),
                                _dot(attn1.astype(BF16), v[:, GLA_DV:])], axis=1)

    def seg_group(gi, carry):
        all_rows = [pl.ds(pl.multiple_of((gi * GLA_SEGS_PER_STEP + s) * seg_len, seg_len), seg_len)
                    for s in range(GLA_SEGS_PER_STEP)]
        qs = [q_ref[rows, :].astype(F32) * (GLA_DK ** -0.5) for rows in all_rows]
        ks = [k_ref[rows, :].astype(F32) for rows in all_rows]
        vs = [v_ref[rows, :] for rows in all_rows]
        bcums, blasts = zip(*[gates(rows) for rows in all_rows])
        o_accs = [intra(q, k, v, bcum) for q, k, v, bcum in zip(qs, ks, vs, bcums)]

        q_ins = [(q * jnp.exp(bcum)).astype(BF16) for q, bcum in zip(qs, bcums)]
        k_up_ts = [(k * jnp.exp(blast - bcum)).T for k, bcum, blast in zip(ks, bcums, blasts)]
        g_ts = [jnp.exp(blast.T) for blast in blasts]
        kvs = [[_dot(jnp.where(lane_chunk == c, k_up_t, 0.0).astype(BF16), v) for c in range(n_chunk)]
               for k_up_t, v in zip(k_up_ts, vs)]
        state = s_ref[...]
        states = []
        for g_t, kv in zip(g_ts, kvs):
            seg_states = []
            for c in range(n_chunk):
                seg_states.append(state.astype(BF16))
                state = g_t[:, c * chunk:c * chunk + 1] * state + kv[c] * bd_f
            states.append(seg_states)
        s_ref[...] = state

        for rows, o_acc, q_in, seg_states in zip(all_rows, o_accs, q_ins, states):
            o_inter = [_dot(q_in[c * chunk:(c + 1) * chunk, :], seg_states[c]) for c in range(n_chunk)]
            o_acc = o_acc + jnp.concatenate(o_inter, axis=0)
            gate = g_ref[rows, :].astype(F32)
            o0 = rms(o_acc[:, :GLA_DV]) * gn
            o1 = rms(o_acc[:, GLA_DV:]) * gn
            out = jnp.concatenate([o0, o1], axis=1) * _silu(gate)
            o_ref[rows, :] = out.astype(o_ref.dtype)
        return carry

    lax.fori_loop(0, n_seg // GLA_SEGS_PER_STEP, seg_group, 0)


def gla(h, h_r, w_up_pad, gla_b, gla_norm_g, *, batch, seq):
    n = batch * seq
    pairs = GLA_HEADS // 2
    qb, kb = OFF_G_Q // LANES, OFF_G_K // LANES
    vb, gb = OFF_G_V // (2 * GLA_DV), OFF_G_G // (2 * GLA_DV)
    kern = functools.partial(_gla_kernel, n_seg=seq // GLA_SEG)
    return pl.pallas_call(
        kern,
        grid=(batch, pairs),
        in_specs=[pl.BlockSpec((seq, LANES), lambda b, p: (b, qb + p)),
                  pl.BlockSpec((seq, LANES), lambda b, p: (b, kb + p)),
                  pl.BlockSpec((seq, 2 * GLA_DV), lambda b, p: (b, vb + p)),
                  pl.BlockSpec((seq, 2 * GLA_DV), lambda b, p: (b, gb + p)),
                  pl.BlockSpec((seq, LANES), lambda b, p: (b, 0)),
                  pl.BlockSpec((LANES, LANES), lambda b, p: (0, p)),
                  pl.BlockSpec((1, LANES), lambda b, p: (0, p)),
                  pl.BlockSpec((1, GLA_DV), lambda b, p: (0, 0))],
        out_specs=pl.BlockSpec((seq, 2 * GLA_DV), lambda b, p: (b, p)),
        out_shape=jax.ShapeDtypeStruct((n, GLA_VW), BF16),
        scratch_shapes=[pltpu.VMEM((2 * GLA_DK, 2 * GLA_DV), F32)],
        compiler_params=_params(("parallel", "parallel")),
        name="gla",
    )(h, h, h, h, h_r, w_up_pad, gla_b.reshape(1, GLA_KW), gla_norm_g.reshape(1, GLA_DV))


def _layer_norm(y, g, b):
    mu = jnp.mean(y, axis=-1, keepdims=True)
    yc = y - mu
    var = jnp.mean(yc * yc, axis=-1, keepdims=True)
    return yc * lax.rsqrt(var + LN_EPS) * g + b


def _top2_route(logits):
    col = lax.broadcasted_iota(jnp.int32, logits.shape, 1)
    neg = jnp.float32(-jnp.inf)
    l1 = jnp.where(col < N_EXPERTS, logits, neg)
    m1 = jnp.max(l1, axis=-1, keepdims=True)
    i1 = jnp.min(jnp.where(l1 == m1, col, LANES), axis=-1, keepdims=True)
    l2 = jnp.where(col == i1, neg, l1)
    m2 = jnp.max(l2, axis=-1, keepdims=True)
    i2 = jnp.min(jnp.where(l2 == m2, col, LANES), axis=-1, keepdims=True)
    e2 = jnp.exp(m2 - m1)
    w1 = 1.0 / (1.0 + e2)
    w2 = e2 / (1.0 + e2)
    out = jnp.where(col == 0, i1.astype(F32), 0.0) + jnp.where(col == 1, i2.astype(F32), 0.0)
    return out + jnp.where(col == 2, w1, 0.0) + jnp.where(col == 3, w2, 0.0)


def _outproj_ln_kernel(*refs, alpha, routed):
    if routed:
        sb_ref, cv_ref, gl_ref, w_ref, x_ref, g_ref, b_ref, wr_ref, xo_ref, xb_ref, route_ref, wb_ref = refs
    else:
        sb_ref, cv_ref, gl_ref, w_ref, x_ref, g_ref, b_ref, xo_ref, xb_ref, wb_ref = refs

    @pl.when(pl.program_id(0) == 0)
    def _():
        wb_ref[...] = w_ref[...].astype(BF16)

    mix = _dot(sb_ref[...], wb_ref[0:SB_WIDTH, :])
    mix = mix + _dot(cv_ref[...], wb_ref[SB_WIDTH:SB_WIDTH + CONV_WIDTH, :])
    mix = mix + _dot(gl_ref[...], wb_ref[SB_WIDTH + CONV_WIDTH:, :])
    out = _layer_norm(alpha * x_ref[...] + mix, g_ref[...], b_ref[...])
    xo_ref[...] = out
    xb_ref[...] = out.astype(BF16).reshape(xb_ref.shape)
    if routed:
        col = lax.broadcasted_iota(jnp.int32, route_ref.shape, 1)
        logits = jnp.zeros(route_ref.shape, F32)
        for e in range(N_EXPERTS):
            logit_e = jnp.sum(out * wr_ref[e:e + 1, :], axis=-1, keepdims=True)
            logits = jnp.where(col == e, logit_e, logits)
        route_ref[...] = _top2_route(logits)


def _token_major_spec(tm, d, tiled):
    if tiled:
        return pl.BlockSpec((tm, d // LANES, LANES), lambda i: (i, 0, 0))
    return pl.BlockSpec((tm, d), lambda i: (i, 0))


def _token_major_shape(n, d, tiled):
    return (n, d // LANES, LANES) if tiled else (n, d)


def outproj_ln(o_sb, o_cv, o_gla, w_out, layer, x, ln_g, ln_b, w_router, *, alpha, tm):
    n, d = x.shape
    routed = w_router is not None
    kern = functools.partial(_outproj_ln_kernel, alpha=alpha, routed=routed)
    row = lambda i: (i, 0)
    fixed = lambda i: (0, 0)
    in_specs = [pl.BlockSpec((tm, SB_WIDTH), row),
                pl.BlockSpec((tm, CONV_WIDTH), row),
                pl.BlockSpec((tm, GLA_VW), row),
                pl.BlockSpec((None, d, d), lambda i: (layer, 0, 0), pipeline_mode=pl.Buffered(1)),
                pl.BlockSpec((tm, d), row),
                pl.BlockSpec((1, d), fixed),
                pl.BlockSpec((1, d), fixed)]
    args = [o_sb, o_cv, o_gla, w_out, x, ln_g.reshape(1, d), ln_b.reshape(1, d)]
    out_specs = [pl.BlockSpec((tm, d), row), _token_major_spec(tm, d, routed)]
    out_shape = [jax.ShapeDtypeStruct((n, d), F32), jax.ShapeDtypeStruct(_token_major_shape(n, d, routed), BF16)]
    if routed:
        in_specs.append(pl.BlockSpec((N_EXPERTS, d), fixed))
        args.append(w_router.T)
        out_specs.append(pl.BlockSpec((tm, LANES), row))
        out_shape.append(jax.ShapeDtypeStruct((n, LANES), F32))
    return pl.pallas_call(
        kern,
        grid=(n // tm,),
        in_specs=in_specs,
        out_specs=out_specs,
        out_shape=out_shape,
        scratch_shapes=[pltpu.VMEM((d, d), BF16)],
        compiler_params=_params(("arbitrary",)),
        name="outproj_ln",
    )(*args)


def _residual_ln_kernel(x_ref, f_ref, g_ref, b_ref, xo_ref, xb_ref, *, alpha):
    f = f_ref[...].reshape(x_ref.shape).astype(F32)
    out = _layer_norm(alpha * x_ref[...] + f, g_ref[...], b_ref[...])
    xo_ref[...] = out
    xb_ref[...] = out.astype(BF16)


def residual_ln(x, f, ln_g, ln_b, *, alpha, tm):
    n, d = x.shape
    kern = functools.partial(_residual_ln_kernel, alpha=alpha)
    row = lambda i: (i, 0)
    fixed = lambda i: (0, 0)
    return pl.pallas_call(
        kern,
        grid=(n // tm,),
        in_specs=[pl.BlockSpec((tm, d), row), _token_major_spec(tm, d, True),
                  pl.BlockSpec((1, d), fixed), pl.BlockSpec((1, d), fixed)],
        out_specs=[pl.BlockSpec((tm, d), row), pl.BlockSpec((tm, d), row)],
        out_shape=[jax.ShapeDtypeStruct((n, d), F32), jax.ShapeDtypeStruct((n, d), BF16)],
        compiler_params=_params(("parallel",)),
        name="residual_ln",
    )(x, f, ln_g.reshape(1, d), ln_b.reshape(1, d))


FFN_TM = 1024
FFN_TF = 256
FFN_SUB = 128


def _swiglu_chunk(x, wg_ref, wu_ref, wd_ref):
    hg = _dot(x, wg_ref[...].astype(BF16))
    hu = _dot(x, wu_ref[...].astype(BF16))
    hidden = (_silu(hg) * hu).astype(BF16)
    return _dot(hidden, wd_ref[...].astype(BF16))


def _ffn_kernel(te_ref, rows_ref, x_ref, wg_ref, wu_ref, wd_ref, o_ref, acc_ref, *, row_counts):
    i, f = pl.program_id(0), pl.program_id(1)

    @pl.when(f == 0)
    def _():
        acc_ref[...] = jnp.zeros_like(acc_ref)

    for m in row_counts:
        @pl.when(rows_ref[i] == m)
        def _(m=m):
            acc_ref[0:m, :] += _swiglu_chunk(x_ref[0:m, :], wg_ref, wu_ref, wd_ref)

    @pl.when(f == pl.num_programs(1) - 1)
    def _():
        o_ref[...] = acc_ref[...].astype(BF16).reshape(o_ref.shape)


def _ffn_gather_kernel(te_ref, rows_ref, idx_ref, nxt_ref, x_hbm, wg_ref, wu_ref, wd_ref, o_ref,
                       acc_ref, xbuf, x2d, sem, *, row_counts, per_step):
    i, f = pl.program_id(0), pl.program_id(1)
    n_i, n_f = pl.num_programs(0), pl.num_programs(1)
    tm, d = x2d.shape
    n_rows = xbuf.shape[1]
    slot = i % 2

    def row_copy(ref, s, r):
        return pltpu.make_async_copy(x_hbm.at[ref[0, 0, r]], xbuf.at[s, r], sem.at[s])

    def for_all_rows(fn):
        def body(r, c):
            fn(r)
            return c
        lax.fori_loop(0, n_rows, body, 0, unroll=DMA_UNROLL)

    @pl.when(f == 0)
    def _():
        acc_ref[...] = jnp.zeros_like(acc_ref)

        @pl.when(i == 0)
        def _():
            for_all_rows(lambda r: row_copy(idx_ref, slot, r).start())

        @pl.when(jnp.logical_or(i == 0, rows_ref[jnp.maximum(i - 1, 0)] > 0))
        def _():
            for_all_rows(lambda r: row_copy(idx_ref, slot, r).wait())
            x2d[...] = xbuf[slot, 0:tm].reshape(tm, d)

    for m in row_counts:
        @pl.when(rows_ref[i] == m)
        def _(m=m):
            for u in range(per_step):
                row_copy(nxt_ref, 1 - slot, f * per_step + u).start()
            acc_ref[0:m, :] += _swiglu_chunk(x2d[0:m, :], wg_ref, wu_ref, wd_ref)

    @pl.when(f == n_f - 1)
    def _():
        o_ref[...] = acc_ref[...].astype(BF16).reshape(o_ref.shape)

        @pl.when(jnp.logical_and(i == n_i - 1, rows_ref[i] > 0))
        def _():
            for_all_rows(lambda r: row_copy(nxt_ref, 1 - slot, r).wait())


def ffn(xb, w_gate, w_up, w_down, layer, tile_expert, tile_rows, *, tm, tf, row_counts):
    r, d = xb.shape
    d_ff = w_gate.shape[-1]
    n_f = d_ff // tf
    per_expert = w_gate.ndim == 4

    def chunk(i, f, rows):
        return jnp.where(rows[i] > 0, f, n_f - 1)

    def w_col(i, f, te, rows):
        return (layer, te[i], 0, chunk(i, f, rows)) if per_expert else (layer, 0, chunk(i, f, rows))

    def w_row(i, f, te, rows):
        return (layer, te[i], chunk(i, f, rows), 0) if per_expert else (layer, chunk(i, f, rows), 0)

    lead = (None, None) if per_expert else (None,)
    grid_spec = pltpu.PrefetchScalarGridSpec(
        num_scalar_prefetch=2,
        grid=(r // tm, n_f),
        in_specs=[pl.BlockSpec((tm, d), lambda i, f, te, rows: (i, 0)),
                  pl.BlockSpec((*lead, d, tf), w_col),
                  pl.BlockSpec((*lead, d, tf), w_col),
                  pl.BlockSpec((*lead, tf, d), w_row)],
        out_specs=pl.BlockSpec((tm, d // LANES, LANES), lambda i, f, te, rows: (i, 0, 0)),
        scratch_shapes=[pltpu.VMEM((tm, d), F32)],
    )
    return pl.pallas_call(
        functools.partial(_ffn_kernel, row_counts=row_counts),
        grid_spec=grid_spec,
        out_shape=jax.ShapeDtypeStruct((r, d // LANES, LANES), BF16),
        compiler_params=_params(("parallel", "arbitrary")),
        name="ffn",
    )(tile_expert, tile_rows, xb, w_gate, w_up, w_down)


def ffn_gathered(x3, src, w_gate, w_up, w_down, layer, tile_expert, tile_rows, *, tm, tf, row_counts):
    _, s_dim, l_dim = x3.shape
    d = s_dim * l_dim
    n_tiles, _, n_rows = src.shape
    d_ff = w_gate.shape[-1]
    n_f = d_ff // tf

    def chunk(i, f, rows):
        return jnp.where(rows[i] > 0, f, n_f - 1)

    smem = dict(memory_space=pltpu.SMEM)
    grid_spec = pltpu.PrefetchScalarGridSpec(
        num_scalar_prefetch=2,
        grid=(n_tiles, n_f),
        in_specs=[pl.BlockSpec((1, 1, n_rows), lambda i, f, te, rows: (i, 0, 0), **smem),
                  pl.BlockSpec((1, 1, n_rows), lambda i, f, te, rows: (jnp.minimum(i + 1, n_tiles - 1), 0, 0), **smem),
                  pl.BlockSpec(memory_space=pl.ANY),
                  pl.BlockSpec((None, None, d, tf), lambda i, f, te, rows: (layer, te[i], 0, chunk(i, f, rows))),
                  pl.BlockSpec((None, None, d, tf), lambda i, f, te, rows: (layer, te[i], 0, chunk(i, f, rows))),
                  pl.BlockSpec((None, None, tf, d), lambda i, f, te, rows: (layer, te[i], chunk(i, f, rows), 0))],
        out_specs=pl.BlockSpec((tm, s_dim, l_dim), lambda i, f, te, rows: (i, 0, 0)),
        scratch_shapes=[pltpu.VMEM((tm, d), F32),
                        pltpu.VMEM((2, n_rows, s_dim, l_dim), BF16),
                        pltpu.VMEM((tm, d), BF16),
                        pltpu.SemaphoreType.DMA((2,))],
    )
    return pl.pallas_call(
        functools.partial(_ffn_gather_kernel, row_counts=row_counts, per_step=n_rows // n_f),
        grid_spec=grid_spec,
        out_shape=jax.ShapeDtypeStruct((n_tiles * tm, s_dim, l_dim), BF16),
        compiler_params=_params(("arbitrary", "arbitrary")),
        name="ffn_gathered",
    )(tile_expert, tile_rows, src, src, x3, w_gate, w_up, w_down)


DMA_UNROLL = 8


def _step_and_next_specs(n_steps, width):
    cur = pl.BlockSpec((1, 1, width), lambda i: (i, 0, 0), memory_space=pltpu.SMEM)
    nxt = pl.BlockSpec((1, 1, width), lambda i: (jnp.minimum(i + 1, n_steps - 1), 0, 0), memory_space=pltpu.SMEM)
    return cur, nxt


def _combine_ln_kernel(pos_ref, nxt_ref, w_ref, x_ref, y_hbm, g_ref, b_ref, xo_ref, xb_ref,
                       buf, sem, *, rows, alpha):
    i = pl.program_id(0)
    slot = i % 2

    def row_copy(ref, s, k, r):
        return pltpu.make_async_copy(y_hbm.at[ref[0, 0, k * rows + r]], buf.at[s, k, r], sem.at[s, k])

    def issue(ref, s):
        def body(r, c):
            row_copy(ref, s, 0, r).start()
            row_copy(ref, s, 1, r).start()
            return c
        lax.fori_loop(0, rows, body, 0, unroll=DMA_UNROLL)

    @pl.when(i == 0)
    def _():
        issue(pos_ref, slot)

    @pl.when(i + 1 < pl.num_programs(0))
    def _():
        issue(nxt_ref, 1 - slot)

    def wait(r, c):
        row_copy(pos_ref, slot, 0, r).wait()
        row_copy(pos_ref, slot, 1, r).wait()
        return c

    lax.fori_loop(0, rows, wait, 0, unroll=DMA_UNROLL)
    w = w_ref[...]
    ya = buf[slot, 0].reshape(x_ref.shape).astype(F32)
    yb = buf[slot, 1].reshape(x_ref.shape).astype(F32)
    y = w[:, 2:3] * ya + w[:, 3:4] * yb
    out = _layer_norm(alpha * x_ref[...] + y, g_ref[...], b_ref[...])
    xo_ref[...] = out
    xb_ref[...] = out.astype(BF16)


def combine_ln(x, y, pos, route, ln_g, ln_b, *, alpha, rows):
    n, d = x.shape
    n_steps = n // rows
    kern = functools.partial(_combine_ln_kernel, rows=rows, alpha=alpha)
    row = lambda i: (i, 0)
    fixed = lambda i: (0, 0)
    return pl.pallas_call(
        kern,
        grid=(n_steps,),
        in_specs=[*_step_and_next_specs(n_steps, 2 * rows),
                  pl.BlockSpec((rows, LANES), row),
                  pl.BlockSpec((rows, d), row),
                  pl.BlockSpec(memory_space=pl.ANY),
                  pl.BlockSpec((1, d), fixed),
                  pl.BlockSpec((1, d), fixed)],
        out_specs=[pl.BlockSpec((rows, d), row), pl.BlockSpec((rows, d), row)],
        out_shape=[jax.ShapeDtypeStruct((n, d), F32), jax.ShapeDtypeStruct((n, d), BF16)],
        scratch_shapes=[pltpu.VMEM((2, 2, rows, d // LANES, LANES), BF16), pltpu.SemaphoreType.DMA((2, 2))],
        compiler_params=_params(("arbitrary",)),
        name="combine_ln",
    )(pos, pos, route, x, y, ln_g.reshape(1, d), ln_b.reshape(1, d))


def moe_block(xf, xb3, route, w_gate, w_up, w_down, layer, ln_g, ln_b, *, alpha, tm, tf, rows):
    n, d = xf.shape
    n_e = w_gate.shape[1]
    ef = route[:, 0:2].astype(jnp.int32).reshape(-1)
    onehot = (ef[:, None] == jnp.arange(n_e, dtype=jnp.int32)[None, :]).astype(jnp.int32)
    csum = jnp.cumsum(onehot, axis=0)
    rank = jnp.take_along_axis(csum, ef[:, None], axis=1)[:, 0] - 1
    counts = csum[-1]
    padded = ((counts + tm - 1) // tm) * tm
    ends = jnp.cumsum(padded)
    pos = (ends - padded)[ef] + rank
    r_rows = 2 * n + n_e * tm
    src = jnp.zeros((r_rows,), jnp.int32).at[pos].set(jnp.arange(2 * n, dtype=jnp.int32) // 2)
    tile_start = jnp.arange(r_rows // tm, dtype=jnp.int32) * tm
    te_raw = jnp.minimum(jnp.searchsorted(ends, tile_start, side="right").astype(jnp.int32), n_e - 1)
    last_e = te_raw[jnp.maximum(ends[-1] // tm - 1, 0)]
    tile_expert = jnp.where(tile_start < ends[-1], te_raw, last_e)
    counts_sub = ((counts + FFN_SUB - 1) // FFN_SUB) * FFN_SUB
    tile_rows = jnp.clip(((ends - padded) + counts_sub)[te_raw] - tile_start, 0, tm)

    n_f = w_gate.shape[-1] // tf
    n_req = -(-tm // n_f) * n_f
    src_tiles = jnp.pad(src.reshape(r_rows // tm, tm), ((0, 0), (0, n_req - tm))).reshape(r_rows // tm, 1, n_req)
    ys = ffn_gathered(xb3, src_tiles, w_gate, w_up, w_down, layer, tile_expert, tile_rows, tm=tm, tf=tf,
                      row_counts=tuple(range(FFN_SUB, tm + 1, FFN_SUB)))
    pos2 = pos.reshape(n // rows, rows, 2).transpose(0, 2, 1).reshape(n // rows, 1, 2 * rows)
    return combine_ln(xf, ys, pos2, route, ln_g, ln_b, alpha=alpha, rows=rows)


def kernel(x, w_in, sb_norm_g, conv_w, gla_w_up, gla_b, gla_norm_g, w_out, ln1_g, ln1_b,
           ffn_w_gate, ffn_w_up, ffn_w_down, moe_w_router, moe_w_gate, moe_w_up, moe_w_down,
           ln2_g, ln2_b):
    batch, seq, d = x.shape
    depth = w_in.shape[0]
    n = batch * seq
    alpha = (2 * depth) ** 0.25

    xf = x.reshape(n, d)
    xb = xf.astype(BF16)
    w_r_all = lax.optimization_barrier(w_in[:, :, OFF_G_R:])
    w_in_t = jnp.swapaxes(w_in, 1, 2)
    for layer in range(depth):
        w_r_pad = jnp.pad(w_r_all[layer], ((0, 0), (0, LANES - GLA_RANK))).astype(BF16)
        h, h_r = in_proj(xb, w_in_t, w_r_pad, layer, tm=IN_PROJ_TM, tn=IN_PROJ_TN)
        o_sb = sb_attention(h, sb_norm_g[layer], batch=batch, seq=seq)
        o_cv = gated_conv(h, conv_w[layer], batch=batch, seq=seq)
        w_up_pad = jnp.pad(gla_w_up[layer], ((0, LANES - GLA_RANK), (0, 0))).astype(BF16)
        o_gla = gla(h, h_r, w_up_pad, gla_b[layer], gla_norm_g[layer], batch=batch, seq=seq)
        j = layer // 2
        if layer % 2 == 1:
            xf, xb, route = outproj_ln(o_sb, o_cv, o_gla, w_out, layer, xf, ln1_g[layer], ln1_b[layer],
                                       moe_w_router[j], alpha=alpha, tm=ROW_TILE)
            xf, xb = moe_block(xf, xb, route, moe_w_gate, moe_w_up, moe_w_down, j,
                               ln2_g[layer], ln2_b[layer], alpha=alpha, tm=FFN_TM, tf=FFN_TF, rows=ROW_TILE)
        else:
            xf, xb = outproj_ln(o_sb, o_cv, o_gla, w_out, layer, xf, ln1_g[layer], ln1_b[layer], None,
                                alpha=alpha, tm=ROW_TILE)
            n_tiles = n // FFN_TM
            f = ffn(xb, ffn_w_gate, ffn_w_up, ffn_w_down, j, jnp.zeros((n_tiles,), jnp.int32),
                    jnp.full((n_tiles,), FFN_TM, jnp.int32), tm=FFN_TM, tf=FFN_TF, row_counts=(FFN_TM,))
            xf, xb = residual_ln(xf, f, ln2_g[layer], ln2_b[layer], alpha=alpha, tm=RESIDUAL_TM)
    return xf.reshape(batch, seq, d)
```

```python
import functools

import jax
import jax.numpy as jnp
from jax import lax
from jax.experimental import pallas as pl
from jax.experimental.pallas import tpu as pltpu

F32 = jnp.float32
BF16 = jnp.bfloat16

D_MODEL = 2048
SB_HEADS = 8
SB_DH = 128
SB_WIDTH = SB_HEADS * SB_DH
CONV_WIDTH = 512
CONV_K = 3
GLA_HEADS = 4
GLA_DK = 64
GLA_DV = 128
GLA_KW = GLA_HEADS * GLA_DK
GLA_VW = GLA_HEADS * GLA_DV
GLA_RANK = 16
GLA_TAU = 16.0
GLA_CHUNK = 16
IN_COLS = 3 * SB_WIDTH + 3 * CONV_WIDTH + 2 * GLA_KW + 2 * GLA_VW + GLA_RANK
N_EXPERTS = 8
LN_EPS = 1e-5
RMS_EPS = 1e-6

LANES = 128
VMEM_LIMIT = 56 * 1024 * 1024

OFF_SB_Q, OFF_SB_K, OFF_SB_V = 0, SB_WIDTH, 2 * SB_WIDTH
OFF_CV_B = 3 * SB_WIDTH
OFF_CV_C = OFF_CV_B + CONV_WIDTH
OFF_CV_H = OFF_CV_C + CONV_WIDTH
OFF_G_Q = OFF_CV_H + CONV_WIDTH
OFF_G_K = OFF_G_Q + GLA_KW
OFF_G_V = OFF_G_K + GLA_KW
OFF_G_G = OFF_G_V + GLA_VW
OFF_G_R = OFF_G_G + GLA_VW

SB_EXIT = -90.0

IN_PROJ_TM = 1024
IN_PROJ_TN = 1024
ROW_TILE = 256
RESIDUAL_TM = 512


def _params(sem):
    return pltpu.CompilerParams(dimension_semantics=sem, vmem_limit_bytes=VMEM_LIMIT)


def _split_bf16(x):
    hi = x.astype(BF16)
    lo = (x - hi.astype(F32)).astype(BF16)
    return hi, lo


def _dot(a, b):
    return jnp.dot(a, b, preferred_element_type=F32)


def _silu(x):
    return x * (1.0 / (1.0 + jnp.exp(-x)))


def _in_proj_kernel(x_ref, w_ref, wr_ref, h_ref, r_ref):
    x = x_ref[...]
    w_t = w_ref[...].astype(BF16)
    h_ref[...] = lax.dot_general(x, w_t, (((1,), (1,)), ((), ())), preferred_element_type=F32).astype(h_ref.dtype)

    @pl.when(pl.program_id(1) == 0)
    def _():
        r_ref[...] = _dot(x, wr_ref[...]).astype(r_ref.dtype)


def in_proj(xb, w_in_t, w_r_pad, layer, *, tm, tn):
    n, d = xb.shape
    return pl.pallas_call(
        _in_proj_kernel,
        grid=(n // tm, OFF_G_R // tn),
        in_specs=[pl.BlockSpec((tm, d), lambda i, j: (i, 0)),
                  pl.BlockSpec((None, tn, d), lambda i, j: (layer, j, 0)),
                  pl.BlockSpec((d, LANES), lambda i, j: (0, 0))],
        out_specs=[pl.BlockSpec((tm, tn), lambda i, j: (i, j)),
                   pl.BlockSpec((tm, LANES), lambda i, j: (i, 0))],
        out_shape=[jax.ShapeDtypeStruct((n, OFF_G_R), BF16), jax.ShapeDtypeStruct((n, LANES), BF16)],
        compiler_params=_params(("parallel", "arbitrary")),
        name="in_proj",
    )(xb, w_in_t, w_r_pad)


SB_BLK = 64
SB_WIN = 256
SB_GROUP = 16


LOG2E = 1.4426950408889634
SB_EXIT_LOG2 = SB_EXIT * LOG2E


def _minus_later_keys_matrix(n_keys):
    r = lax.broadcasted_iota(jnp.int32, (2 * n_keys, n_keys), 0)
    c = lax.broadcasted_iota(jnp.int32, (2 * n_keys, n_keys), 1)
    return jnp.where((r % n_keys) > c, -1.0, 0.0).astype(BF16)


def _sb_scores(q, k, scale):
    y = lax.dot_general(q, k, (((1,), (1,)), ((), ())), preferred_element_type=F32) * (scale * LOG2E)
    sp = jnp.maximum(y, 0.0) + jnp.log2(1.0 + jnp.exp2(-jnp.abs(y)))
    return y, sp


def _sb_tail(sp_masked, minus_later):
    hi, lo = _split_bf16(sp_masked)
    return _dot(jnp.concatenate([hi, lo], axis=1), minus_later)


def _sb_kernel(q_ref, k_ref, v_ref, g_ref, o_ref, acc_ref, c_ref, *, group, scale):
    minus_later_win = _minus_later_keys_matrix(SB_WIN)
    minus_later_blk = _minus_later_keys_matrix(SB_BLK)
    row = lax.broadcasted_iota(jnp.int32, (SB_BLK, SB_WIN), 0)
    col = lax.broadcasted_iota(jnp.int32, (SB_BLK, SB_WIN), 1)
    causal_full = col < row + (SB_WIN - SB_BLK)
    n_clamped = (SB_WIN - SB_BLK) // SB_BLK
    gain = g_ref[...]

    def q_rows(qi):
        return pl.ds(pl.multiple_of(qi * SB_BLK, SB_BLK), SB_BLK)

    def windows(qis):
        wss = [pl.multiple_of(jnp.maximum(qi - n_clamped, 0) * SB_BLK, SB_BLK) for qi in qis]
        causal = [col < row + (qi * SB_BLK - ws) for qi, ws in zip(qis[:n_clamped], wss)]
        causal += [causal_full] * (len(qis) - n_clamped)
        scores = [_sb_scores(q_ref[q_rows(qi), :], k_ref[pl.ds(ws, SB_WIN), :], scale) for qi, ws in zip(qis, wss)]
        sp_m = [jnp.where(m, sp, 0.0) for m, (_, sp) in zip(causal, scores)]
        tails = [_sb_tail(s, minus_later_win) for s in sp_m]
        c_max = None
        for u, (m, (y, sp), s, tail, ws) in enumerate(zip(causal, scores, sp_m, tails, wss)):
            a = jnp.where(m, jnp.exp2(y - sp + tail), 0.0)
            acc_ref[u] = _dot(a.astype(BF16), v_ref[pl.ds(ws, SB_WIN), :])
            c = tail[:, 0:1] - s[:, 0:1]
            c_ref[u] = c
            if u <= n_clamped:
                c = jnp.where(ws > 0, c, -jnp.inf)
            c_max = c if c_max is None else jnp.maximum(c_max, c)
        return c_max

    def earlier_blocks(qi, u):
        q = q_ref[q_rows(qi), :]

        def body(carry):
            j, _ = carry
            start = pl.multiple_of(j * SB_BLK, SB_BLK)
            y, sp = _sb_scores(q, k_ref[pl.ds(start, SB_BLK), :], scale)
            tail = _sb_tail(sp, minus_later_blk)
            c = c_ref[u]
            a = jnp.exp2(y - sp + tail + c)
            acc_ref[u] += _dot(a.astype(BF16), v_ref[pl.ds(start, SB_BLK), :])
            c_new = c + tail[:, 0:1] - sp[:, 0:1]
            c_ref[u] = c_new
            return j - 1, jnp.max(c_new) > SB_EXIT_LOG2

        def cond(carry):
            j, more = carry
            return jnp.logical_and(j >= 0, more)

        lax.while_loop(cond, body, (jnp.maximum(qi - n_clamped, 0) - 1, jnp.max(c_ref[u]) > SB_EXIT_LOG2))

    def run_group(gi, carry):
        c_max = windows([gi * group + u for u in range(group)])

        @pl.when(jnp.max(c_max) > SB_EXIT_LOG2)
        def _():
            for u in range(group):
                earlier_blocks(gi * group + u, u)

        for u in range(group):
            o = acc_ref[u]
            ms = jnp.mean(o * o, axis=-1, keepdims=True)
            rows = pl.ds(pl.multiple_of((gi * group + u) * SB_BLK, SB_BLK), SB_BLK)
            o_ref[rows, :] = (o * lax.rsqrt(ms + RMS_EPS) * gain).astype(o_ref.dtype)
        return carry

    lax.fori_loop(0, q_ref.shape[0] // (SB_BLK * group), run_group, 0)


def sb_attention(h, norm_g, *, batch, seq):
    n = batch * seq
    qb, kb, vb = OFF_SB_Q // SB_DH, OFF_SB_K // SB_DH, OFF_SB_V // SB_DH
    group = min(SB_GROUP, seq // SB_BLK)
    kern = functools.partial(_sb_kernel, group=group, scale=SB_DH ** -0.5)
    return pl.pallas_call(
        kern,
        grid=(batch, SB_HEADS),
        in_specs=[pl.BlockSpec((seq, SB_DH), lambda b, hh: (b, qb + hh)),
                  pl.BlockSpec((seq, SB_DH), lambda b, hh: (b, kb + hh)),
                  pl.BlockSpec((seq, SB_DH), lambda b, hh: (b, vb + hh)),
                  pl.BlockSpec((1, SB_DH), lambda b, hh: (0, 0))],
        out_specs=pl.BlockSpec((seq, SB_DH), lambda b, hh: (b, hh)),
        out_shape=jax.ShapeDtypeStruct((n, SB_WIDTH), BF16),
        scratch_shapes=[pltpu.VMEM((group, SB_BLK, SB_DH), F32), pltpu.VMEM((group, SB_BLK, 1), F32)],
        compiler_params=_params(("parallel", "parallel")),
        name="sb_attention",
    )(h, h, h, norm_g.reshape(1, SB_DH))


def _conv_kernel(b_ref, c_ref, h_ref, w_ref, o_ref):
    u = c_ref[...].astype(F32) * h_ref[...].astype(F32)
    t = lax.broadcasted_iota(jnp.int32, u.shape, 0)
    u1 = jnp.where(t >= 1, pltpu.roll(u, 1, 0), 0.0)
    u2 = jnp.where(t >= 2, pltpu.roll(u, 2, 0), 0.0)
    w = w_ref[...]
    y = w[0:1, :] * u2 + w[1:2, :] * u1 + w[2:3, :] * u
    o_ref[...] = (b_ref[...].astype(F32) * y).astype(o_ref.dtype)


def gated_conv(h, conv_w, *, batch, seq):
    n = batch * seq
    nb = CONV_WIDTH // LANES
    bb, cb, hb = OFF_CV_B // LANES, OFF_CV_C // LANES, OFF_CV_H // LANES
    return pl.pallas_call(
        _conv_kernel,
        grid=(batch, nb),
        in_specs=[pl.BlockSpec((seq, LANES), lambda b, j: (b, bb + j)),
                  pl.BlockSpec((seq, LANES), lambda b, j: (b, cb + j)),
                  pl.BlockSpec((seq, LANES), lambda b, j: (b, hb + j)),
                  pl.BlockSpec((CONV_K, LANES), lambda b, j: (0, j))],
        out_specs=pl.BlockSpec((seq, LANES), lambda b, j: (b, j)),
        out_shape=jax.ShapeDtypeStruct((n, CONV_WIDTH), BF16),
        compiler_params=_params(("parallel", "parallel")),
        name="gated_conv",
    )(h, h, h, conv_w)


GLA_SEG = 128
GLA_SEGS_PER_STEP = 4


def _gla_kernel(q_ref, k_ref, v_ref, g_ref, r_ref, wup_ref, b_ref, gn_ref, o_ref, s_ref, *, n_seg):
    seg_len, chunk = GLA_SEG, GLA_CHUNK
    n_chunk = seg_len // chunk
    two_dk, two_dv = 2 * GLA_DK, 2 * GLA_DV

    row = lax.broadcasted_iota(jnp.int32, (seg_len, seg_len), 0)
    col = lax.broadcasted_iota(jnp.int32, (seg_len, seg_len), 1)
    same = (row // chunk) == (col // chunk)
    cum_m = jnp.logical_and(same, col <= row).astype(BF16)
    tot_m = same.astype(BF16)
    r2 = lax.broadcasted_iota(jnp.int32, (two_dk, two_dv), 0)
    c2 = lax.broadcasted_iota(jnp.int32, (two_dk, two_dv), 1)
    head_bd = (r2 // GLA_DK) == (c2 // GLA_DV)
    ones_bd = head_bd.astype(BF16)
    bd_f = head_bd.astype(F32)
    half = chunk // 2
    band_same = jnp.where(jnp.logical_and(row // half == col // half, col <= row), row - col, -1)
    ri = lax.broadcasted_iota(jnp.int32, (seg_len // 2, seg_len), 0)
    ci = lax.broadcasted_iota(jnp.int32, (seg_len // 2, seg_len), 1)
    in_lower = jnp.logical_and(ci // chunk == ri // half, ci % chunk < half)
    band_cross = jnp.where(in_lower, (ri - ci + chunk) % half, -1)
    lane_chunk = col // chunk

    wup = wup_ref[...]
    bias = b_ref[...]
    gn = gn_ref[...]
    s_ref[...] = jnp.zeros_like(s_ref)

    def rms(x):
        return x * lax.rsqrt(jnp.mean(x * x, axis=-1, keepdims=True) + RMS_EPS)

    def gates(rows):
        u = _dot(r_ref[rows, :], wup) + bias
        la = (jnp.minimum(u, 0.0) - jnp.log1p(jnp.exp(-jnp.abs(u)))) * (1.0 / GLA_TAU)
        hi, lo = _split_bf16(la)
        return _dot(cum_m, hi) + _dot(cum_m, lo), _dot(tot_m, hi) + _dot(tot_m, lo)

    def intra(q, k, v, bcum):
        n_half = seg_len // half
        q3, k3, b3 = (a.reshape(n_half, half, two_dk) for a in (q, k, bcum))
        score = _dot((q * k).astype(BF16), ones_bd)
        attn = [jnp.where(band_same == 0, score[:, :GLA_DV], 0.0), jnp.where(band_same == 0, score[:, GLA_DV:], 0.0)]
        for rho in range(1, half):
            decay = jnp.exp(b3 - pltpu.roll(b3, rho, 1))
            p = (q3 * pltpu.roll(k3, rho, 1) * decay).reshape(seg_len, two_dk)
            score = _dot(p.astype(BF16), ones_bd)
            attn = [jnp.where(band_same == rho, score[:, :GLA_DV], attn[0]),
                    jnp.where(band_same == rho, score[:, GLA_DV:], attn[1])]

        q4, k4, b4 = (a.reshape(n_chunk, 2, half, two_dk) for a in (q, k, bcum))
        q_up, b_up, k_lo, b_lo = q4[:, 1], b4[:, 1], k4[:, 0], b4[:, 0]
        cross = [jnp.zeros((n_chunk * half, seg_len), F32)] * 2
        for rho in range(half):
            k_r = pltpu.roll(k_lo, rho, 1) if rho else k_lo
            b_r = pltpu.roll(b_lo, rho, 1) if rho else b_lo
            p = (q_up * k_r * jnp.exp(b_up - b_r)).reshape(n_chunk * half, two_dk)
            score = _dot(p.astype(BF16), ones_bd)
            cross = [jnp.where(band_cross == rho, score[:, :GLA_DV], cross[0]),
                     jnp.where(band_cross == rho, score[:, GLA_DV:], cross[1])]

        def merged(same, up_lo):
            a4 = same.reshape(n_chunk, 2, half, seg_len)
            upper = a4[:, 1] + up_lo.reshape(n_chunk, half, seg_len)
            return jnp.stack([a4[:, 0], upper], axis=1).reshape(seg_len, seg_len).astype(BF16)

        return jnp.concatenate([_dot(merged(attn[0], cross[0]), v[:, :GLA_DV]),
                                _dot(merged(attn[1], cross[1]), v[:, GLA_DV:])], axis=1)

    def seg_group(gi, carry):
        all_rows = [pl.ds(pl.multiple_of((gi * GLA_SEGS_PER_STEP + s) * seg_len, seg_len), seg_len)
                    for s in range(GLA_SEGS_PER_STEP)]
        qs = [q_ref[rows, :].astype(F32) * (GLA_DK ** -0.5) for rows in all_rows]
        ks = [k_ref[rows, :].astype(F32) for rows in all_rows]
        vs = [v_ref[rows, :] for rows in all_rows]
        bcums, blasts = zip(*[gates(rows) for rows in all_rows])
        o_accs = [intra(q, k, v, bcum) for q, k, v, bcum in zip(qs, ks, vs, bcums)]

        q_ins = [(q * jnp.exp(bcum)).astype(BF16) for q, bcum in zip(qs, bcums)]
        k_up_ts = [(k * jnp.exp(blast - bcum)).T for k, bcum, blast in zip(ks, bcums, blasts)]
        g_ts = [jnp.exp(blast.T) for blast in blasts]
        kvs = [[_dot(jnp.where(lane_chunk == c, k_up_t, 0.0).astype(BF16), v) for c in range(n_chunk)]
               for k_up_t, v in zip(k_up_ts, vs)]
        state = s_ref[...]
        states = []
        for g_t, kv in zip(g_ts, kvs):
            seg_states = []
            for c in range(n_chunk):
                seg_states.append(state.astype(BF16))
                state = g_t[:, c * chunk:c * chunk + 1] * state + kv[c] * bd_f
            states.append(seg_states)
        s_ref[...] = state

        for rows, o_acc, q_in, seg_states in zip(all_rows, o_accs, q_ins, states):
            o_inter = [_dot(q_in[c * chunk:(c + 1) * chunk, :], seg_states[c]) for c in range(n_chunk)]
            o_acc = o_acc + jnp.concatenate(o_inter, axis=0)
            gate = g_ref[rows, :].astype(F32)
            o0 = rms(o_acc[:, :GLA_DV]) * gn
            o1 = rms(o_acc[:, GLA_DV:]) * gn
            out = jnp.concatenate([o0, o1], axis=1) * _silu(gate)
            o_ref[rows, :] = out.astype(o_ref.dtype)
        return carry

    lax.fori_loop(0, n_seg // GLA_SEGS_PER_STEP, seg_group, 0)


def gla(h, h_r, w_up_pad, gla_b, gla_norm_g, *, batch, seq):
    n = batch * seq
    pairs = GLA_HEADS // 2
    qb, kb = OFF_G_Q // LANES, OFF_G_K // LANES
    vb, gb = OFF_G_V // (2 * GLA_DV), OFF_G_G // (2 * GLA_DV)
    kern = functools.partial(_gla_kernel, n_seg=seq // GLA_SEG)
    return pl.pallas_call(
        kern,
        grid=(batch, pairs),
        in_specs=[pl.BlockSpec((seq, LANES), lambda b, p: (b, qb + p)),
                  pl.BlockSpec((seq, LANES), lambda b, p: (b, kb + p)),
                  pl.BlockSpec((seq, 2 * GLA_DV), lambda b, p: (b, vb + p)),
                  pl.BlockSpec((seq, 2 * GLA_DV), lambda b, p: (b, gb + p)),
                  pl.BlockSpec((seq, LANES), lambda b, p: (b, 0)),
                  pl.BlockSpec((LANES, LANES), lambda b, p: (0, p)),
                  pl.BlockSpec((1, LANES), lambda b, p: (0, p)),
                  pl.BlockSpec((1, GLA_DV), lambda b, p: (0, 0))],
        out_specs=pl.BlockSpec((seq, 2 * GLA_DV), lambda b, p: (b, p)),
        out_shape=jax.ShapeDtypeStruct((n, GLA_VW), BF16),
        scratch_shapes=[pltpu.VMEM((2 * GLA_DK, 2 * GLA_DV), F32)],
        compiler_params=_params(("parallel", "parallel")),
        name="gla",
    )(h, h, h, h, h_r, w_up_pad, gla_b.reshape(1, GLA_KW), gla_norm_g.reshape(1, GLA_DV))


def _layer_norm(y, g, b):
    mu = jnp.mean(y, axis=-1, keepdims=True)
    yc = y - mu
    var = jnp.mean(yc * yc, axis=-1, keepdims=True)
    return yc * lax.rsqrt(var + LN_EPS) * g + b


def _top2_route(logits):
    col = lax.broadcasted_iota(jnp.int32, logits.shape, 1)
    neg = jnp.float32(-jnp.inf)
    l1 = jnp.where(col < N_EXPERTS, logits, neg)
    m1 = jnp.max(l1, axis=-1, keepdims=True)
    i1 = jnp.min(jnp.where(l1 == m1, col, LANES), axis=-1, keepdims=True)
    l2 = jnp.where(col == i1, neg, l1)
    m2 = jnp.max(l2, axis=-1, keepdims=True)
    i2 = jnp.min(jnp.where(l2 == m2, col, LANES), axis=-1, keepdims=True)
    e2 = jnp.exp(m2 - m1)
    w1 = 1.0 / (1.0 + e2)
    w2 = e2 / (1.0 + e2)
    out = jnp.where(col == 0, i1.astype(F32), 0.0) + jnp.where(col == 1, i2.astype(F32), 0.0)
    return out + jnp.where(col == 2, w1, 0.0) + jnp.where(col == 3, w2, 0.0)


def _outproj_ln_kernel(*refs, alpha, routed):
    if routed:
        sb_ref, cv_ref, gl_ref, w_ref, x_ref, g_ref, b_ref, wr_ref, xo_ref, xb_ref, route_ref, wb_ref = refs
    else:
        sb_ref, cv_ref, gl_ref, w_ref, x_ref, g_ref, b_ref, xo_ref, xb_ref, wb_ref = refs

    @pl.when(pl.program_id(0) == 0)
    def _():
        wb_ref[...] = w_ref[...].astype(BF16)

    mix = _dot(sb_ref[...], wb_ref[0:SB_WIDTH, :])
    mix = mix + _dot(cv_ref[...], wb_ref[SB_WIDTH:SB_WIDTH + CONV_WIDTH, :])
    mix = mix + _dot(gl_ref[...], wb_ref[SB_WIDTH + CONV_WIDTH:, :])
    out = _layer_norm(alpha * x_ref[...] + mix, g_ref[...], b_ref[...])
    xo_ref[...] = out
    xb_ref[...] = out.astype(BF16).reshape(xb_ref.shape)
    if routed:
        col = lax.broadcasted_iota(jnp.int32, route_ref.shape, 1)
        logits = jnp.zeros(route_ref.shape, F32)
        for e in range(N_EXPERTS):
            logit_e = jnp.sum(out * wr_ref[e:e + 1, :], axis=-1, keepdims=True)
            logits = jnp.where(col == e, logit_e, logits)
        route_ref[...] = _top2_route(logits)


def _token_major_spec(tm, d, tiled):
    if tiled:
        return pl.BlockSpec((tm, d // LANES, LANES), lambda i: (i, 0, 0))
    return pl.BlockSpec((tm, d), lambda i: (i, 0))


def _token_major_shape(n, d, tiled):
    return (n, d // LANES, LANES) if tiled else (n, d)


def outproj_ln(o_sb, o_cv, o_gla, w_out, layer, x, ln_g, ln_b, w_router, *, alpha, tm):
    n, d = x.shape
    routed = w_router is not None
    kern = functools.partial(_outproj_ln_kernel, alpha=alpha, routed=routed)
    row = lambda i: (i, 0)
    fixed = lambda i: (0, 0)
    in_specs = [pl.BlockSpec((tm, SB_WIDTH), row),
                pl.BlockSpec((tm, CONV_WIDTH), row),
                pl.BlockSpec((tm, GLA_VW), row),
                pl.BlockSpec((None, d, d), lambda i: (layer, 0, 0), pipeline_mode=pl.Buffered(1)),
                pl.BlockSpec((tm, d), row),
                pl.BlockSpec((1, d), fixed),
                pl.BlockSpec((1, d), fixed)]
    args = [o_sb, o_cv, o_gla, w_out, x, ln_g.reshape(1, d), ln_b.reshape(1, d)]
    out_specs = [pl.BlockSpec((tm, d), row), _token_major_spec(tm, d, routed)]
    out_shape = [jax.ShapeDtypeStruct((n, d), F32), jax.ShapeDtypeStruct(_token_major_shape(n, d, routed), BF16)]
    if routed:
        in_specs.append(pl.BlockSpec((N_EXPERTS, d), fixed))
        args.append(w_router.T)
        out_specs.append(pl.BlockSpec((tm, LANES), row))
        out_shape.append(jax.ShapeDtypeStruct((n, LANES), F32))
    return pl.pallas_call(
        kern,
        grid=(n // tm,),
        in_specs=in_specs,
        out_specs=out_specs,
        out_shape=out_shape,
        scratch_shapes=[pltpu.VMEM((d, d), BF16)],
        compiler_params=_params(("arbitrary",)),
        name="outproj_ln",
    )(*args)


def _residual_ln_kernel(x_ref, f_ref, g_ref, b_ref, xo_ref, xb_ref, *, alpha):
    f = f_ref[...].reshape(x_ref.shape).astype(F32)
    out = _layer_norm(alpha * x_ref[...] + f, g_ref[...], b_ref[...])
    xo_ref[...] = out
    xb_ref[...] = out.astype(BF16)


def residual_ln(x, f, ln_g, ln_b, *, alpha, tm):
    n, d = x.shape
    kern = functools.partial(_residual_ln_kernel, alpha=alpha)
    row = lambda i: (i, 0)
    fixed = lambda i: (0, 0)
    return pl.pallas_call(
        kern,
        grid=(n // tm,),
        in_specs=[pl.BlockSpec((tm, d), row), _token_major_spec(tm, d, True),
                  pl.BlockSpec((1, d), fixed), pl.BlockSpec((1, d), fixed)],
        out_specs=[pl.BlockSpec((tm, d), row), pl.BlockSpec((tm, d), row)],
        out_shape=[jax.ShapeDtypeStruct((n, d), F32), jax.ShapeDtypeStruct((n, d), BF16)],
        compiler_params=_params(("parallel",)),
        name="residual_ln",
    )(x, f, ln_g.reshape(1, d), ln_b.reshape(1, d))


FFN_TM = 1024
FFN_TF = 256
FFN_SUB = 128


def _swiglu_chunk(x, wg_ref, wu_ref, wd_ref):
    hg = _dot(x, wg_ref[...].astype(BF16))
    hu = _dot(x, wu_ref[...].astype(BF16))
    hidden = (_silu(hg) * hu).astype(BF16)
    return _dot(hidden, wd_ref[...].astype(BF16))


def _ffn_kernel(te_ref, rows_ref, x_ref, wg_ref, wu_ref, wd_ref, o_ref, acc_ref, *, row_counts):
    i, f = pl.program_id(0), pl.program_id(1)

    @pl.when(f == 0)
    def _():
        acc_ref[...] = jnp.zeros_like(acc_ref)

    for m in row_counts:
        @pl.when(rows_ref[i] == m)
        def _(m=m):
            acc_ref[0:m, :] += _swiglu_chunk(x_ref[0:m, :], wg_ref, wu_ref, wd_ref)

    @pl.when(f == pl.num_programs(1) - 1)
    def _():
        o_ref[...] = acc_ref[...].astype(BF16).reshape(o_ref.shape)


def _ffn_gather_kernel(te_ref, rows_ref, idx_ref, nxt_ref, x_hbm, wg_ref, wu_ref, wd_ref, o_ref,
                       acc_ref, xbuf, x2d, sem, *, row_counts, per_step):
    i, f = pl.program_id(0), pl.program_id(1)
    n_i, n_f = pl.num_programs(0), pl.num_programs(1)
    tm, d = x2d.shape
    n_rows = xbuf.shape[1]
    slot = i % 2

    def row_copy(ref, s, r):
        return pltpu.make_async_copy(x_hbm.at[ref[0, 0, r]], xbuf.at[s, r], sem.at[s])

    def for_all_rows(fn):
        def body(r, c):
            fn(r)
            return c
        lax.fori_loop(0, n_rows, body, 0, unroll=DMA_UNROLL)

    @pl.when(f == 0)
    def _():
        acc_ref[...] = jnp.zeros_like(acc_ref)

        @pl.when(i == 0)
        def _():
            for_all_rows(lambda r: row_copy(idx_ref, slot, r).start())

        @pl.when(jnp.logical_or(i == 0, rows_ref[jnp.maximum(i - 1, 0)] > 0))
        def _():
            for_all_rows(lambda r: row_copy(idx_ref, slot, r).wait())
            x2d[...] = xbuf[slot, 0:tm].reshape(tm, d)

    for m in row_counts:
        @pl.when(rows_ref[i] == m)
        def _(m=m):
            for u in range(per_step):
                row_copy(nxt_ref, 1 - slot, f * per_step + u).start()
            acc_ref[0:m, :] += _swiglu_chunk(x2d[0:m, :], wg_ref, wu_ref, wd_ref)

    @pl.when(f == n_f - 1)
    def _():
        o_ref[...] = acc_ref[...].astype(BF16).reshape(o_ref.shape)

        @pl.when(jnp.logical_and(i == n_i - 1, rows_ref[i] > 0))
        def _():
            for_all_rows(lambda r: row_copy(nxt_ref, 1 - slot, r).wait())


def ffn(xb, w_gate, w_up, w_down, layer, tile_expert, tile_rows, *, tm, tf, row_counts):
    r, d = xb.shape
    d_ff = w_gate.shape[-1]
    n_f = d_ff // tf
    per_expert = w_gate.ndim == 4

    def chunk(i, f, rows):
        return jnp.where(rows[i] > 0, f, n_f - 1)

    def w_col(i, f, te, rows):
        return (layer, te[i], 0, chunk(i, f, rows)) if per_expert else (layer, 0, chunk(i, f, rows))

    def w_row(i, f, te, rows):
        return (layer, te[i], chunk(i, f, rows), 0) if per_expert else (layer, chunk(i, f, rows), 0)

    lead = (None, None) if per_expert else (None,)
    grid_spec = pltpu.PrefetchScalarGridSpec(
        num_scalar_prefetch=2,
        grid=(r // tm, n_f),
        in_specs=[pl.BlockSpec((tm, d), lambda i, f, te, rows: (i, 0)),
                  pl.BlockSpec((*lead, d, tf), w_col),
                  pl.BlockSpec((*lead, d, tf), w_col),
                  pl.BlockSpec((*lead, tf, d), w_row)],
        out_specs=pl.BlockSpec((tm, d // LANES, LANES), lambda i, f, te, rows: (i, 0, 0)),
        scratch_shapes=[pltpu.VMEM((tm, d), F32)],
    )
    return pl.pallas_call(
        functools.partial(_ffn_kernel, row_counts=row_counts),
        grid_spec=grid_spec,
        out_shape=jax.ShapeDtypeStruct((r, d // LANES, LANES), BF16),
        compiler_params=_params(("parallel", "arbitrary")),
        name="ffn",
    )(tile_expert, tile_rows, xb, w_gate, w_up, w_down)


def ffn_gathered(x3, src, w_gate, w_up, w_down, layer, tile_expert, tile_rows, *, tm, tf, row_counts):
    _, s_dim, l_dim = x3.shape
    d = s_dim * l_dim
    n_tiles, _, n_rows = src.shape
    d_ff = w_gate.shape[-1]
    n_f = d_ff // tf

    def chunk(i, f, rows):
        return jnp.where(rows[i] > 0, f, n_f - 1)

    smem = dict(memory_space=pltpu.SMEM)
    grid_spec = pltpu.PrefetchScalarGridSpec(
        num_scalar_prefetch=2,
        grid=(n_tiles, n_f),
        in_specs=[pl.BlockSpec((1, 1, n_rows), lambda i, f, te, rows: (i, 0, 0), **smem),
                  pl.BlockSpec((1, 1, n_rows), lambda i, f, te, rows: (jnp.minimum(i + 1, n_tiles - 1), 0, 0), **smem),
                  pl.BlockSpec(memory_space=pl.ANY),
                  pl.BlockSpec((None, None, d, tf), lambda i, f, te, rows: (layer, te[i], 0, chunk(i, f, rows))),
                  pl.BlockSpec((None, None, d, tf), lambda i, f, te, rows: (layer, te[i], 0, chunk(i, f, rows))),
                  pl.BlockSpec((None, None, tf, d), lambda i, f, te, rows: (layer, te[i], chunk(i, f, rows), 0))],
        out_specs=pl.BlockSpec((tm, s_dim, l_dim), lambda i, f, te, rows: (i, 0, 0)),
        scratch_shapes=[pltpu.VMEM((tm, d), F32),
                        pltpu.VMEM((2, n_rows, s_dim, l_dim), BF16),
                        pltpu.VMEM((tm, d), BF16),
                        pltpu.SemaphoreType.DMA((2,))],
    )
    return pl.pallas_call(
        functools.partial(_ffn_gather_kernel, row_counts=row_counts, per_step=n_rows // n_f),
        grid_spec=grid_spec,
        out_shape=jax.ShapeDtypeStruct((n_tiles * tm, s_dim, l_dim), BF16),
        compiler_params=_params(("arbitrary", "arbitrary")),
        name="ffn_gathered",
    )(tile_expert, tile_rows, src, src, x3, w_gate, w_up, w_down)


DMA_UNROLL = 8


def _step_and_next_specs(n_steps, width):
    cur = pl.BlockSpec((1, 1, width), lambda i: (i, 0, 0), memory_space=pltpu.SMEM)
    nxt = pl.BlockSpec((1, 1, width), lambda i: (jnp.minimum(i + 1, n_steps - 1), 0, 0), memory_space=pltpu.SMEM)
    return cur, nxt


def _combine_ln_kernel(pos_ref, nxt_ref, w_ref, x_ref, y_hbm, g_ref, b_ref, xo_ref, xb_ref,
                       buf, sem, *, rows, alpha):
    i = pl.program_id(0)
    slot = i % 2

    def row_copy(ref, s, k, r):
        return pltpu.make_async_copy(y_hbm.at[ref[0, 0, k * rows + r]], buf.at[s, k, r], sem.at[s, k])

    def issue(ref, s):
        def body(r, c):
            row_copy(ref, s, 0, r).start()
            row_copy(ref, s, 1, r).start()
            return c
        lax.fori_loop(0, rows, body, 0, unroll=DMA_UNROLL)

    @pl.when(i == 0)
    def _():
        issue(pos_ref, slot)

    @pl.when(i + 1 < pl.num_programs(0))
    def _():
        issue(nxt_ref, 1 - slot)

    def wait(r, c):
        row_copy(pos_ref, slot, 0, r).wait()
        row_copy(pos_ref, slot, 1, r).wait()
        return c

    lax.fori_loop(0, rows, wait, 0, unroll=DMA_UNROLL)
    w = w_ref[...]
    ya = buf[slot, 0].reshape(x_ref.shape).astype(F32)
    yb = buf[slot, 1].reshape(x_ref.shape).astype(F32)
    y = w[:, 2:3] * ya + w[:, 3:4] * yb
    out = _layer_norm(alpha * x_ref[...] + y, g_ref[...], b_ref[...])
    xo_ref[...] = out
    xb_ref[...] = out.astype(BF16)


def combine_ln(x, y, pos, route, ln_g, ln_b, *, alpha, rows):
    n, d = x.shape
    n_steps = n // rows
    kern = functools.partial(_combine_ln_kernel, rows=rows, alpha=alpha)
    row = lambda i: (i, 0)
    fixed = lambda i: (0, 0)
    return pl.pallas_call(
        kern,
        grid=(n_steps,),
        in_specs=[*_step_and_next_specs(n_steps, 2 * rows),
                  pl.BlockSpec((rows, LANES), row),
                  pl.BlockSpec((rows, d), row),
                  pl.BlockSpec(memory_space=pl.ANY),
                  pl.BlockSpec((1, d), fixed),
                  pl.BlockSpec((1, d), fixed)],
        out_specs=[pl.BlockSpec((rows, d), row), pl.BlockSpec((rows, d), row)],
        out_shape=[jax.ShapeDtypeStruct((n, d), F32), jax.ShapeDtypeStruct((n, d), BF16)],
        scratch_shapes=[pltpu.VMEM((2, 2, rows, d // LANES, LANES), BF16), pltpu.SemaphoreType.DMA((2, 2))],
        compiler_params=_params(("arbitrary",)),
        name="combine_ln",
    )(pos, pos, route, x, y, ln_g.reshape(1, d), ln_b.reshape(1, d))


def moe_block(xf, xb3, route, w_gate, w_up, w_down, layer, ln_g, ln_b, *, alpha, tm, tf, rows):
    n, d = xf.shape
    n_e = w_gate.shape[1]
    ef = route[:, 0:2].astype(jnp.int32).reshape(-1)
    onehot = (ef[:, None] == jnp.arange(n_e, dtype=jnp.int32)[None, :]).astype(jnp.int32)
    csum = jnp.cumsum(onehot, axis=0)
    rank = jnp.take_along_axis(csum, ef[:, None], axis=1)[:, 0] - 1
    counts = csum[-1]
    padded = ((counts + tm - 1) // tm) * tm
    ends = jnp.cumsum(padded)
    pos = (ends - padded)[ef] + rank
    r_rows = 2 * n + n_e * tm
    src = jnp.zeros((r_rows,), jnp.int32).at[pos].set(jnp.arange(2 * n, dtype=jnp.int32) // 2)
    tile_start = jnp.arange(r_rows // tm, dtype=jnp.int32) * tm
    te_raw = jnp.minimum(jnp.searchsorted(ends, tile_start, side="right").astype(jnp.int32), n_e - 1)
    last_e = te_raw[jnp.maximum(ends[-1] // tm - 1, 0)]
    tile_expert = jnp.where(tile_start < ends[-1], te_raw, last_e)
    counts_sub = ((counts + FFN_SUB - 1) // FFN_SUB) * FFN_SUB
    tile_rows = jnp.clip(((ends - padded) + counts_sub)[te_raw] - tile_start, 0, tm)

    n_f = w_gate.shape[-1] // tf
    n_req = -(-tm // n_f) * n_f
    src_tiles = jnp.pad(src.reshape(r_rows // tm, tm), ((0, 0), (0, n_req - tm))).reshape(r_rows // tm, 1, n_req)
    ys = ffn_gathered(xb3, src_tiles, w_gate, w_up, w_down, layer, tile_expert, tile_rows, tm=tm, tf=tf,
                      row_counts=tuple(range(FFN_SUB, tm + 1, FFN_SUB)))
    pos2 = pos.reshape(n // rows, rows, 2).transpose(0, 2, 1).reshape(n // rows, 1, 2 * rows)
    return combine_ln(xf, ys, pos2, route, ln_g, ln_b, alpha=alpha, rows=rows)


def kernel(x, w_in, sb_norm_g, conv_w, gla_w_up, gla_b, gla_norm_g, w_out, ln1_g, ln1_b,
           ffn_w_gate, ffn_w_up, ffn_w_down, moe_w_router, moe_w_gate, moe_w_up, moe_w_down,
           ln2_g, ln2_b):
    batch, seq, d = x.shape
    depth = w_in.shape[0]
    n = batch * seq
    alpha = (2 * depth) ** 0.25

    xf = x.reshape(n, d)
    xb = xf.astype(BF16)
    w_r_all = lax.optimization_barrier(w_in[:, :, OFF_G_R:])
    w_in_t = jnp.swapaxes(w_in, 1, 2)
    for layer in range(depth):
        w_r_pad = jnp.pad(w_r_all[layer], ((0, 0), (0, LANES - GLA_RANK))).astype(BF16)
        h, h_r = in_proj(xb, w_in_t, w_r_pad, layer, tm=IN_PROJ_TM, tn=IN_PROJ_TN)
        o_sb = sb_attention(h, sb_norm_g[layer], batch=batch, seq=seq)
        o_cv = gated_conv(h, conv_w[layer], batch=batch, seq=seq)
        w_up_pad = jnp.pad(gla_w_up[layer], ((0, LANES - GLA_RANK), (0, 0))).astype(BF16)
        o_gla = gla(h, h_r, w_up_pad, gla_b[layer], gla_norm_g[layer], batch=batch, seq=seq)
        j = layer // 2
        if layer % 2 == 1:
            xf, xb, route = outproj_ln(o_sb, o_cv, o_gla, w_out, layer, xf, ln1_g[layer], ln1_b[layer],
                                       moe_w_router[j], alpha=alpha, tm=ROW_TILE)
            xf, xb = moe_block(xf, xb, route, moe_w_gate, moe_w_up, moe_w_down, j,
                               ln2_g[layer], ln2_b[layer], alpha=alpha, tm=FFN_TM, tf=FFN_TF, rows=ROW_TILE)
        else:
            xf, xb = outproj_ln(o_sb, o_cv, o_gla, w_out, layer, xf, ln1_g[layer], ln1_b[layer], None,
                                alpha=alpha, tm=ROW_TILE)
            n_tiles = n // FFN_TM
            f = ffn(xb, ffn_w_gate, ffn_w_up, ffn_w_down, j, jnp.zeros((n_tiles,), jnp.int32),
                    jnp.full((n_tiles,), FFN_TM, jnp.int32), tm=FFN_TM, tf=FFN_TF, row_counts=(FFN_TM,))
            xf, xb = residual_ln(xf, f, ln2_g[layer], ln2_b[layer], alpha=alpha, tm=RESIDUAL_TM)
    return xf.reshape(batch, seq, d)
```

```python
import functools

import jax
import jax.numpy as jnp
from jax import lax
from jax.experimental import pallas as pl
from jax.experimental.pallas import tpu as pltpu

F32 = jnp.float32
BF16 = jnp.bfloat16

D_MODEL = 2048
SB_HEADS = 8
SB_DH = 128
SB_WIDTH = SB_HEADS * SB_DH
CONV_WIDTH = 512
CONV_K = 3
GLA_HEADS = 4
GLA_DK = 64
GLA_DV = 128
GLA_KW = GLA_HEADS * GLA_DK
GLA_VW = GLA_HEADS * GLA_DV
GLA_RANK = 16
GLA_TAU = 16.0
GLA_CHUNK = 16
IN_COLS = 3 * SB_WIDTH + 3 * CONV_WIDTH + 2 * GLA_KW + 2 * GLA_VW + GLA_RANK
N_EXPERTS = 8
LN_EPS = 1e-5
RMS_EPS = 1e-6

LANES = 128
VMEM_LIMIT = 56 * 1024 * 1024

OFF_SB_Q, OFF_SB_K, OFF_SB_V = 0, SB_WIDTH, 2 * SB_WIDTH
OFF_CV_B = 3 * SB_WIDTH
OFF_CV_C = OFF_CV_B + CONV_WIDTH
OFF_CV_H = OFF_CV_C + CONV_WIDTH
OFF_G_Q = OFF_CV_H + CONV_WIDTH
OFF_G_K = OFF_G_Q + GLA_KW
OFF_G_V = OFF_G_K + GLA_KW
OFF_G_G = OFF_G_V + GLA_VW
OFF_G_R = OFF_G_G + GLA_VW

SB_EXIT = -90.0

IN_PROJ_TM = 1024
IN_PROJ_TN = 1024
ROW_TILE = 256
RESIDUAL_TM = 512


def _params(sem):
    return pltpu.CompilerParams(dimension_semantics=sem, vmem_limit_bytes=VMEM_LIMIT)


def _split_bf16(x):
    hi = x.astype(BF16)
    lo = (x - hi.astype(F32)).astype(BF16)
    return hi, lo


def _dot(a, b):
    return jnp.dot(a, b, preferred_element_type=F32)


def _silu(x):
    return x * (1.0 / (1.0 + jnp.exp(-x)))


def _in_proj_kernel(x_ref, w_ref, wr_ref, h_ref, r_ref):
    x = x_ref[...].astype(BF16)
    w_t = w_ref[...].astype(BF16)
    h_ref[...] = lax.dot_general(x, w_t, (((1,), (1,)), ((), ())), preferred_element_type=F32).astype(h_ref.dtype)

    @pl.when(pl.program_id(1) == 0)
    def _():
        r_ref[...] = _dot(x, wr_ref[...]).astype(r_ref.dtype)


def in_proj(xb, w_in_t, w_r_pad, layer, *, tm, tn):
    n, d = xb.shape
    return pl.pallas_call(
        _in_proj_kernel,
        grid=(n // tm, OFF_G_R // tn),
        in_specs=[pl.BlockSpec((tm, d), lambda i, j: (i, 0)),
                  pl.BlockSpec((None, tn, d), lambda i, j: (layer, j, 0)),
                  pl.BlockSpec((d, LANES), lambda i, j: (0, 0))],
        out_specs=[pl.BlockSpec((tm, tn), lambda i, j: (i, j)),
                   pl.BlockSpec((tm, LANES), lambda i, j: (i, 0))],
        out_shape=[jax.ShapeDtypeStruct((n, OFF_G_R), BF16), jax.ShapeDtypeStruct((n, LANES), BF16)],
        compiler_params=_params(("parallel", "arbitrary")),
        name="in_proj",
    )(xb, w_in_t, w_r_pad)


SB_BLK = 64
SB_WIN = 256
SB_GROUP = 16
SB_STACK = 2


LOG2E = 1.4426950408889634
SB_EXIT_LOG2 = SB_EXIT * LOG2E


def _minus_later_keys_matrix(n_keys):
    r = lax.broadcasted_iota(jnp.int32, (2 * n_keys, n_keys), 0)
    c = lax.broadcasted_iota(jnp.int32, (2 * n_keys, n_keys), 1)
    return jnp.where((r % n_keys) > c, -1.0, 0.0).astype(BF16)


def _sb_scores(q, k, scale):
    y = lax.dot_general(q, k, (((1,), (1,)), ((), ())), preferred_element_type=F32) * (scale * LOG2E)
    sp = jnp.maximum(y, 0.0) + jnp.log2(1.0 + jnp.exp2(-jnp.abs(y)))
    return y, sp


def _sb_tail(sp_masked, minus_later):
    hi, lo = _split_bf16(sp_masked)
    return _dot(jnp.concatenate([hi, lo], axis=1), minus_later)


def _sb_kernel(q_ref, k_ref, v_ref, g_ref, o_ref, acc_ref, c_ref, *, group, scale):
    minus_later_win = _minus_later_keys_matrix(SB_WIN)
    minus_later_blk = _minus_later_keys_matrix(SB_BLK)
    row = lax.broadcasted_iota(jnp.int32, (SB_BLK, SB_WIN), 0)
    col = lax.broadcasted_iota(jnp.int32, (SB_BLK, SB_WIN), 1)
    causal_full = col < row + (SB_WIN - SB_BLK)
    n_clamped = (SB_WIN - SB_BLK) // SB_BLK
    gain = g_ref[...]

    def q_rows(qi):
        return pl.ds(pl.multiple_of(qi * SB_BLK, SB_BLK), SB_BLK)

    def windows(qis):
        wss = [pl.multiple_of(jnp.maximum(qi - n_clamped, 0) * SB_BLK, SB_BLK) for qi in qis]
        causal = [col < row + (qi * SB_BLK - ws) for qi, ws in zip(qis[:n_clamped], wss)]
        causal += [causal_full] * (len(qis) - n_clamped)
        scores = [_sb_scores(q_ref[q_rows(qi), :], k_ref[pl.ds(ws, SB_WIN), :], scale) for qi, ws in zip(qis, wss)]
        sp_m = [jnp.where(m, sp, 0.0) for m, (_, sp) in zip(causal, scores)]
        tails = []
        for a in range(0, len(sp_m), SB_STACK):
            stacked = _sb_tail(jnp.concatenate(sp_m[a:a + SB_STACK], axis=0), minus_later_win)
            tails += [stacked[b * SB_BLK:(b + 1) * SB_BLK] for b in range(len(sp_m[a:a + SB_STACK]))]
        c_max = None
        for u, (m, (y, sp), s, tail, ws) in enumerate(zip(causal, scores, sp_m, tails, wss)):
            a = jnp.where(m, jnp.exp2(y - sp + tail), 0.0)
            acc_ref[u] = _dot(a.astype(BF16), v_ref[pl.ds(ws, SB_WIN), :])
            c = tail[:, 0:1] - s[:, 0:1]
            c_ref[u] = c
            if u <= n_clamped:
                c = jnp.where(ws > 0, c, -jnp.inf)
            c_max = c if c_max is None else jnp.maximum(c_max, c)
        return c_max

    def earlier_blocks(qi, u):
        q = q_ref[q_rows(qi), :]

        def body(carry):
            j, _ = carry
            start = pl.multiple_of(j * SB_BLK, SB_BLK)
            y, sp = _sb_scores(q, k_ref[pl.ds(start, SB_BLK), :], scale)
            tail = _sb_tail(sp, minus_later_blk)
            c = c_ref[u]
            a = jnp.exp2(y - sp + tail + c)
            acc_ref[u] += _dot(a.astype(BF16), v_ref[pl.ds(start, SB_BLK), :])
            c_new = c + tail[:, 0:1] - sp[:, 0:1]
            c_ref[u] = c_new
            return j - 1, jnp.max(c_new) > SB_EXIT_LOG2

        def cond(carry):
            j, more = carry
            return jnp.logical_and(j >= 0, more)

        lax.while_loop(cond, body, (jnp.maximum(qi - n_clamped, 0) - 1, jnp.max(c_ref[u]) > SB_EXIT_LOG2))

    def run_group(gi, carry):
        c_max = windows([gi * group + u for u in range(group)])

        @pl.when(jnp.max(c_max) > SB_EXIT_LOG2)
        def _():
            for u in range(group):
                earlier_blocks(gi * group + u, u)

        for u in range(group):
            o = acc_ref[u]
            ms = jnp.mean(o * o, axis=-1, keepdims=True)
            rows = pl.ds(pl.multiple_of((gi * group + u) * SB_BLK, SB_BLK), SB_BLK)
            o_ref[rows, :] = (o * lax.rsqrt(ms + RMS_EPS) * gain).astype(o_ref.dtype)
        return carry

    lax.fori_loop(0, q_ref.shape[0] // (SB_BLK * group), run_group, 0)


def sb_attention(h, norm_g, *, batch, seq):
    n = batch * seq
    qb, kb, vb = OFF_SB_Q // SB_DH, OFF_SB_K // SB_DH, OFF_SB_V // SB_DH
    group = min(SB_GROUP, seq // SB_BLK)
    kern = functools.partial(_sb_kernel, group=group, scale=SB_DH ** -0.5)
    return pl.pallas_call(
        kern,
        grid=(batch, SB_HEADS),
        in_specs=[pl.BlockSpec((seq, SB_DH), lambda b, hh: (b, qb + hh)),
                  pl.BlockSpec((seq, SB_DH), lambda b, hh: (b, kb + hh)),
                  pl.BlockSpec((seq, SB_DH), lambda b, hh: (b, vb + hh)),
                  pl.BlockSpec((1, SB_DH), lambda b, hh: (0, 0))],
        out_specs=pl.BlockSpec((seq, SB_DH), lambda b, hh: (b, hh)),
        out_shape=jax.ShapeDtypeStruct((n, SB_WIDTH), BF16),
        scratch_shapes=[pltpu.VMEM((group, SB_BLK, SB_DH), F32), pltpu.VMEM((group, SB_BLK, 1), F32)],
        compiler_params=_params(("parallel", "parallel")),
        name="sb_attention",
    )(h, h, h, norm_g.reshape(1, SB_DH))


def _conv_kernel(b_ref, c_ref, h_ref, w_ref, o_ref):
    u = c_ref[...].astype(F32) * h_ref[...].astype(F32)
    t = lax.broadcasted_iota(jnp.int32, u.shape, 0)
    u1 = jnp.where(t >= 1, pltpu.roll(u, 1, 0), 0.0)
    u2 = jnp.where(t >= 2, pltpu.roll(u, 2, 0), 0.0)
    w = w_ref[...]
    y = w[0:1, :] * u2 + w[1:2, :] * u1 + w[2:3, :] * u
    o_ref[...] = (b_ref[...].astype(F32) * y).astype(o_ref.dtype)


def gated_conv(h, conv_w, *, batch, seq):
    n = batch * seq
    nb = CONV_WIDTH // LANES
    bb, cb, hb = OFF_CV_B // LANES, OFF_CV_C // LANES, OFF_CV_H // LANES
    return pl.pallas_call(
        _conv_kernel,
        grid=(batch, nb),
        in_specs=[pl.BlockSpec((seq, LANES), lambda b, j: (b, bb + j)),
                  pl.BlockSpec((seq, LANES), lambda b, j: (b, cb + j)),
                  pl.BlockSpec((seq, LANES), lambda b, j: (b, hb + j)),
                  pl.BlockSpec((CONV_K, LANES), lambda b, j: (0, j))],
        out_specs=pl.BlockSpec((seq, LANES), lambda b, j: (b, j)),
        out_shape=jax.ShapeDtypeStruct((n, CONV_WIDTH), BF16),
        compiler_params=_params(("parallel", "parallel")),
        name="gated_conv",
    )(h, h, h, conv_w)


GLA_SEG = 128
GLA_SEGS_PER_STEP = 4


def _gla_kernel(q_ref, k_ref, v_ref, g_ref, r_ref, wup_ref, b_ref, gn_ref, o_ref, s_ref, *, n_seg):
    seg_len, chunk = GLA_SEG, GLA_CHUNK
    n_chunk = seg_len // chunk
    two_dk, two_dv = 2 * GLA_DK, 2 * GLA_DV

    row = lax.broadcasted_iota(jnp.int32, (seg_len, seg_len), 0)
    col = lax.broadcasted_iota(jnp.int32, (seg_len, seg_len), 1)
    same = (row // chunk) == (col // chunk)
    cum_m = jnp.logical_and(same, col <= row).astype(BF16)
    tot_m = same.astype(BF16)
    r2 = lax.broadcasted_iota(jnp.int32, (two_dk, two_dv), 0)
    c2 = lax.broadcasted_iota(jnp.int32, (two_dk, two_dv), 1)
    head_bd = (r2 // GLA_DK) == (c2 // GLA_DV)
    ones_bd = head_bd.astype(BF16)
    bd_f = head_bd.astype(F32)
    half = chunk // 2
    band_same = jnp.where(jnp.logical_and(row // half == col // half, col <= row), row - col, -1)
    ri = lax.broadcasted_iota(jnp.int32, (seg_len // 2, seg_len), 0)
    ci = lax.broadcasted_iota(jnp.int32, (seg_len // 2, seg_len), 1)
    in_lower = jnp.logical_and(ci // chunk == ri // half, ci % chunk < half)
    band_cross = jnp.where(in_lower, (ri - ci + chunk) % half, -1)
    lane_chunk = col // chunk

    wup = wup_ref[...]
    bias = b_ref[...]
    gn = gn_ref[...]
    s_ref[...] = jnp.zeros_like(s_ref)

    def rms(x):
        return x * lax.rsqrt(jnp.mean(x * x, axis=-1, keepdims=True) + RMS_EPS)

    def gates(rows):
        u = _dot(r_ref[rows, :], wup) + bias
        la = (jnp.minimum(u, 0.0) - jnp.log1p(jnp.exp(-jnp.abs(u)))) * (1.0 / GLA_TAU)
        hi, lo = _split_bf16(la)
        return _dot(cum_m, hi) + _dot(cum_m, lo), _dot(tot_m, hi) + _dot(tot_m, lo)

    def intra(q, k, v, bcum):
        n_half = seg_len // half
        q3, k3, b3 = (a.reshape(n_half, half, two_dk) for a in (q, k, bcum))
        score = _dot((q * k).astype(BF16), ones_bd)
        attn = [jnp.where(band_same == 0, score[:, :GLA_DV], 0.0), jnp.where(band_same == 0, score[:, GLA_DV:], 0.0)]
        for rho in range(1, half):
            decay = jnp.exp(b3 - pltpu.roll(b3, rho, 1))
            p = (q3 * pltpu.roll(k3, rho, 1) * decay).reshape(seg_len, two_dk)
            score = _dot(p.astype(BF16), ones_bd)
            attn = [jnp.where(band_same == rho, score[:, :GLA_DV], attn[0]),
                    jnp.where(band_same == rho, score[:, GLA_DV:], attn[1])]

        q4, k4, b4 = (a.reshape(n_chunk, 2, half, two_dk) for a in (q, k, bcum))
        q_up, b_up, k_lo, b_lo = q4[:, 1], b4[:, 1], k4[:, 0], b4[:, 0]
        cross = [jnp.zeros((n_chunk * half, seg_len), F32)] * 2
        for rho in range(half):
            k_r = pltpu.roll(k_lo, rho, 1) if rho else k_lo
            b_r = pltpu.roll(b_lo, rho, 1) if rho else b_lo
            p = (q_up * k_r * jnp.exp(b_up - b_r)).reshape(n_chunk * half, two_dk)
            score = _dot(p.astype(BF16), ones_bd)
            cross = [jnp.where(band_cross == rho, score[:, :GLA_DV], cross[0]),
                     jnp.where(band_cross == rho, score[:, GLA_DV:], cross[1])]

        def merged(same, up_lo):
            a4 = same.reshape(n_chunk, 2, half, seg_len)
            upper = a4[:, 1] + up_lo.reshape(n_chunk, half, seg_len)
            return jnp.stack([a4[:, 0], upper], axis=1).reshape(seg_len, seg_len).astype(BF16)

        return jnp.concatenate([_dot(merged(attn[0], cross[0]), v[:, :GLA_DV]),
                                _dot(merged(attn[1], cross[1]), v[:, GLA_DV:])], axis=1)

    def seg_group(gi, carry):
        all_rows = [pl.ds(pl.multiple_of((gi * GLA_SEGS_PER_STEP + s) * seg_len, seg_len), seg_len)
                    for s in range(GLA_SEGS_PER_STEP)]
        qs = [q_ref[rows, :].astype(F32) * (GLA_DK ** -0.5) for rows in all_rows]
        ks = [k_ref[rows, :].astype(F32) for rows in all_rows]
        vs = [v_ref[rows, :] for rows in all_rows]
        bcums, blasts = zip(*[gates(rows) for rows in all_rows])
        o_accs = [intra(q, k, v, bcum) for q, k, v, bcum in zip(qs, ks, vs, bcums)]

        q_ins = [(q * jnp.exp(bcum)).astype(BF16) for q, bcum in zip(qs, bcums)]
        k_up_ts = [(k * jnp.exp(blast - bcum)).T for k, bcum, blast in zip(ks, bcums, blasts)]
        g_ts = [jnp.exp(blast.T) for blast in blasts]
        kvs = [[_dot(jnp.where(lane_chunk == c, k_up_t, 0.0).astype(BF16), v) for c in range(n_chunk)]
               for k_up_t, v in zip(k_up_ts, vs)]
        state = s_ref[...]
        states = []
        for g_t, kv in zip(g_ts, kvs):
            seg_states = []
            for c in range(n_chunk):
                seg_states.append(state.astype(BF16))
                state = g_t[:, c * chunk:c * chunk + 1] * state + kv[c] * bd_f
            states.append(seg_states)
        s_ref[...] = state

        for rows, o_acc, q_in, seg_states in zip(all_rows, o_accs, q_ins, states):
            o_inter = [_dot(q_in[c * chunk:(c + 1) * chunk, :], seg_states[c]) for c in range(n_chunk)]
            o_acc = o_acc + jnp.concatenate(o_inter, axis=0)
            gate = g_ref[rows, :].astype(F32)
            o0 = rms(o_acc[:, :GLA_DV]) * gn
            o1 = rms(o_acc[:, GLA_DV:]) * gn
            out = jnp.concatenate([o0, o1], axis=1) * _silu(gate)
            o_ref[rows, :] = out.astype(o_ref.dtype)
        return carry

    lax.fori_loop(0, n_seg // GLA_SEGS_PER_STEP, seg_group, 0)


def gla(h, h_r, w_up_pad, gla_b, gla_norm_g, *, batch, seq):
    n = batch * seq
    pairs = GLA_HEADS // 2
    qb, kb = OFF_G_Q // LANES, OFF_G_K // LANES
    vb, gb = OFF_G_V // (2 * GLA_DV), OFF_G_G // (2 * GLA_DV)
    kern = functools.partial(_gla_kernel, n_seg=seq // GLA_SEG)
    return pl.pallas_call(
        kern,
        grid=(batch, pairs),
        in_specs=[pl.BlockSpec((seq, LANES), lambda b, p: (b, qb + p)),
                  pl.BlockSpec((seq, LANES), lambda b, p: (b, kb + p)),
                  pl.BlockSpec((seq, 2 * GLA_DV), lambda b, p: (b, vb + p)),
                  pl.BlockSpec((seq, 2 * GLA_DV), lambda b, p: (b, gb + p)),
                  pl.BlockSpec((seq, LANES), lambda b, p: (b, 0)),
                  pl.BlockSpec((LANES, LANES), lambda b, p: (0, p)),
                  pl.BlockSpec((1, LANES), lambda b, p: (0, p)),
                  pl.BlockSpec((1, GLA_DV), lambda b, p: (0, 0))],
        out_specs=pl.BlockSpec((seq, 2 * GLA_DV), lambda b, p: (b, p)),
        out_shape=jax.ShapeDtypeStruct((n, GLA_VW), BF16),
        scratch_shapes=[pltpu.VMEM((2 * GLA_DK, 2 * GLA_DV), F32)],
        compiler_params=_params(("parallel", "parallel")),
        name="gla",
    )(h, h, h, h, h_r, w_up_pad, gla_b.reshape(1, GLA_KW), gla_norm_g.reshape(1, GLA_DV))


def _layer_norm(y, g, b):
    mu = jnp.mean(y, axis=-1, keepdims=True)
    yc = y - mu
    var = jnp.mean(yc * yc, axis=-1, keepdims=True)
    return yc * lax.rsqrt(var + LN_EPS) * g + b


def _top2_route(logits):
    col = lax.broadcasted_iota(jnp.int32, logits.shape, 1)
    neg = jnp.float32(-jnp.inf)
    l1 = jnp.where(col < N_EXPERTS, logits, neg)
    m1 = jnp.max(l1, axis=-1, keepdims=True)
    i1 = jnp.min(jnp.where(l1 == m1, col, LANES), axis=-1, keepdims=True)
    l2 = jnp.where(col == i1, neg, l1)
    m2 = jnp.max(l2, axis=-1, keepdims=True)
    i2 = jnp.min(jnp.where(l2 == m2, col, LANES), axis=-1, keepdims=True)
    e2 = jnp.exp(m2 - m1)
    w1 = 1.0 / (1.0 + e2)
    w2 = e2 / (1.0 + e2)
    out = jnp.where(col == 0, i1.astype(F32), 0.0) + jnp.where(col == 1, i2.astype(F32), 0.0)
    return out + jnp.where(col == 2, w1, 0.0) + jnp.where(col == 3, w2, 0.0)


def _outproj_ln_kernel(*refs, alpha, routed):
    if routed:
        sb_ref, cv_ref, gl_ref, w_ref, x_ref, g_ref, b_ref, wr_ref, xo_ref, xb_ref, route_ref, wb_ref = refs
    else:
        sb_ref, cv_ref, gl_ref, w_ref, x_ref, g_ref, b_ref, xo_ref, xb_ref, wb_ref = refs

    @pl.when(pl.program_id(0) == 0)
    def _():
        wb_ref[...] = w_ref[...].astype(BF16)

    mix = _dot(sb_ref[...], wb_ref[0:SB_WIDTH, :])
    mix = mix + _dot(cv_ref[...], wb_ref[SB_WIDTH:SB_WIDTH + CONV_WIDTH, :])
    mix = mix + _dot(gl_ref[...], wb_ref[SB_WIDTH + CONV_WIDTH:, :])
    out = _layer_norm(alpha * x_ref[...] + mix, g_ref[...], b_ref[...])
    xo_ref[...] = out
    xb_ref[...] = out.astype(BF16).reshape(xb_ref.shape)
    if routed:
        col = lax.broadcasted_iota(jnp.int32, route_ref.shape, 1)
        logits = jnp.zeros(route_ref.shape, F32)
        for e in range(N_EXPERTS):
            logit_e = jnp.sum(out * wr_ref[e:e + 1, :], axis=-1, keepdims=True)
            logits = jnp.where(col == e, logit_e, logits)
        route_ref[...] = _top2_route(logits)


def _token_major_spec(tm, d, tiled):
    if tiled:
        return pl.BlockSpec((tm, d // LANES, LANES), lambda i: (i, 0, 0))
    return pl.BlockSpec((tm, d), lambda i: (i, 0))


def _token_major_shape(n, d, tiled):
    return (n, d // LANES, LANES) if tiled else (n, d)


def outproj_ln(o_sb, o_cv, o_gla, w_out, layer, x, ln_g, ln_b, w_router, *, alpha, tm):
    n, d = x.shape
    routed = w_router is not None
    kern = functools.partial(_outproj_ln_kernel, alpha=alpha, routed=routed)
    row = lambda i: (i, 0)
    fixed = lambda i: (0, 0)
    in_specs = [pl.BlockSpec((tm, SB_WIDTH), row),
                pl.BlockSpec((tm, CONV_WIDTH), row),
                pl.BlockSpec((tm, GLA_VW), row),
                pl.BlockSpec((None, d, d), lambda i: (layer, 0, 0), pipeline_mode=pl.Buffered(1)),
                pl.BlockSpec((tm, d), row),
                pl.BlockSpec((1, d), fixed),
                pl.BlockSpec((1, d), fixed)]
    args = [o_sb, o_cv, o_gla, w_out, x, ln_g.reshape(1, d), ln_b.reshape(1, d)]
    out_specs = [pl.BlockSpec((tm, d), row), _token_major_spec(tm, d, routed)]
    out_shape = [jax.ShapeDtypeStruct((n, d), F32), jax.ShapeDtypeStruct(_token_major_shape(n, d, routed), BF16)]
    if routed:
        in_specs.append(pl.BlockSpec((N_EXPERTS, d), fixed))
        args.append(w_router.T)
        out_specs.append(pl.BlockSpec((tm, LANES), row))
        out_shape.append(jax.ShapeDtypeStruct((n, LANES), F32))
    return pl.pallas_call(
        kern,
        grid=(n // tm,),
        in_specs=in_specs,
        out_specs=out_specs,
        out_shape=out_shape,
        scratch_shapes=[pltpu.VMEM((d, d), BF16)],
        compiler_params=_params(("arbitrary",)),
        name="outproj_ln",
    )(*args)


def _residual_ln_kernel(x_ref, f_ref, g_ref, b_ref, xo_ref, xb_ref, *, alpha):
    f = f_ref[...].reshape(x_ref.shape).astype(F32)
    out = _layer_norm(alpha * x_ref[...] + f, g_ref[...], b_ref[...])
    xo_ref[...] = out
    xb_ref[...] = out.astype(BF16)


def residual_ln(x, f, ln_g, ln_b, *, alpha, tm):
    n, d = x.shape
    kern = functools.partial(_residual_ln_kernel, alpha=alpha)
    row = lambda i: (i, 0)
    fixed = lambda i: (0, 0)
    return pl.pallas_call(
        kern,
        grid=(n // tm,),
        in_specs=[pl.BlockSpec((tm, d), row), _token_major_spec(tm, d, True),
                  pl.BlockSpec((1, d), fixed), pl.BlockSpec((1, d), fixed)],
        out_specs=[pl.BlockSpec((tm, d), row), pl.BlockSpec((tm, d), row)],
        out_shape=[jax.ShapeDtypeStruct((n, d), F32), jax.ShapeDtypeStruct((n, d), BF16)],
        compiler_params=_params(("parallel",)),
        name="residual_ln",
    )(x, f, ln_g.reshape(1, d), ln_b.reshape(1, d))


FFN_TM = 1024
FFN_TF = 256
FFN_SUB = 128


def _swiglu_chunk(x, wg_ref, wu_ref, wd_ref):
    hg = _dot(x, wg_ref[...].astype(BF16))
    hu = _dot(x, wu_ref[...].astype(BF16))
    hidden = (_silu(hg) * hu).astype(BF16)
    return _dot(hidden, wd_ref[...].astype(BF16))


def _ffn_kernel(te_ref, rows_ref, x_ref, wg_ref, wu_ref, wd_ref, o_ref, acc_ref, *, row_counts):
    i, f = pl.program_id(0), pl.program_id(1)

    @pl.when(f == 0)
    def _():
        acc_ref[...] = jnp.zeros_like(acc_ref)

    for m in row_counts:
        @pl.when(rows_ref[i] == m)
        def _(m=m):
            acc_ref[0:m, :] += _swiglu_chunk(x_ref[0:m, :], wg_ref, wu_ref, wd_ref)

    @pl.when(f == pl.num_programs(1) - 1)
    def _():
        o_ref[...] = acc_ref[...].astype(BF16).reshape(o_ref.shape)


def _ffn_gather_kernel(te_ref, rows_ref, idx_ref, nxt_ref, x_hbm, wg_ref, wu_ref, wd_ref, o_ref,
                       acc_ref, xbuf, x2d, sem, *, row_counts, per_step):
    i, f = pl.program_id(0), pl.program_id(1)
    n_i, n_f = pl.num_programs(0), pl.num_programs(1)
    tm, d = x2d.shape
    n_rows = xbuf.shape[1]
    slot = i % 2

    def row_copy(ref, s, r):
        return pltpu.make_async_copy(x_hbm.at[ref[0, 0, r]], xbuf.at[s, r], sem.at[s])

    def for_all_rows(fn):
        def body(r, c):
            fn(r)
            return c
        lax.fori_loop(0, n_rows, body, 0, unroll=DMA_UNROLL)

    @pl.when(f == 0)
    def _():
        acc_ref[...] = jnp.zeros_like(acc_ref)

        @pl.when(i == 0)
        def _():
            for_all_rows(lambda r: row_copy(idx_ref, slot, r).start())

        @pl.when(jnp.logical_or(i == 0, rows_ref[jnp.maximum(i - 1, 0)] > 0))
        def _():
            for_all_rows(lambda r: row_copy(idx_ref, slot, r).wait())
            x2d[...] = xbuf[slot, 0:tm].reshape(tm, d)

    for m in row_counts:
        @pl.when(rows_ref[i] == m)
        def _(m=m):
            for u in range(per_step):
                row_copy(nxt_ref, 1 - slot, f * per_step + u).start()
            acc_ref[0:m, :] += _swiglu_chunk(x2d[0:m, :], wg_ref, wu_ref, wd_ref)

    @pl.when(f == n_f - 1)
    def _():
        o_ref[...] = acc_ref[...].astype(BF16).reshape(o_ref.shape)

        @pl.when(jnp.logical_and(i == n_i - 1, rows_ref[i] > 0))
        def _():
            for_all_rows(lambda r: row_copy(nxt_ref, 1 - slot, r).wait())


def ffn(xb, w_gate, w_up, w_down, layer, tile_expert, tile_rows, *, tm, tf, row_counts):
    r, d = xb.shape
    d_ff = w_gate.shape[-1]
    n_f = d_ff // tf
    per_expert = w_gate.ndim == 4

    def chunk(i, f, rows):
        return jnp.where(rows[i] > 0, f, n_f - 1)

    def w_col(i, f, te, rows):
        return (layer, te[i], 0, chunk(i, f, rows)) if per_expert else (layer, 0, chunk(i, f, rows))

    def w_row(i, f, te, rows):
        return (layer, te[i], chunk(i, f, rows), 0) if per_expert else (layer, chunk(i, f, rows), 0)

    lead = (None, None) if per_expert else (None,)
    grid_spec = pltpu.PrefetchScalarGridSpec(
        num_scalar_prefetch=2,
        grid=(r // tm, n_f),
        in_specs=[pl.BlockSpec((tm, d), lambda i, f, te, rows: (i, 0)),
                  pl.BlockSpec((*lead, d, tf), w_col),
                  pl.BlockSpec((*lead, d, tf), w_col),
                  pl.BlockSpec((*lead, tf, d), w_row)],
        out_specs=pl.BlockSpec((tm, d // LANES, LANES), lambda i, f, te, rows: (i, 0, 0)),
        scratch_shapes=[pltpu.VMEM((tm, d), F32)],
    )
    return pl.pallas_call(
        functools.partial(_ffn_kernel, row_counts=row_counts),
        grid_spec=grid_spec,
        out_shape=jax.ShapeDtypeStruct((r, d // LANES, LANES), BF16),
        compiler_params=_params(("parallel", "arbitrary")),
        name="ffn",
    )(tile_expert, tile_rows, xb, w_gate, w_up, w_down)


def ffn_gathered(x3, src, w_gate, w_up, w_down, layer, tile_expert, tile_rows, *, tm, tf, row_counts):
    _, s_dim, l_dim = x3.shape
    d = s_dim * l_dim
    n_tiles, _, n_rows = src.shape
    d_ff = w_gate.shape[-1]
    n_f = d_ff // tf

    def chunk(i, f, rows):
        return jnp.where(rows[i] > 0, f, n_f - 1)

    smem = dict(memory_space=pltpu.SMEM)
    grid_spec = pltpu.PrefetchScalarGridSpec(
        num_scalar_prefetch=2,
        grid=(n_tiles, n_f),
        in_specs=[pl.BlockSpec((1, 1, n_rows), lambda i, f, te, rows: (i, 0, 0), **smem),
                  pl.BlockSpec((1, 1, n_rows), lambda i, f, te, rows: (jnp.minimum(i + 1, n_tiles - 1), 0, 0), **smem),
                  pl.BlockSpec(memory_space=pl.ANY),
                  pl.BlockSpec((None, None, d, tf), lambda i, f, te, rows: (layer, te[i], 0, chunk(i, f, rows))),
                  pl.BlockSpec((None, None, d, tf), lambda i, f, te, rows: (layer, te[i], 0, chunk(i, f, rows))),
                  pl.BlockSpec((None, None, tf, d), lambda i, f, te, rows: (layer, te[i], chunk(i, f, rows), 0))],
        out_specs=pl.BlockSpec((tm, s_dim, l_dim), lambda i, f, te, rows: (i, 0, 0)),
        scratch_shapes=[pltpu.VMEM((tm, d), F32),
                        pltpu.VMEM((2, n_rows, s_dim, l_dim), BF16),
                        pltpu.VMEM((tm, d), BF16),
                        pltpu.SemaphoreType.DMA((2,))],
    )
    return pl.pallas_call(
        functools.partial(_ffn_gather_kernel, row_counts=row_counts, per_step=n_rows // n_f),
        grid_spec=grid_spec,
        out_shape=jax.ShapeDtypeStruct((n_tiles * tm, s_dim, l_dim), BF16),
        compiler_params=_params(("arbitrary", "arbitrary")),
        name="ffn_gathered",
    )(tile_expert, tile_rows, src, src, x3, w_gate, w_up, w_down)


DMA_UNROLL = 8


def _step_and_next_specs(n_steps, width):
    cur = pl.BlockSpec((1, 1, width), lambda i: (i, 0, 0), memory_space=pltpu.SMEM)
    nxt = pl.BlockSpec((1, 1, width), lambda i: (jnp.minimum(i + 1, n_steps - 1), 0, 0), memory_space=pltpu.SMEM)
    return cur, nxt


def _combine_ln_kernel(pos_ref, nxt_ref, w_ref, x_ref, y_hbm, g_ref, b_ref, xo_ref, xb_ref,
                       buf, sem, *, rows, alpha):
    i = pl.program_id(0)
    slot = i % 2

    def row_copy(ref, s, k, r):
        return pltpu.make_async_copy(y_hbm.at[ref[0, 0, k * rows + r]], buf.at[s, k, r], sem.at[s, k])

    def issue(ref, s):
        def body(r, c):
            row_copy(ref, s, 0, r).start()
            row_copy(ref, s, 1, r).start()
            return c
        lax.fori_loop(0, rows, body, 0, unroll=DMA_UNROLL)

    @pl.when(i == 0)
    def _():
        issue(pos_ref, slot)

    @pl.when(i + 1 < pl.num_programs(0))
    def _():
        issue(nxt_ref, 1 - slot)

    def wait(r, c):
        row_copy(pos_ref, slot, 0, r).wait()
        row_copy(pos_ref, slot, 1, r).wait()
        return c

    lax.fori_loop(0, rows, wait, 0, unroll=DMA_UNROLL)
    w = w_ref[...]
    ya = buf[slot, 0].reshape(x_ref.shape).astype(F32)
    yb = buf[slot, 1].reshape(x_ref.shape).astype(F32)
    y = w[:, 2:3] * ya + w[:, 3:4] * yb
    out = _layer_norm(alpha * x_ref[...] + y, g_ref[...], b_ref[...])
    xo_ref[...] = out
    xb_ref[...] = out.astype(BF16)


def combine_ln(x, y, pos, route, ln_g, ln_b, *, alpha, rows):
    n, d = x.shape
    n_steps = n // rows
    kern = functools.partial(_combine_ln_kernel, rows=rows, alpha=alpha)
    row = lambda i: (i, 0)
    fixed = lambda i: (0, 0)
    return pl.pallas_call(
        kern,
        grid=(n_steps,),
        in_specs=[*_step_and_next_specs(n_steps, 2 * rows),
                  pl.BlockSpec((rows, LANES), row),
                  pl.BlockSpec((rows, d), row),
                  pl.BlockSpec(memory_space=pl.ANY),
                  pl.BlockSpec((1, d), fixed),
                  pl.BlockSpec((1, d), fixed)],
        out_specs=[pl.BlockSpec((rows, d), row), pl.BlockSpec((rows, d), row)],
        out_shape=[jax.ShapeDtypeStruct((n, d), F32), jax.ShapeDtypeStruct((n, d), BF16)],
        scratch_shapes=[pltpu.VMEM((2, 2, rows, d // LANES, LANES), BF16), pltpu.SemaphoreType.DMA((2, 2))],
        compiler_params=_params(("arbitrary",)),
        name="combine_ln",
    )(pos, pos, route, x, y, ln_g.reshape(1, d), ln_b.reshape(1, d))


def moe_block(xf, xb3, route, w_gate, w_up, w_down, layer, ln_g, ln_b, *, alpha, tm, tf, rows):
    n, d = xf.shape
    n_e = w_gate.shape[1]
    ef = route[:, 0:2].astype(jnp.int32).reshape(-1)
    onehot = (ef[:, None] == jnp.arange(n_e, dtype=jnp.int32)[None, :]).astype(jnp.int32)
    csum = jnp.cumsum(onehot, axis=0)
    rank = jnp.take_along_axis(csum, ef[:, None], axis=1)[:, 0] - 1
    counts = csum[-1]
    padded = ((counts + tm - 1) // tm) * tm
    ends = jnp.cumsum(padded)
    pos = (ends - padded)[ef] + rank
    r_rows = 2 * n + n_e * tm
    src = jnp.zeros((r_rows,), jnp.int32).at[pos].set(jnp.arange(2 * n, dtype=jnp.int32) // 2)
    tile_start = jnp.arange(r_rows // tm, dtype=jnp.int32) * tm
    te_raw = jnp.minimum(jnp.searchsorted(ends, tile_start, side="right").astype(jnp.int32), n_e - 1)
    last_e = te_raw[jnp.maximum(ends[-1] // tm - 1, 0)]
    tile_expert = jnp.where(tile_start < ends[-1], te_raw, last_e)
    counts_sub = ((counts + FFN_SUB - 1) // FFN_SUB) * FFN_SUB
    tile_rows = jnp.clip(((ends - padded) + counts_sub)[te_raw] - tile_start, 0, tm)

    n_f = w_gate.shape[-1] // tf
    n_req = -(-tm // n_f) * n_f
    src_tiles = jnp.pad(src.reshape(r_rows // tm, tm), ((0, 0), (0, n_req - tm))).reshape(r_rows // tm, 1, n_req)
    ys = ffn_gathered(xb3, src_tiles, w_gate, w_up, w_down, layer, tile_expert, tile_rows, tm=tm, tf=tf,
                      row_counts=tuple(range(FFN_SUB, tm + 1, FFN_SUB)))
    pos2 = pos.reshape(n // rows, rows, 2).transpose(0, 2, 1).reshape(n // rows, 1, 2 * rows)
    return combine_ln(xf, ys, pos2, route, ln_g, ln_b, alpha=alpha, rows=rows)


def kernel(x, w_in, sb_norm_g, conv_w, gla_w_up, gla_b, gla_norm_g, w_out, ln1_g, ln1_b,
           ffn_w_gate, ffn_w_up, ffn_w_down, moe_w_router, moe_w_gate, moe_w_up, moe_w_down,
           ln2_g, ln2_b):
    batch, seq, d = x.shape
    depth = w_in.shape[0]
    n = batch * seq
    alpha = (2 * depth) ** 0.25

    xf = x.reshape(n, d)
    xb = xf
    w_r_all = lax.optimization_barrier(w_in[:, :, OFF_G_R:])
    w_in_t = jnp.swapaxes(w_in, 1, 2)
    for layer in range(depth):
        w_r_pad = jnp.pad(w_r_all[layer], ((0, 0), (0, LANES - GLA_RANK))).astype(BF16)
        h, h_r = in_proj(xb, w_in_t, w_r_pad, layer, tm=IN_PROJ_TM, tn=IN_PROJ_TN)
        o_sb = sb_attention(h, sb_norm_g[layer], batch=batch, seq=seq)
        o_cv = gated_conv(h, conv_w[layer], batch=batch, seq=seq)
        w_up_pad = jnp.pad(gla_w_up[layer], ((0, LANES - GLA_RANK), (0, 0))).astype(BF16)
        o_gla = gla(h, h_r, w_up_pad, gla_b[layer], gla_norm_g[layer], batch=batch, seq=seq)
        j = layer // 2
        if layer % 2 == 1:
            xf, xb, route = outproj_ln(o_sb, o_cv, o_gla, w_out, layer, xf, ln1_g[layer], ln1_b[layer],
                                       moe_w_router[j], alpha=alpha, tm=ROW_TILE)
            xf, xb = moe_block(xf, xb, route, moe_w_gate, moe_w_up, moe_w_down, j,
                               ln2_g[layer], ln2_b[layer], alpha=alpha, tm=FFN_TM, tf=FFN_TF, rows=ROW_TILE)
        else:
            xf, xb = outproj_ln(o_sb, o_cv, o_gla, w_out, layer, xf, ln1_g[layer], ln1_b[layer], None,
                                alpha=alpha, tm=ROW_TILE)
            n_tiles = n // FFN_TM
            f = ffn(xb, ffn_w_gate, ffn_w_up, ffn_w_down, j, jnp.zeros((n_tiles,), jnp.int32),
                    jnp.full((n_tiles,), FFN_TM, jnp.int32), tm=FFN_TM, tf=FFN_TF, row_counts=(FFN_TM,))
            xf, xb = residual_ln(xf, f, ln2_g[layer], ln2_b[layer], alpha=alpha, tm=RESIDUAL_TM)
    return xf.reshape(batch, seq, d)
```

```python
import functools

import jax
import jax.numpy as jnp
from jax import lax
from jax.experimental import pallas as pl
from jax.experimental.pallas import tpu as pltpu

F32 = jnp.float32
BF16 = jnp.bfloat16

D_MODEL = 2048
SB_HEADS = 8
SB_DH = 128
SB_WIDTH = SB_HEADS * SB_DH
CONV_WIDTH = 512
CONV_K = 3
GLA_HEADS = 4
GLA_DK = 64
GLA_DV = 128
GLA_KW = GLA_HEADS * GLA_DK
GLA_VW = GLA_HEADS * GLA_DV
GLA_RANK = 16
GLA_TAU = 16.0
GLA_CHUNK = 16
IN_COLS = 3 * SB_WIDTH + 3 * CONV_WIDTH + 2 * GLA_KW + 2 * GLA_VW + GLA_RANK
N_EXPERTS = 8
LN_EPS = 1e-5
RMS_EPS = 1e-6

LANES = 128
VMEM_LIMIT = 56 * 1024 * 1024

OFF_SB_Q, OFF_SB_K, OFF_SB_V = 0, SB_WIDTH, 2 * SB_WIDTH
OFF_CV_B = 3 * SB_WIDTH
OFF_CV_C = OFF_CV_B + CONV_WIDTH
OFF_CV_H = OFF_CV_C + CONV_WIDTH
OFF_G_Q = OFF_CV_H + CONV_WIDTH
OFF_G_K = OFF_G_Q + GLA_KW
OFF_G_V = OFF_G_K + GLA_KW
OFF_G_G = OFF_G_V + GLA_VW
OFF_G_R = OFF_G_G + GLA_VW

SB_EXIT = -90.0

IN_PROJ_TM = 1024
IN_PROJ_TN = 1024
ROW_TILE = 256
RESIDUAL_TM = 512


def _params(sem):
    return pltpu.CompilerParams(dimension_semantics=sem, vmem_limit_bytes=VMEM_LIMIT)


def _split_bf16(x):
    hi = x.astype(BF16)
    lo = (x - hi.astype(F32)).astype(BF16)
    return hi, lo


def _dot(a, b):
    return jnp.dot(a, b, preferred_element_type=F32)


def _silu(x):
    return x * (1.0 / (1.0 + jnp.exp(-x)))


def _in_proj_kernel(x_ref, w_ref, wr_ref, h_ref, r_ref):
    x = x_ref[...].astype(BF16)
    w_t = w_ref[...].astype(BF16)
    h_ref[...] = lax.dot_general(x, w_t, (((1,), (1,)), ((), ())), preferred_element_type=F32).astype(h_ref.dtype)

    @pl.when(pl.program_id(1) == 0)
    def _():
        r_ref[...] = _dot(x, wr_ref[...]).astype(r_ref.dtype)


def in_proj(xb, w_in_t, w_r_pad, layer, *, tm, tn):
    n, d = xb.shape
    return pl.pallas_call(
        _in_proj_kernel,
        grid=(n // tm, OFF_G_R // tn),
        in_specs=[pl.BlockSpec((tm, d), lambda i, j: (i, 0)),
                  pl.BlockSpec((None, tn, d), lambda i, j: (layer, j, 0)),
                  pl.BlockSpec((d, LANES), lambda i, j: (0, 0))],
        out_specs=[pl.BlockSpec((tm, tn), lambda i, j: (i, j)),
                   pl.BlockSpec((tm, LANES), lambda i, j: (i, 0))],
        out_shape=[jax.ShapeDtypeStruct((n, OFF_G_R), BF16), jax.ShapeDtypeStruct((n, LANES), BF16)],
        compiler_params=_params(("parallel", "arbitrary")),
        name="in_proj",
    )(xb, w_in_t, w_r_pad)


SB_BLK = 64
SB_WIN = 256
SB_GROUP = 16
SB_STACK = 2


LOG2E = 1.4426950408889634
SB_EXIT_LOG2 = SB_EXIT * LOG2E


def _minus_later_keys_matrix(n_keys):
    r = lax.broadcasted_iota(jnp.int32, (2 * n_keys, n_keys), 0)
    c = lax.broadcasted_iota(jnp.int32, (2 * n_keys, n_keys), 1)
    return jnp.where((r % n_keys) > c, -1.0, 0.0).astype(BF16)


def _sb_scores(q, k, scale):
    y = lax.dot_general(q, k, (((1,), (1,)), ((), ())), preferred_element_type=F32) * (scale * LOG2E)
    sp = jnp.maximum(y, 0.0) + jnp.log2(1.0 + jnp.exp2(-jnp.abs(y)))
    return y, sp


def _sb_tail(sp_masked, minus_later):
    hi, lo = _split_bf16(sp_masked)
    return _dot(jnp.concatenate([hi, lo], axis=1), minus_later)


def _sb_kernel(q_ref, k_ref, v_ref, g_ref, o_ref, acc_ref, c_ref, *, group, scale):
    minus_later_win = _minus_later_keys_matrix(SB_WIN)
    minus_later_blk = _minus_later_keys_matrix(SB_BLK)
    row = lax.broadcasted_iota(jnp.int32, (SB_BLK, SB_WIN), 0)
    col = lax.broadcasted_iota(jnp.int32, (SB_BLK, SB_WIN), 1)
    causal_full = col < row + (SB_WIN - SB_BLK)
    n_clamped = (SB_WIN - SB_BLK) // SB_BLK
    gain = g_ref[...]

    def q_rows(qi):
        return pl.ds(pl.multiple_of(qi * SB_BLK, SB_BLK), SB_BLK)

    def windows(qis):
        wss = [pl.multiple_of(jnp.maximum(qi - n_clamped, 0) * SB_BLK, SB_BLK) for qi in qis]
        causal = [col < row + (qi * SB_BLK - ws) for qi, ws in zip(qis[:n_clamped], wss)]
        causal += [causal_full] * (len(qis) - n_clamped)
        scores = [_sb_scores(q_ref[q_rows(qi), :], k_ref[pl.ds(ws, SB_WIN), :], scale) for qi, ws in zip(qis, wss)]
        sp_m = [jnp.where(m, sp, 0.0) for m, (_, sp) in zip(causal, scores)]
        tails = []
        for a in range(0, len(sp_m), SB_STACK):
            stacked = _sb_tail(jnp.concatenate(sp_m[a:a + SB_STACK], axis=0), minus_later_win)
            tails += [stacked[b * SB_BLK:(b + 1) * SB_BLK] for b in range(len(sp_m[a:a + SB_STACK]))]
        c_max = None
        for u, (m, (y, sp), s, tail, ws) in enumerate(zip(causal, scores, sp_m, tails, wss)):
            a = jnp.where(m, jnp.exp2(y - sp + tail), 0.0)
            acc_ref[u] = _dot(a.astype(BF16), v_ref[pl.ds(ws, SB_WIN), :])
            c = tail[:, 0:1] - s[:, 0:1]
            c_ref[u] = c
            if u <= n_clamped:
                c = jnp.where(ws > 0, c, -jnp.inf)
            c_max = c if c_max is None else jnp.maximum(c_max, c)
        return c_max

    def earlier_blocks(qi, u):
        q = q_ref[q_rows(qi), :]

        def body(carry):
            j, _ = carry
            start = pl.multiple_of(j * SB_BLK, SB_BLK)
            y, sp = _sb_scores(q, k_ref[pl.ds(start, SB_BLK), :], scale)
            tail = _sb_tail(sp, minus_later_blk)
            c = c_ref[u]
            a = jnp.exp2(y - sp + tail + c)
            acc_ref[u] += _dot(a.astype(BF16), v_ref[pl.ds(start, SB_BLK), :])
            c_new = c + tail[:, 0:1] - sp[:, 0:1]
            c_ref[u] = c_new
            return j - 1, jnp.max(c_new) > SB_EXIT_LOG2

        def cond(carry):
            j, more = carry
            return jnp.logical_and(j >= 0, more)

        lax.while_loop(cond, body, (jnp.maximum(qi - n_clamped, 0) - 1, jnp.max(c_ref[u]) > SB_EXIT_LOG2))

    def run_group(gi, carry):
        c_max = windows([gi * group + u for u in range(group)])

        @pl.when(jnp.max(c_max) > SB_EXIT_LOG2)
        def _():
            for u in range(group):
                earlier_blocks(gi * group + u, u)

        for u in range(group):
            o = acc_ref[u]
            ms = jnp.mean(o * o, axis=-1, keepdims=True)
            rows = pl.ds(pl.multiple_of((gi * group + u) * SB_BLK, SB_BLK), SB_BLK)
            o_ref[rows, :] = (o * lax.rsqrt(ms + RMS_EPS) * gain).astype(o_ref.dtype)
        return carry

    lax.fori_loop(0, q_ref.shape[0] // (SB_BLK * group), run_group, 0)


def sb_attention(h, norm_g, *, batch, seq):
    n = batch * seq
    qb, kb, vb = OFF_SB_Q // SB_DH, OFF_SB_K // SB_DH, OFF_SB_V // SB_DH
    group = min(SB_GROUP, seq // SB_BLK)
    kern = functools.partial(_sb_kernel, group=group, scale=SB_DH ** -0.5)
    return pl.pallas_call(
        kern,
        grid=(batch, SB_HEADS),
        in_specs=[pl.BlockSpec((seq, SB_DH), lambda b, hh: (b, qb + hh)),
                  pl.BlockSpec((seq, SB_DH), lambda b, hh: (b, kb + hh)),
                  pl.BlockSpec((seq, SB_DH), lambda b, hh: (b, vb + hh)),
                  pl.BlockSpec((1, SB_DH), lambda b, hh: (0, 0))],
        out_specs=pl.BlockSpec((seq, SB_DH), lambda b, hh: (b, hh)),
        out_shape=jax.ShapeDtypeStruct((n, SB_WIDTH), BF16),
        scratch_shapes=[pltpu.VMEM((group, SB_BLK, SB_DH), F32), pltpu.VMEM((group, SB_BLK, 1), F32)],
        compiler_params=_params(("parallel", "parallel")),
        name="sb_attention",
    )(h, h, h, norm_g.reshape(1, SB_DH))


def _conv_kernel(b_ref, c_ref, h_ref, w_ref, o_ref):
    u = c_ref[...].astype(F32) * h_ref[...].astype(F32)
    t = lax.broadcasted_iota(jnp.int32, u.shape, 0)
    u1 = jnp.where(t >= 1, pltpu.roll(u, 1, 0), 0.0)
    u2 = jnp.where(t >= 2, pltpu.roll(u, 2, 0), 0.0)
    w = w_ref[...]
    y = w[0:1, :] * u2 + w[1:2, :] * u1 + w[2:3, :] * u
    o_ref[...] = (b_ref[...].astype(F32) * y).astype(o_ref.dtype)


def gated_conv(h, conv_w, *, batch, seq):
    n = batch * seq
    nb = CONV_WIDTH // LANES
    bb, cb, hb = OFF_CV_B // LANES, OFF_CV_C // LANES, OFF_CV_H // LANES
    return pl.pallas_call(
        _conv_kernel,
        grid=(batch, nb),
        in_specs=[pl.BlockSpec((seq, LANES), lambda b, j: (b, bb + j)),
                  pl.BlockSpec((seq, LANES), lambda b, j: (b, cb + j)),
                  pl.BlockSpec((seq, LANES), lambda b, j: (b, hb + j)),
                  pl.BlockSpec((CONV_K, LANES), lambda b, j: (0, j))],
        out_specs=pl.BlockSpec((seq, LANES), lambda b, j: (b, j)),
        out_shape=jax.ShapeDtypeStruct((n, CONV_WIDTH), BF16),
        compiler_params=_params(("parallel", "parallel")),
        name="gated_conv",
    )(h, h, h, conv_w)


GLA_SEG = 128
GLA_SEGS_PER_STEP = 4


def _gla_kernel(q_ref, k_ref, v_ref, g_ref, r_ref, wup_ref, b_ref, gn_ref, o_ref, s_ref, *, n_seg):
    seg_len, chunk = GLA_SEG, GLA_CHUNK
    n_chunk = seg_len // chunk
    two_dk, two_dv = 2 * GLA_DK, 2 * GLA_DV

    row = lax.broadcasted_iota(jnp.int32, (seg_len, seg_len), 0)
    col = lax.broadcasted_iota(jnp.int32, (seg_len, seg_len), 1)
    same = (row // chunk) == (col // chunk)
    cum_m = jnp.logical_and(same, col <= row).astype(BF16)
    tot_m = same.astype(BF16)
    r2 = lax.broadcasted_iota(jnp.int32, (two_dk, two_dv), 0)
    c2 = lax.broadcasted_iota(jnp.int32, (two_dk, two_dv), 1)
    head_bd = (r2 // GLA_DK) == (c2 // GLA_DV)
    ones_bd = head_bd.astype(BF16)
    bd_f = head_bd.astype(F32)
    half = chunk // 2
    band_same = jnp.where(jnp.logical_and(row // half == col // half, col <= row), row - col, -1)
    ri = lax.broadcasted_iota(jnp.int32, (seg_len // 2, seg_len), 0)
    ci = lax.broadcasted_iota(jnp.int32, (seg_len // 2, seg_len), 1)
    in_lower = jnp.logical_and(ci // chunk == ri // half, ci % chunk < half)
    band_cross = jnp.where(in_lower, (ri - ci + chunk) % half, -1)
    lane_chunk = col // chunk

    wup = wup_ref[...]
    bias = b_ref[...]
    gn = gn_ref[...]
    s_ref[...] = jnp.zeros_like(s_ref)

    def rms(x):
        return x * lax.rsqrt(jnp.mean(x * x, axis=-1, keepdims=True) + RMS_EPS)

    def gates(rows):
        u = _dot(r_ref[rows, :], wup) + bias
        la = (jnp.minimum(u, 0.0) - jnp.log1p(jnp.exp(-jnp.abs(u)))) * (1.0 / GLA_TAU)
        hi, lo = _split_bf16(la)
        return _dot(cum_m, hi) + _dot(cum_m, lo), _dot(tot_m, hi) + _dot(tot_m, lo)

    def head_scores(ps):
        rows = ps[0].shape[0]
        stacked = _dot(jnp.concatenate([p.astype(BF16) for p in ps], axis=0), ones_bd)
        return [stacked[i * rows:(i + 1) * rows] for i in range(len(ps))]

    def place(band, rho, scores, attns):
        return [[jnp.where(band == rho, s[:, :GLA_DV], a[0]), jnp.where(band == rho, s[:, GLA_DV:], a[1])]
                for s, a in zip(scores, attns)]

    def intra(qs, ks, vs, bcums):
        n_half = seg_len // half
        zero = [[0.0, 0.0]] * len(qs)
        thirds = [[a.reshape(n_half, half, two_dk) for a in (q, k, b)] for q, k, b in zip(qs, ks, bcums)]
        attn = place(band_same, 0, head_scores([q * k for q, k in zip(qs, ks)]), zero)
        for rho in range(1, half):
            ps = [(q3 * pltpu.roll(k3, rho, 1) * jnp.exp(b3 - pltpu.roll(b3, rho, 1))).reshape(seg_len, two_dk)
                  for q3, k3, b3 in thirds]
            attn = place(band_same, rho, head_scores(ps), attn)

        fourths = [[a.reshape(n_chunk, 2, half, two_dk) for a in (q, k, b)] for q, k, b in zip(qs, ks, bcums)]
        cross = zero
        for rho in range(half):
            ps = []
            for q4, k4, b4 in fourths:
                k_r = pltpu.roll(k4[:, 0], rho, 1) if rho else k4[:, 0]
                b_r = pltpu.roll(b4[:, 0], rho, 1) if rho else b4[:, 0]
                ps.append((q4[:, 1] * k_r * jnp.exp(b4[:, 1] - b_r)).reshape(n_chunk * half, two_dk))
            cross = place(band_cross, rho, head_scores(ps), cross)

        def merged(same, up_lo):
            a4 = same.reshape(n_chunk, 2, half, seg_len)
            upper = a4[:, 1] + up_lo.reshape(n_chunk, half, seg_len)
            return jnp.stack([a4[:, 0], upper], axis=1).reshape(seg_len, seg_len).astype(BF16)

        return [jnp.concatenate([_dot(merged(a[0], c[0]), v[:, :GLA_DV]),
                                 _dot(merged(a[1], c[1]), v[:, GLA_DV:])], axis=1)
                for a, c, v in zip(attn, cross, vs)]

    def seg_group(gi, carry):
        all_rows = [pl.ds(pl.multiple_of((gi * GLA_SEGS_PER_STEP + s) * seg_len, seg_len), seg_len)
                    for s in range(GLA_SEGS_PER_STEP)]
        qs = [q_ref[rows, :].astype(F32) * (GLA_DK ** -0.5) for rows in all_rows]
        ks = [k_ref[rows, :].astype(F32) for rows in all_rows]
        vs = [v_ref[rows, :] for rows in all_rows]
        bcums, blasts = zip(*[gates(rows) for rows in all_rows])
        o_accs = intra(qs, ks, vs, bcums)

        q_ins = [(q * jnp.exp(bcum)).astype(BF16) for q, bcum in zip(qs, bcums)]
        k_up_ts = [(k * jnp.exp(blast - bcum)).T for k, bcum, blast in zip(ks, bcums, blasts)]
        g_ts = [jnp.exp(blast.T) for blast in blasts]
        kvs = [[_dot(jnp.where(lane_chunk == c, k_up_t, 0.0).astype(BF16), v) for c in range(n_chunk)]
               for k_up_t, v in zip(k_up_ts, vs)]
        state = s_ref[...]
        states = []
        for g_t, kv in zip(g_ts, kvs):
            seg_states = []
            for c in range(n_chunk):
                seg_states.append(state.astype(BF16))
                state = g_t[:, c * chunk:c * chunk + 1] * state + kv[c] * bd_f
            states.append(seg_states)
        s_ref[...] = state

        for rows, o_acc, q_in, seg_states in zip(all_rows, o_accs, q_ins, states):
            o_inter = [_dot(q_in[c * chunk:(c + 1) * chunk, :], seg_states[c]) for c in range(n_chunk)]
            o_acc = o_acc + jnp.concatenate(o_inter, axis=0)
            gate = g_ref[rows, :].astype(F32)
            o0 = rms(o_acc[:, :GLA_DV]) * gn
            o1 = rms(o_acc[:, GLA_DV:]) * gn
            out = jnp.concatenate([o0, o1], axis=1) * _silu(gate)
            o_ref[rows, :] = out.astype(o_ref.dtype)
        return carry

    lax.fori_loop(0, n_seg // GLA_SEGS_PER_STEP, seg_group, 0)


def gla(h, h_r, w_up_pad, gla_b, gla_norm_g, *, batch, seq):
    n = batch * seq
    pairs = GLA_HEADS // 2
    qb, kb = OFF_G_Q // LANES, OFF_G_K // LANES
    vb, gb = OFF_G_V // (2 * GLA_DV), OFF_G_G // (2 * GLA_DV)
    kern = functools.partial(_gla_kernel, n_seg=seq // GLA_SEG)
    return pl.pallas_call(
        kern,
        grid=(batch, pairs),
        in_specs=[pl.BlockSpec((seq, LANES), lambda b, p: (b, qb + p)),
                  pl.BlockSpec((seq, LANES), lambda b, p: (b, kb + p)),
                  pl.BlockSpec((seq, 2 * GLA_DV), lambda b, p: (b, vb + p)),
                  pl.BlockSpec((seq, 2 * GLA_DV), lambda b, p: (b, gb + p)),
                  pl.BlockSpec((seq, LANES), lambda b, p: (b, 0)),
                  pl.BlockSpec((LANES, LANES), lambda b, p: (0, p)),
                  pl.BlockSpec((1, LANES), lambda b, p: (0, p)),
                  pl.BlockSpec((1, GLA_DV), lambda b, p: (0, 0))],
        out_specs=pl.BlockSpec((seq, 2 * GLA_DV), lambda b, p: (b, p)),
        out_shape=jax.ShapeDtypeStruct((n, GLA_VW), BF16),
        scratch_shapes=[pltpu.VMEM((2 * GLA_DK, 2 * GLA_DV), F32)],
        compiler_params=_params(("parallel", "parallel")),
        name="gla",
    )(h, h, h, h, h_r, w_up_pad, gla_b.reshape(1, GLA_KW), gla_norm_g.reshape(1, GLA_DV))


def _layer_norm(y, g, b):
    mu = jnp.mean(y, axis=-1, keepdims=True)
    yc = y - mu
    var = jnp.mean(yc * yc, axis=-1, keepdims=True)
    return yc * lax.rsqrt(var + LN_EPS) * g + b


def _top2_route(logits):
    col = lax.broadcasted_iota(jnp.int32, logits.shape, 1)
    neg = jnp.float32(-jnp.inf)
    l1 = jnp.where(col < N_EXPERTS, logits, neg)
    m1 = jnp.max(l1, axis=-1, keepdims=True)
    i1 = jnp.min(jnp.where(l1 == m1, col, LANES), axis=-1, keepdims=True)
    l2 = jnp.where(col == i1, neg, l1)
    m2 = jnp.max(l2, axis=-1, keepdims=True)
    i2 = jnp.min(jnp.where(l2 == m2, col, LANES), axis=-1, keepdims=True)
    e2 = jnp.exp(m2 - m1)
    w1 = 1.0 / (1.0 + e2)
    w2 = e2 / (1.0 + e2)
    out = jnp.where(col == 0, i1.astype(F32), 0.0) + jnp.where(col == 1, i2.astype(F32), 0.0)
    return out + jnp.where(col == 2, w1, 0.0) + jnp.where(col == 3, w2, 0.0)


def _outproj_ln_kernel(*refs, alpha, routed):
    if routed:
        sb_ref, cv_ref, gl_ref, w_ref, x_ref, g_ref, b_ref, wr_ref, xo_ref, xb_ref, route_ref, wb_ref = refs
    else:
        sb_ref, cv_ref, gl_ref, w_ref, x_ref, g_ref, b_ref, xo_ref, xb_ref, wb_ref = refs

    @pl.when(pl.program_id(0) == 0)
    def _():
        wb_ref[...] = w_ref[...].astype(BF16)

    mix = _dot(sb_ref[...], wb_ref[0:SB_WIDTH, :])
    mix = mix + _dot(cv_ref[...], wb_ref[SB_WIDTH:SB_WIDTH + CONV_WIDTH, :])
    mix = mix + _dot(gl_ref[...], wb_ref[SB_WIDTH + CONV_WIDTH:, :])
    out = _layer_norm(alpha * x_ref[...] + mix, g_ref[...], b_ref[...])
    xo_ref[...] = out
    xb_ref[...] = out.astype(BF16).reshape(xb_ref.shape)
    if routed:
        col = lax.broadcasted_iota(jnp.int32, route_ref.shape, 1)
        logits = jnp.zeros(route_ref.shape, F32)
        for e in range(N_EXPERTS):
            logit_e = jnp.sum(out * wr_ref[e:e + 1, :], axis=-1, keepdims=True)
            logits = jnp.where(col == e, logit_e, logits)
        route_ref[...] = _top2_route(logits)


def _token_major_spec(tm, d, tiled):
    if tiled:
        return pl.BlockSpec((tm, d // LANES, LANES), lambda i: (i, 0, 0))
    return pl.BlockSpec((tm, d), lambda i: (i, 0))


def _token_major_shape(n, d, tiled):
    return (n, d // LANES, LANES) if tiled else (n, d)


def outproj_ln(o_sb, o_cv, o_gla, w_out, layer, x, ln_g, ln_b, w_router, *, alpha, tm):
    n, d = x.shape
    routed = w_router is not None
    kern = functools.partial(_outproj_ln_kernel, alpha=alpha, routed=routed)
    row = lambda i: (i, 0)
    fixed = lambda i: (0, 0)
    in_specs = [pl.BlockSpec((tm, SB_WIDTH), row),
                pl.BlockSpec((tm, CONV_WIDTH), row),
                pl.BlockSpec((tm, GLA_VW), row),
                pl.BlockSpec((None, d, d), lambda i: (layer, 0, 0), pipeline_mode=pl.Buffered(1)),
                pl.BlockSpec((tm, d), row),
                pl.BlockSpec((1, d), fixed),
                pl.BlockSpec((1, d), fixed)]
    args = [o_sb, o_cv, o_gla, w_out, x, ln_g.reshape(1, d), ln_b.reshape(1, d)]
    out_specs = [pl.BlockSpec((tm, d), row), _token_major_spec(tm, d, routed)]
    out_shape = [jax.ShapeDtypeStruct((n, d), F32), jax.ShapeDtypeStruct(_token_major_shape(n, d, routed), BF16)]
    if routed:
        in_specs.append(pl.BlockSpec((N_EXPERTS, d), fixed))
        args.append(w_router.T)
        out_specs.append(pl.BlockSpec((tm, LANES), row))
        out_shape.append(jax.ShapeDtypeStruct((n, LANES), F32))
    return pl.pallas_call(
        kern,
        grid=(n // tm,),
        in_specs=in_specs,
        out_specs=out_specs,
        out_shape=out_shape,
        scratch_shapes=[pltpu.VMEM((d, d), BF16)],
        compiler_params=_params(("arbitrary",)),
        name="outproj_ln",
    )(*args)


def _residual_ln_kernel(x_ref, f_ref, g_ref, b_ref, xo_ref, xb_ref, *, alpha):
    f = f_ref[...].reshape(x_ref.shape).astype(F32)
    out = _layer_norm(alpha * x_ref[...] + f, g_ref[...], b_ref[...])
    xo_ref[...] = out
    xb_ref[...] = out.astype(BF16)


def residual_ln(x, f, ln_g, ln_b, *, alpha, tm):
    n, d = x.shape
    kern = functools.partial(_residual_ln_kernel, alpha=alpha)
    row = lambda i: (i, 0)
    fixed = lambda i: (0, 0)
    return pl.pallas_call(
        kern,
        grid=(n // tm,),
        in_specs=[pl.BlockSpec((tm, d), row), _token_major_spec(tm, d, True),
                  pl.BlockSpec((1, d), fixed), pl.BlockSpec((1, d), fixed)],
        out_specs=[pl.BlockSpec((tm, d), row), pl.BlockSpec((tm, d), row)],
        out_shape=[jax.ShapeDtypeStruct((n, d), F32), jax.ShapeDtypeStruct((n, d), BF16)],
        compiler_params=_params(("parallel",)),
        name="residual_ln",
    )(x, f, ln_g.reshape(1, d), ln_b.reshape(1, d))


FFN_TM = 1024
FFN_TF = 256
FFN_SUB = 128


def _swiglu_chunk(x, wg_ref, wu_ref, wd_ref):
    hg = _dot(x, wg_ref[...].astype(BF16))
    hu = _dot(x, wu_ref[...].astype(BF16))
    hidden = (_silu(hg) * hu).astype(BF16)
    return _dot(hidden, wd_ref[...].astype(BF16))


def _ffn_kernel(te_ref, rows_ref, x_ref, wg_ref, wu_ref, wd_ref, o_ref, acc_ref, *, row_counts):
    i, f = pl.program_id(0), pl.program_id(1)

    @pl.when(f == 0)
    def _():
        acc_ref[...] = jnp.zeros_like(acc_ref)

    for m in row_counts:
        @pl.when(rows_ref[i] == m)
        def _(m=m):
            acc_ref[0:m, :] += _swiglu_chunk(x_ref[0:m, :], wg_ref, wu_ref, wd_ref)

    @pl.when(f == pl.num_programs(1) - 1)
    def _():
        o_ref[...] = acc_ref[...].astype(BF16).reshape(o_ref.shape)


def _ffn_gather_kernel(te_ref, rows_ref, idx_ref, nxt_ref, x_hbm, wg_ref, wu_ref, wd_ref, o_ref,
                       acc_ref, xbuf, x2d, sem, *, row_counts, per_step):
    i, f = pl.program_id(0), pl.program_id(1)
    n_i, n_f = pl.num_programs(0), pl.num_programs(1)
    tm, d = x2d.shape
    n_rows = xbuf.shape[1]
    slot = i % 2

    def row_copy(ref, s, r):
        return pltpu.make_async_copy(x_hbm.at[ref[0, 0, r]], xbuf.at[s, r], sem.at[s])

    def for_all_rows(fn):
        def body(r, c):
            fn(r)
            return c
        lax.fori_loop(0, n_rows, body, 0, unroll=DMA_UNROLL)

    @pl.when(f == 0)
    def _():
        acc_ref[...] = jnp.zeros_like(acc_ref)

        @pl.when(i == 0)
        def _():
            for_all_rows(lambda r: row_copy(idx_ref, slot, r).start())

        @pl.when(jnp.logical_or(i == 0, rows_ref[jnp.maximum(i - 1, 0)] > 0))
        def _():
            for_all_rows(lambda r: row_copy(idx_ref, slot, r).wait())
            x2d[...] = xbuf[slot, 0:tm].reshape(tm, d)

    for m in row_counts:
        @pl.when(rows_ref[i] == m)
        def _(m=m):
            for u in range(per_step):
                row_copy(nxt_ref, 1 - slot, f * per_step + u).start()
            acc_ref[0:m, :] += _swiglu_chunk(x2d[0:m, :], wg_ref, wu_ref, wd_ref)

    @pl.when(f == n_f - 1)
    def _():
        o_ref[...] = acc_ref[...].astype(BF16).reshape(o_ref.shape)

        @pl.when(jnp.logical_and(i == n_i - 1, rows_ref[i] > 0))
        def _():
            for_all_rows(lambda r: row_copy(nxt_ref, 1 - slot, r).wait())


def ffn(xb, w_gate, w_up, w_down, layer, tile_expert, tile_rows, *, tm, tf, row_counts):
    r, d = xb.shape
    d_ff = w_gate.shape[-1]
    n_f = d_ff // tf
    per_expert = w_gate.ndim == 4

    def chunk(i, f, rows):
        return jnp.where(rows[i] > 0, f, n_f - 1)

    def w_col(i, f, te, rows):
        return (layer, te[i], 0, chunk(i, f, rows)) if per_expert else (layer, 0, chunk(i, f, rows))

    def w_row(i, f, te, rows):
        return (layer, te[i], chunk(i, f, rows), 0) if per_expert else (layer, chunk(i, f, rows), 0)

    lead = (None, None) if per_expert else (None,)
    grid_spec = pltpu.PrefetchScalarGridSpec(
        num_scalar_prefetch=2,
        grid=(r // tm, n_f),
        in_specs=[pl.BlockSpec((tm, d), lambda i, f, te, rows: (i, 0)),
                  pl.BlockSpec((*lead, d, tf), w_col),
                  pl.BlockSpec((*lead, d, tf), w_col),
                  pl.BlockSpec((*lead, tf, d), w_row)],
        out_specs=pl.BlockSpec((tm, d // LANES, LANES), lambda i, f, te, rows: (i, 0, 0)),
        scratch_shapes=[pltpu.VMEM((tm, d), F32)],
    )
    return pl.pallas_call(
        functools.partial(_ffn_kernel, row_counts=row_counts),
        grid_spec=grid_spec,
        out_shape=jax.ShapeDtypeStruct((r, d // LANES, LANES), BF16),
        compiler_params=_params(("parallel", "arbitrary")),
        name="ffn",
    )(tile_expert, tile_rows, xb, w_gate, w_up, w_down)


def ffn_gathered(x3, src, w_gate, w_up, w_down, layer, tile_expert, tile_rows, *, tm, tf, row_counts):
    _, s_dim, l_dim = x3.shape
    d = s_dim * l_dim
    n_tiles, _, n_rows = src.shape
    d_ff = w_gate.shape[-1]
    n_f = d_ff // tf

    def chunk(i, f, rows):
        return jnp.where(rows[i] > 0, f, n_f - 1)

    smem = dict(memory_space=pltpu.SMEM)
    grid_spec = pltpu.PrefetchScalarGridSpec(
        num_scalar_prefetch=2,
        grid=(n_tiles, n_f),
        in_specs=[pl.BlockSpec((1, 1, n_rows), lambda i, f, te, rows: (i, 0, 0), **smem),
                  pl.BlockSpec((1, 1, n_rows), lambda i, f, te, rows: (jnp.minimum(i + 1, n_tiles - 1), 0, 0), **smem),
                  pl.BlockSpec(memory_space=pl.ANY),
                  pl.BlockSpec((None, None, d, tf), lambda i, f, te, rows: (layer, te[i], 0, chunk(i, f, rows))),
                  pl.BlockSpec((None, None, d, tf), lambda i, f, te, rows: (layer, te[i], 0, chunk(i, f, rows))),
                  pl.BlockSpec((None, None, tf, d), lambda i, f, te, rows: (layer, te[i], chunk(i, f, rows), 0))],
        out_specs=pl.BlockSpec((tm, s_dim, l_dim), lambda i, f, te, rows: (i, 0, 0)),
        scratch_shapes=[pltpu.VMEM((tm, d), F32),
                        pltpu.VMEM((2, n_rows, s_dim, l_dim), BF16),
                        pltpu.VMEM((tm, d), BF16),
                        pltpu.SemaphoreType.DMA((2,))],
    )
    return pl.pallas_call(
        functools.partial(_ffn_gather_kernel, row_counts=row_counts, per_step=n_rows // n_f),
        grid_spec=grid_spec,
        out_shape=jax.ShapeDtypeStruct((n_tiles * tm, s_dim, l_dim), BF16),
        compiler_params=_params(("arbitrary", "arbitrary")),
        name="ffn_gathered",
    )(tile_expert, tile_rows, src, src, x3, w_gate, w_up, w_down)


DMA_UNROLL = 8


def _step_and_next_specs(n_steps, width):
    cur = pl.BlockSpec((1, 1, width), lambda i: (i, 0, 0), memory_space=pltpu.SMEM)
    nxt = pl.BlockSpec((1, 1, width), lambda i: (jnp.minimum(i + 1, n_steps - 1), 0, 0), memory_space=pltpu.SMEM)
    return cur, nxt


def _combine_ln_kernel(pos_ref, nxt_ref, w_ref, x_ref, y_hbm, g_ref, b_ref, xo_ref, xb_ref,
                       buf, sem, *, rows, alpha):
    i = pl.program_id(0)
    slot = i % 2

    def row_copy(ref, s, k, r):
        return pltpu.make_async_copy(y_hbm.at[ref[0, 0, k * rows + r]], buf.at[s, k, r], sem.at[s, k])

    def issue(ref, s):
        def body(r, c):
            row_copy(ref, s, 0, r).start()
            row_copy(ref, s, 1, r).start()
            return c
        lax.fori_loop(0, rows, body, 0, unroll=DMA_UNROLL)

    @pl.when(i == 0)
    def _():
        issue(pos_ref, slot)

    @pl.when(i + 1 < pl.num_programs(0))
    def _():
        issue(nxt_ref, 1 - slot)

    def wait(r, c):
        row_copy(pos_ref, slot, 0, r).wait()
        row_copy(pos_ref, slot, 1, r).wait()
        return c

    lax.fori_loop(0, rows, wait, 0, unroll=DMA_UNROLL)
    w = w_ref[...]
    ya = buf[slot, 0].reshape(x_ref.shape).astype(F32)
    yb = buf[slot, 1].reshape(x_ref.shape).astype(F32)
    y = w[:, 2:3] * ya + w[:, 3:4] * yb
    out = _layer_norm(alpha * x_ref[...] + y, g_ref[...], b_ref[...])
    xo_ref[...] = out
    xb_ref[...] = out.astype(BF16)


def combine_ln(x, y, pos, route, ln_g, ln_b, *, alpha, rows):
    n, d = x.shape
    n_steps = n // rows
    kern = functools.partial(_combine_ln_kernel, rows=rows, alpha=alpha)
    row = lambda i: (i, 0)
    fixed = lambda i: (0, 0)
    return pl.pallas_call(
        kern,
        grid=(n_steps,),
        in_specs=[*_step_and_next_specs(n_steps, 2 * rows),
                  pl.BlockSpec((rows, LANES), row),
                  pl.BlockSpec((rows, d), row),
                  pl.BlockSpec(memory_space=pl.ANY),
                  pl.BlockSpec((1, d), fixed),
                  pl.BlockSpec((1, d), fixed)],
        out_specs=[pl.BlockSpec((rows, d), row), pl.BlockSpec((rows, d), row)],
        out_shape=[jax.ShapeDtypeStruct((n, d), F32), jax.ShapeDtypeStruct((n, d), BF16)],
        scratch_shapes=[pltpu.VMEM((2, 2, rows, d // LANES, LANES), BF16), pltpu.SemaphoreType.DMA((2, 2))],
        compiler_params=_params(("arbitrary",)),
        name="combine_ln",
    )(pos, pos, route, x, y, ln_g.reshape(1, d), ln_b.reshape(1, d))


def moe_block(xf, xb3, route, w_gate, w_up, w_down, layer, ln_g, ln_b, *, alpha, tm, tf, rows):
    n, d = xf.shape
    n_e = w_gate.shape[1]
    ef = route[:, 0:2].astype(jnp.int32).reshape(-1)
    onehot = (ef[:, None] == jnp.arange(n_e, dtype=jnp.int32)[None, :]).astype(jnp.int32)
    csum = jnp.cumsum(onehot, axis=0)
    rank = jnp.take_along_axis(csum, ef[:, None], axis=1)[:, 0] - 1
    counts = csum[-1]
    padded = ((counts + tm - 1) // tm) * tm
    ends = jnp.cumsum(padded)
    pos = (ends - padded)[ef] + rank
    r_rows = 2 * n + n_e * tm
    src = jnp.zeros((r_rows,), jnp.int32).at[pos].set(jnp.arange(2 * n, dtype=jnp.int32) // 2)
    tile_start = jnp.arange(r_rows // tm, dtype=jnp.int32) * tm
    te_raw = jnp.minimum(jnp.searchsorted(ends, tile_start, side="right").astype(jnp.int32), n_e - 1)
    last_e = te_raw[jnp.maximum(ends[-1] // tm - 1, 0)]
    tile_expert = jnp.where(tile_start < ends[-1], te_raw, last_e)
    counts_sub = ((counts + FFN_SUB - 1) // FFN_SUB) * FFN_SUB
    tile_rows = jnp.clip(((ends - padded) + counts_sub)[te_raw] - tile_start, 0, tm)

    n_f = w_gate.shape[-1] // tf
    n_req = -(-tm // n_f) * n_f
    src_tiles = jnp.pad(src.reshape(r_rows // tm, tm), ((0, 0), (0, n_req - tm))).reshape(r_rows // tm, 1, n_req)
    ys = ffn_gathered(xb3, src_tiles, w_gate, w_up, w_down, layer, tile_expert, tile_rows, tm=tm, tf=tf,
                      row_counts=tuple(range(FFN_SUB, tm + 1, FFN_SUB)))
    pos2 = pos.reshape(n // rows, rows, 2).transpose(0, 2, 1).reshape(n // rows, 1, 2 * rows)
    return combine_ln(xf, ys, pos2, route, ln_g, ln_b, alpha=alpha, rows=rows)


def kernel(x, w_in, sb_norm_g, conv_w, gla_w_up, gla_b, gla_norm_g, w_out, ln1_g, ln1_b,
           ffn_w_gate, ffn_w_up, ffn_w_down, moe_w_router, moe_w_gate, moe_w_up, moe_w_down,
           ln2_g, ln2_b):
    batch, seq, d = x.shape
    depth = w_in.shape[0]
    n = batch * seq
    alpha = (2 * depth) ** 0.25

    xf = x.reshape(n, d)
    xb = xf
    w_r_all = lax.optimization_barrier(w_in[:, :, OFF_G_R:])
    w_in_t = jnp.swapaxes(w_in, 1, 2)
    for layer in range(depth):
        w_r_pad = jnp.pad(w_r_all[layer], ((0, 0), (0, LANES - GLA_RANK))).astype(BF16)
        h, h_r = in_proj(xb, w_in_t, w_r_pad, layer, tm=IN_PROJ_TM, tn=IN_PROJ_TN)
        o_sb = sb_attention(h, sb_norm_g[layer], batch=batch, seq=seq)
        o_cv = gated_conv(h, conv_w[layer], batch=batch, seq=seq)
        w_up_pad = jnp.pad(gla_w_up[layer], ((0, LANES - GLA_RANK), (0, 0))).astype(BF16)
        o_gla = gla(h, h_r, w_up_pad, gla_b[layer], gla_norm_g[layer], batch=batch, seq=seq)
        j = layer // 2
        if layer % 2 == 1:
            xf, xb, route = outproj_ln(o_sb, o_cv, o_gla, w_out, layer, xf, ln1_g[layer], ln1_b[layer],
                                       moe_w_router[j], alpha=alpha, tm=ROW_TILE)
            xf, xb = moe_block(xf, xb, route, moe_w_gate, moe_w_up, moe_w_down, j,
                               ln2_g[layer], ln2_b[layer], alpha=alpha, tm=FFN_TM, tf=FFN_TF, rows=ROW_TILE)
        else:
            xf, xb = outproj_ln(o_sb, o_cv, o_gla, w_out, layer, xf, ln1_g[layer], ln1_b[layer], None,
                                alpha=alpha, tm=ROW_TILE)
            n_tiles = n // FFN_TM
            f = ffn(xb, ffn_w_gate, ffn_w_up, ffn_w_down, j, jnp.zeros((n_tiles,), jnp.int32),
                    jnp.full((n_tiles,), FFN_TM, jnp.int32), tm=FFN_TM, tf=FFN_TF, row_counts=(FFN_TM,))
            xf, xb = residual_ln(xf, f, ln2_g[layer], ln2_b[layer], alpha=alpha, tm=RESIDUAL_TM)
    return xf.reshape(batch, seq, d)
```
